```python
import jax, jax.numpy as jnp
from jax import lax
import numpy as np

D_MODEL = 1024
BATCH = 8
SEQ = 2048
DEPTH = 1

F32 = jnp.float32
EPS = 1e-6
MEM_LEN = 256
GRID_W = 64
CONV_W = 3
ML_HEADS = 4
ML_HEAD_DIM = 128
ML_WIDTH = ML_HEADS * ML_HEAD_DIM
ML_CHUNK = 64
NA_HEADS = 8
NA_HEAD_DIM = 64
NA_WIDTH = NA_HEADS * NA_HEAD_DIM
NA_WIN_ROWS_MAX = 8
NA_WIN_COLS = 16
XA_HEADS = 4
XA_HEAD_DIM = 128
XA_WIDTH = XA_HEADS * XA_HEAD_DIM
N_BRANCHES = 3
D_FF = 2816
IN_SPLITS = (ML_WIDTH, ML_WIDTH, ML_WIDTH, ML_WIDTH, 4 * ML_HEADS, 3 * NA_WIDTH, XA_WIDTH, N_BRANCHES * D_MODEL)
IN_WIDTH = sum(IN_SPLITS)
NEG_INIT = -1e30

kernel_name = "hybrid_mlstm_natten_memxattn_block"


def rms_norm(x, g):
    xf = x.astype(F32)
    y = xf * lax.rsqrt(jnp.mean(xf * xf, axis=-1, keepdims=True) + EPS)
    return (y * g.astype(F32)).astype(x.dtype)


def dwconv3(x, w, b):
    xp = jnp.pad(x, ((0, 0), (1, 1), (0, 0)))
    return xp[:, :-2] * w[0] + xp[:, 1:-1] * w[1] + xp[:, 2:] * w[2] + b


def mlstm_chunkwise(q, k, v, i_pre, f_pre):
    B, H, S, Dh = q.shape
    L = ML_CHUNK
    NC = S // L
    q = q.astype(F32).reshape(B, H, NC, L, Dh) * (Dh ** -0.5)
    k = k.astype(F32).reshape(B, H, NC, L, Dh)
    v = v.astype(F32).reshape(B, H, NC, L, Dh)
    log_f = jax.nn.log_sigmoid(f_pre.astype(F32)).reshape(B, H, NC, L)
    log_i = i_pre.astype(F32).reshape(B, H, NC, L)
    a = jnp.cumsum(log_f, axis=-1)
    g = a[..., -1]
    w_end = g[..., None] - a + log_i
    m_end = jnp.max(w_end, axis=-1)

    def step(carry, xs):
        C, n, m = carry
        k_c, v_c, w_c, mw_c, g_c = xs
        m_new = jnp.maximum(g_c + m, mw_c)
        decay = jnp.exp(g_c + m - m_new)
        wt = jnp.exp(w_c - m_new[..., None])
        C_new = decay[..., None, None] * C + jnp.einsum('bhl,bhld,bhle->bhde', wt, v_c, k_c)
        n_new = decay[..., None] * n + jnp.einsum('bhl,bhle->bhe', wt, k_c)
        return (C_new, n_new, m_new), (C, n, m)

    init = (jnp.zeros((B, H, Dh, Dh), F32), jnp.zeros((B, H, Dh), F32), jnp.full((B, H), NEG_INIT, F32))
    xs = (jnp.moveaxis(k, 2, 0), jnp.moveaxis(v, 2, 0), jnp.moveaxis(w_end, 2, 0),
          jnp.moveaxis(m_end, 2, 0), jnp.moveaxis(g, 2, 0))
    _, (C_prev, n_prev, m_prev) = lax.scan(step, init, xs)
    C_prev = jnp.moveaxis(C_prev, 0, 2)
    n_prev = jnp.moveaxis(n_prev, 0, 2)
    m_prev = jnp.moveaxis(m_prev, 0, 2)

    causal = jnp.tril(jnp.ones((L, L), dtype=bool))
    D = a[..., :, None] - a[..., None, :] + log_i[..., None, :]
    D = jnp.where(causal, D, -jnp.inf)
    inter_log = a + m_prev[..., None]
    m_j = jnp.maximum(inter_log, jnp.max(D, axis=-1))
    qk = jnp.einsum('bhcjd,bhcsd->bhcjs', q, k)
    P = jnp.exp(D - m_j[..., None]) * qk
    inter_w = jnp.exp(inter_log - m_j)
    num = jnp.einsum('bhcjs,bhcsd->bhcjd', P, v) + inter_w[..., None] * jnp.einsum('bhcde,bhcje->bhcjd', C_prev, q)
    den = jnp.sum(P, axis=-1) + inter_w * jnp.einsum('bhce,bhcje->bhcj', n_prev, q)
    h = num / jnp.maximum(jnp.abs(den), jnp.exp(-m_j))[..., None]
    return h.reshape(B, H, S, Dh)


def mlstm_branch(q_pre, k_pre, v, o_pre, gates, w_conv, b_conv, b_ig, b_fg, norm_g):
    B, S, _ = q_pre.shape
    qk = jax.nn.silu(dwconv3(jnp.concatenate([q_pre, k_pre], axis=-1), w_conv, b_conv))
    q, k = jnp.split(qk, 2, axis=-1)
    heads = lambda t: t.reshape(B, S, ML_HEADS, ML_HEAD_DIM).transpose(0, 2, 1, 3)
    qh, kh, vh = heads(q), heads(k), heads(v)
    gt = gates.astype(F32).reshape(B, S, 4, ML_HEADS).transpose(2, 0, 3, 1)
    b_ig = b_ig.astype(F32)
    b_fg = b_fg.astype(F32)
    i_fw = gt[0] + b_ig[0][None, :, None]
    f_fw = gt[1] + b_fg[0][None, :, None]
    i_bw = gt[2] + b_ig[1][None, :, None]
    f_bw = gt[3] + b_fg[1][None, :, None]
    h_fw = mlstm_chunkwise(qh, kh, vh, i_fw, f_fw)
    flip = lambda t: jnp.flip(t, axis=2)
    h_bw = flip(mlstm_chunkwise(flip(qh), flip(kh), flip(vh), flip(i_bw), flip(f_bw)))
    h = h_fw + h_bw
    h = h * lax.rsqrt(jnp.mean(h * h, axis=-1, keepdims=True) + EPS)
    h = h * norm_g.astype(F32).reshape(ML_HEADS, 1, ML_HEAD_DIM)
    h = h.transpose(0, 2, 1, 3).reshape(B, S, ML_WIDTH)
    return (jax.nn.sigmoid(o_pre.astype(F32)) * h).astype(q_pre.dtype)


def neighbourhood_attention(q, k, v, rpb):
    B, S, _ = q.shape
    rows = S // GRID_W
    kr = min(NA_WIN_ROWS_MAX, rows)
    grid = lambda t: t.reshape(B, rows, GRID_W, NA_HEADS, NA_HEAD_DIM).transpose(0, 3, 1, 2, 4)
    qg, kg, vg = grid(q).astype(F32), grid(k), grid(v)
    r = jnp.arange(rows)
    row_start = jnp.clip(r - kr // 2, 0, rows - kr)
    row_idx = row_start[:, None] + jnp.arange(kr)[None, :]
    k_band = kg[:, :, row_idx].astype(F32)
    v_band = vg[:, :, row_idx].astype(F32)
    c = jnp.arange(GRID_W)
    col_start = jnp.clip(c - NA_WIN_COLS // 2, 0, GRID_W - NA_WIN_COLS)
    col_ok = (c[None, :] >= col_start[:, None]) & (c[None, :] < col_start[:, None] + NA_WIN_COLS)
    dr = row_idx - r[:, None] + (NA_WIN_ROWS_MAX - 1)
    dc = jnp.clip(c[None, :] - c[:, None] + (NA_WIN_COLS - 1), 0, 2 * NA_WIN_COLS - 2)
    bias = rpb.astype(F32)[:, dr[:, None, :, None], dc[None, :, None, :]]
    s = jnp.einsum('bhrcd,bhrkjd->bhrckj', qg, k_band) * (NA_HEAD_DIM ** -0.5) + bias[None]
    s = jnp.where(col_ok[None, None, None, :, None, :], s, -jnp.inf)
    p = jax.nn.softmax(s.reshape(B, NA_HEADS, rows, GRID_W, kr * GRID_W), axis=-1).reshape(s.shape)
    o = jnp.einsum('bhrckj,bhrkjd->bhrcd', p, v_band)
    return o.transpose(0, 2, 3, 1, 4).reshape(B, S, NA_WIDTH).astype(q.dtype)


def memory_cross_attention(q, mem_k, mem_v):
    B, S, _ = q.shape
    M = mem_k.shape[1]
    qh = q.reshape(B, S, XA_HEADS, XA_HEAD_DIM).astype(F32)
    kh = mem_k.reshape(B, M, XA_HEADS, XA_HEAD_DIM).astype(F32)
    vh = mem_v.reshape(B, M, XA_HEADS, XA_HEAD_DIM).astype(F32)
    s = jnp.einsum('bshd,bmhd->bhsm', qh, kh) * (XA_HEAD_DIM ** -0.5)
    p = jax.nn.softmax(s, axis=-1)
    o = jnp.einsum('bhsm,bmhd->bshd', p, vh)
    return o.reshape(B, S, XA_WIDTH).astype(q.dtype)


def conv_glu_ffn(h, w_up, w_conv, b_conv, w_down):
    a, u = jnp.split(h @ w_up, 2, axis=-1)
    return (jax.nn.gelu(dwconv3(a, w_conv, b_conv)) * u) @ w_down


def setup_inputs(seed: int = 0) -> dict:
    key = jax.random.key(seed)
    ks = jax.random.split(key, 24)

    def dense(k, shape, fan_in):
        return jax.random.normal(k, shape, F32) * (fan_in ** -0.5)

    def gain(k, shape):
        return 1.0 + 0.05 * jax.random.normal(k, shape, F32)

    def small(k, shape, scale):
        return scale * jax.random.normal(k, shape, F32)

    L = DEPTH
    return {
        "x": jax.random.normal(ks[0], (BATCH, SEQ, D_MODEL), F32),
        "mem": jax.random.normal(ks[1], (BATCH, MEM_LEN, D_MODEL), F32),
        "mix_norm_g": gain(ks[2], (L, D_MODEL)),
        "w_in": dense(ks[3], (L, D_MODEL, IN_WIDTH), D_MODEL),
        "b_ml_igate": small(ks[4], (L, 2, ML_HEADS), 0.1),
        "b_ml_fgate": jnp.linspace(3.0, 6.0, ML_HEADS, dtype=F32) + small(ks[5], (L, 2, ML_HEADS), 0.1),
        "w_ml_conv": dense(ks[6], (L, CONV_W, 2 * ML_WIDTH), CONV_W),
        "b_ml_conv": small(ks[7], (L, 2 * ML_WIDTH), 0.02),
        "ml_norm_g": gain(ks[8], (L, ML_WIDTH)),
        "na_rpb": small(ks[9], (L, NA_HEADS, 2 * NA_WIN_ROWS_MAX - 1, 2 * NA_WIN_COLS - 1), 0.1),
        "mem_norm_g": gain(ks[10], (L, D_MODEL)),
        "w_mem_kv": dense(ks[11], (L, D_MODEL, 2 * XA_WIDTH), D_MODEL),
        "b_merge_gate": small(ks[12], (L, N_BRANCHES * D_MODEL), 0.02),
        "w_br_ml": dense(ks[13], (L, ML_WIDTH, D_MODEL), ML_WIDTH),
        "w_br_na": dense(ks[14], (L, NA_WIDTH, D_MODEL), NA_WIDTH),
        "w_br_xa": dense(ks[15], (L, XA_WIDTH, D_MODEL), XA_WIDTH),
        "w_out": dense(ks[16], (L, D_MODEL, D_MODEL), D_MODEL),
        "ffn_norm_g": gain(ks[17], (L, D_MODEL)),
        "w_ffn_up": dense(ks[18], (L, D_MODEL, 2 * D_FF), D_MODEL),
        "w_ffn_conv": dense(ks[19], (L, CONV_W, D_FF), CONV_W),
        "b_ffn_conv": small(ks[20], (L, D_FF), 0.02),
        "w_ffn_down": dense(ks[21], (L, D_FF, D_MODEL), D_FF),
        "final_norm_g": gain(ks[22], (D_MODEL,)),
    }


def reference(x, mem, mix_norm_g, w_in, b_ml_igate, b_ml_fgate, w_ml_conv, b_ml_conv, ml_norm_g,
              na_rpb, mem_norm_g, w_mem_kv, b_merge_gate, w_br_ml, w_br_na, w_br_xa, w_out,
              ffn_norm_g, w_ffn_up, w_ffn_conv, b_ffn_conv, w_ffn_down, final_norm_g):
    split_points = [int(p) for p in np.cumsum(IN_SPLITS)[:-1]]
    for l in range(DEPTH):
        h = rms_norm(x, mix_norm_g[l])
        proj = h @ w_in[l]
        ml_q, ml_k, ml_v, ml_o, ml_gates, na_qkv, xa_q, merge_pre = jnp.split(proj, split_points, axis=-1)
        y_ml = mlstm_branch(ml_q, ml_k, ml_v, ml_o, ml_gates, w_ml_conv[l], b_ml_conv[l],
                            b_ml_igate[l], b_ml_fgate[l], ml_norm_g[l])
        na_q, na_k, na_v = jnp.split(na_qkv, 3, axis=-1)
        y_na = neighbourhood_attention(na_q, na_k, na_v, na_rpb[l])
        mem_k, mem_v = jnp.split(rms_norm(mem, mem_norm_g[l]) @ w_mem_kv[l], 2, axis=-1)
        y_xa = memory_cross_attention(xa_q, mem_k, mem_v)
        gates = jax.nn.sigmoid((merge_pre + b_merge_gate[l]).astype(F32)).astype(x.dtype)
        g_ml, g_na, g_xa = jnp.split(gates, N_BRANCHES, axis=-1)
        merged = g_ml * (y_ml @ w_br_ml[l]) + g_na * (y_na @ w_br_na[l]) + g_xa * (y_xa @ w_br_xa[l])
        x = x + merged @ w_out[l]
        x = x + conv_glu_ffn(rms_norm(x, ffn_norm_g[l]), w_ffn_up[l], w_ffn_conv[l], b_ffn_conv[l], w_ffn_down[l])
    return rms_norm(x, final_norm_g)
```

```python
import functools

import jax
import jax.numpy as jnp
from jax import lax
from jax.experimental import pallas as pl
from jax.experimental.pallas import tpu as pltpu

F32 = jnp.float32
BF16 = jnp.bfloat16
EPS = 1e-6
NEG_INIT = -1e30

GRID_W = 64
ML_HEADS = 4
ML_HEAD_DIM = 128
ML_WIDTH = ML_HEADS * ML_HEAD_DIM
NA_HEADS = 8
NA_HEAD_DIM = 64
NA_WIDTH = NA_HEADS * NA_HEAD_DIM
NA_WIN_ROWS = 8
NA_WIN_COLS = 16
XA_HEADS = 4
XA_HEAD_DIM = 128
XA_WIDTH = XA_HEADS * XA_HEAD_DIM

LANES = 128
ML_CHUNK = 128
VMEM_LIMIT = 56 * 1024 * 1024


def _params(dims, vmem=None):
    return pltpu.CompilerParams(dimension_semantics=dims, vmem_limit_bytes=vmem)


def _rms(x, g):
    return x * lax.rsqrt(jnp.mean(x * x, axis=-1, keepdims=True) + EPS) * g


def _inproj_kernel(x_ref, g_ref, w_ref, wg_ref, o_ref, gates_ref, h_scr):
    @pl.when(pl.program_id(1) == 0)
    def _():
        hb = _rms(x_ref[...], g_ref[...]).astype(BF16)
        h_scr[...] = hb
        gates_ref[...] = jnp.dot(hb, wg_ref[...], preferred_element_type=F32)

    o_ref[...] = jnp.dot(h_scr[...], w_ref[...], preferred_element_type=F32).astype(BF16)


def _inproj(x2, g, w_main, w_gate, tm, tn):
    T, D = x2.shape
    N = w_main.shape[1]
    return pl.pallas_call(
        _inproj_kernel,
        grid=(T // tm, N // tn),
        in_specs=[
            pl.BlockSpec((tm, D), lambda i, j: (i, 0)),
            pl.BlockSpec((1, D), lambda i, j: (0, 0)),
            pl.BlockSpec((D, tn), lambda i, j: (0, j)),
            pl.BlockSpec((D, LANES), lambda i, j: (0, 0)),
        ],
        out_specs=[
            pl.BlockSpec((tm, tn), lambda i, j: (i, j)),
            pl.BlockSpec((tm, LANES), lambda i, j: (i, 0)),
        ],
        out_shape=[
            jax.ShapeDtypeStruct((T, N), BF16),
            jax.ShapeDtypeStruct((T, LANES), F32),
        ],
        scratch_shapes=[pltpu.VMEM((tm, D), BF16)],
        compiler_params=_params(("parallel", "arbitrary"), VMEM_LIMIT),
        name="inproj",
    )(x2, g, w_main, w_gate)


def _log_sigmoid(x):
    return jnp.minimum(x, 0.0) - jnp.log(1.0 + jnp.exp(-jnp.abs(x)))


def _mlstm_kernel(q_ref, k_ref, v_ref, o_ref, gc_ref, gr_ref, bc_ref, br_ref,
                  wq_ref, wk_ref, bq_ref, bk_ref, ng_ref, y_ref,
                  q_scr, k_scr, va_scr, gcol_scr, grow_scr, hf_scr, hb_scr, sf_scr, sb_scr):
    S = q_ref.shape[1]
    L = ML_CHUNK
    NC = S // L
    Dh = ML_HEAD_DIM

    def conv_silu(x_ref, w_ref, b_ref):
        x = x_ref[0].astype(F32)
        row = lax.broadcasted_iota(jnp.int32, x.shape, 0)
        x_prev = jnp.where(row == 0, 0.0, pltpu.roll(x, 1, 0))
        x_next = jnp.where(row == S - 1, 0.0, pltpu.roll(x, S - 1, 0))
        w = w_ref[...]
        y = x_prev * w[0:1] + x * w[1:2] + x_next * w[2:3] + b_ref[...]
        return y * jax.nn.sigmoid(y)

    q_scr[...] = (conv_silu(q_ref, wq_ref, bq_ref) * (Dh ** -0.5)).astype(BF16)
    k_scr[...] = conv_silu(k_ref, wk_ref, bk_ref).astype(BF16)

    lane = lax.broadcasted_iota(jnp.int32, (S, Dh), 1)
    va_scr[:, :Dh] = v_ref[0]
    va_scr[:, Dh:] = jnp.where(lane == 0, 1.0, 0.0).astype(BF16)

    gc = gc_ref[0, 0] + bc_ref[0]
    col = lax.broadcasted_iota(jnp.int32, gc.shape, 1)
    gcol_scr[...] = jnp.where(col % 2 == 1, _log_sigmoid(gc), gc)
    gr = gr_ref[0, 0] + br_ref[0]
    rw = lax.broadcasted_iota(jnp.int32, gr.shape, 0)
    grow_scr[...] = jnp.where(rw % 2 == 1, _log_sigmoid(gr), gr)

    sf_scr[...] = jnp.zeros_like(sf_scr)
    sb_scr[...] = jnp.zeros_like(sb_scr)

    jj = lax.broadcasted_iota(jnp.int32, (L, L), 0)
    ss = lax.broadcasted_iota(jnp.int32, (L, L), 1)
    lower = ss <= jj
    upper = ss >= jj

    def chunk(c, m, st_ref, h_scr, fwd):
        off = pl.multiple_of(c * L, L)
        q = q_scr[pl.ds(off, L), :]
        k = k_scr[pl.ds(off, L), :]
        va = va_scr[pl.ds(off, L), :]
        gcol = gcol_scr[pl.ds(off, L), :]
        grow = grow_scr[:, pl.ds(off, L)]
        ci = 0 if fwd else 2
        i_col, lf_col = gcol[:, ci:ci + 1], gcol[:, ci + 1:ci + 2]
        i_row, lf_row = grow[ci:ci + 1, :], grow[ci + 1:ci + 2, :]
        mask, mask_t = (lower, upper) if fwd else (upper, lower)
        a_col = jnp.sum(jnp.where(mask, lf_row, 0.0), axis=1, keepdims=True)
        a_row = jnp.sum(jnp.where(mask_t, lf_col, 0.0), axis=0, keepdims=True)
        g = jnp.sum(lf_row, axis=1, keepdims=True)
        d = jnp.where(mask, a_col - a_row + i_row, -jnp.inf)
        w_end = g - a_col + i_col
        m_new = jnp.maximum(g + m, jnp.max(w_end, axis=0, keepdims=True))
        decay = jnp.exp(g + m - m_new)
        wt = jnp.exp(w_end - m_new)
        inter_log = a_col + m
        m_j = jnp.maximum(inter_log, jnp.max(d, axis=1, keepdims=True))
        qk = lax.dot_general(q, k, (((1,), (1,)), ((), ())), preferred_element_type=F32)
        p = jnp.exp(d - m_j) * qk
        inter_w = jnp.exp(inter_log - m_j)
        st = st_ref[...]
        nd = (jnp.dot(p.astype(BF16), va, preferred_element_type=F32)
              + inter_w * jnp.dot(q, st.astype(BF16), preferred_element_type=F32))
        num, den = nd[:, :Dh], nd[:, Dh:Dh + 1]
        h_scr[pl.ds(off, L), :] = num / jnp.maximum(jnp.abs(den), jnp.exp(-m_j))
        kw = (k.astype(F32) * wt).astype(BF16)
        st_ref[...] = decay * st + lax.dot_general(
            kw, va, (((0,), (0,)), ((), ())), preferred_element_type=F32)
        return m_new

    def body(c, carry):
        m_f, m_b = carry
        m_f = chunk(c, m_f, sf_scr, hf_scr, True)
        m_b = chunk(NC - 1 - c, m_b, sb_scr, hb_scr, False)
        return m_f, m_b

    m0 = jnp.full((1, 1), NEG_INIT, F32)
    lax.fori_loop(0, NC, body, (m0, m0))

    h = hf_scr[...] + hb_scr[...]
    h = _rms(h, ng_ref[...])
    y_ref[0] = (jax.nn.sigmoid(o_ref[0].astype(F32)) * h).astype(BF16)


def _mlstm(proj3, gcol, grow, bcol, brow, w_conv, b_conv, norm_g):
    B, S, _ = proj3.shape
    H, Dh = ML_HEADS, ML_HEAD_DIM
    blk = lambda off: pl.BlockSpec((1, S, Dh), lambda b, h: (b, 0, off + h))
    return pl.pallas_call(
        _mlstm_kernel,
        grid=(B, H),
        in_specs=[
            blk(0), blk(H), blk(2 * H), blk(3 * H),
            pl.BlockSpec((1, 1, S, 4), lambda b, h: (b, h, 0, 0)),
            pl.BlockSpec((1, 1, 4, S), lambda b, h: (b, h, 0, 0)),
            pl.BlockSpec((1, 1, 4), lambda b, h: (h, 0, 0)),
            pl.BlockSpec((1, 4, 1), lambda b, h: (h, 0, 0)),
            pl.BlockSpec((3, Dh), lambda b, h: (0, h)),
            pl.BlockSpec((3, Dh), lambda b, h: (0, H + h)),
            pl.BlockSpec((1, Dh), lambda b, h: (0, h)),
            pl.BlockSpec((1, Dh), lambda b, h: (0, H + h)),
            pl.BlockSpec((1, Dh), lambda b, h: (0, h)),
        ],
        out_specs=pl.BlockSpec((1, S, Dh), lambda b, h: (b, 0, h)),
        out_shape=jax.ShapeDtypeStruct((B, S, ML_WIDTH), BF16),
        scratch_shapes=[
            pltpu.VMEM((S, Dh), BF16), pltpu.VMEM((S, Dh), BF16), pltpu.VMEM((S, 2 * Dh), BF16),
            pltpu.VMEM((S, 4), F32), pltpu.VMEM((4, S), F32),
            pltpu.VMEM((S, Dh), F32), pltpu.VMEM((S, Dh), F32),
            pltpu.VMEM((Dh, 2 * Dh), F32), pltpu.VMEM((Dh, 2 * Dh), F32),
        ],
        compiler_params=_params(("parallel", "parallel"), VMEM_LIMIT),
        name="mlstm",
    )(proj3, proj3, proj3, proj3, gcol, grow, bcol, brow, w_conv, w_conv, b_conv, b_conv, norm_g)


def _na_kernel(q_ref, k_ref, v_ref, t_ref, y_ref):
    S = k_ref.shape[1]
    rows = S // GRID_W
    W = GRID_W
    r = pl.program_id(1)
    row_start = jnp.clip(r - NA_WIN_ROWS // 2, 0, rows - NA_WIN_ROWS)
    dr0 = row_start - r + (NA_WIN_ROWS - 1)
    band = pl.ds(pl.multiple_of(row_start * W, W), NA_WIN_ROWS * W)
    lane = lax.broadcasted_iota(jnp.int32, (W, LANES), 1)
    for hp in range(NA_HEADS // 2):
        cols = slice(hp * LANES, (hp + 1) * LANES)
        q2 = q_ref[0, :, cols] * (NA_HEAD_DIM ** -0.5)
        k2 = k_ref[0, band, cols]
        v2 = v_ref[0, band, cols]
        outs = []
        for e in range(2):
            sel = (lane < NA_HEAD_DIM) if e == 0 else (lane >= NA_HEAD_DIM)
            qm = jnp.where(sel, q2, jnp.zeros_like(q2))
            s = lax.dot_general(qm, k2, (((1,), (1,)), ((), ())), preferred_element_type=F32)
            bias = jnp.concatenate(
                [t_ref[dr0 + kk, 2 * hp + e] for kk in range(0, NA_WIN_ROWS, 2)], axis=-1)
            s = s + bias
            p = jnp.exp(s - jnp.max(s, axis=-1, keepdims=True))
            o = jnp.dot(p.astype(BF16), v2, preferred_element_type=F32)
            outs.append(o / jnp.sum(p, axis=-1, keepdims=True))
        y_ref[0, :, cols] = jnp.where(lane < NA_HEAD_DIM, outs[0], outs[1]).astype(BF16)


def _na(proj3, table, q_blk, k_blk, v_blk):
    B, S, _ = proj3.shape
    rows = S // GRID_W
    return pl.pallas_call(
        _na_kernel,
        grid=(B, rows),
        in_specs=[
            pl.BlockSpec((1, GRID_W, NA_WIDTH), lambda b, r: (b, r, q_blk)),
            pl.BlockSpec((1, S, NA_WIDTH), lambda b, r: (b, 0, k_blk)),
            pl.BlockSpec((1, S, NA_WIDTH), lambda b, r: (b, 0, v_blk)),
            pl.BlockSpec(table.shape, lambda b, r: (0, 0, 0, 0)),
        ],
        out_specs=pl.BlockSpec((1, GRID_W, NA_WIDTH), lambda b, r: (b, r, 0)),
        out_shape=jax.ShapeDtypeStruct((B, S, NA_WIDTH), BF16),
        compiler_params=_params(("parallel", "arbitrary"), VMEM_LIMIT),
        name="natten",
    )(proj3, proj3, proj3, table)


def _na_bias_table(rpb):
    c = jnp.arange(GRID_W)
    dc = jnp.clip(c[None, :] - c[:, None] + (NA_WIN_COLS - 1), 0, 2 * NA_WIN_COLS - 2)
    col_start = jnp.clip(c - NA_WIN_COLS // 2, 0, GRID_W - NA_WIN_COLS)
    col_ok = (c[None, :] >= col_start[:, None]) & (c[None, :] < col_start[:, None] + NA_WIN_COLS)
    t = jnp.where(col_ok, rpb.astype(F32)[:, :, dc], -jnp.inf)
    t2 = jnp.concatenate([t[:, :-1], t[:, 1:]], axis=-1)
    return jnp.transpose(t2, (1, 0, 2, 3))


def _memkv_kernel(m_ref, g_ref, w_ref, o_ref):
    hb = _rms(m_ref[...], g_ref[...]).astype(BF16)
    o_ref[...] = jnp.dot(hb, w_ref[...], preferred_element_type=F32).astype(BF16)


def _memkv(mem2, g, w, tm):
    T, D = mem2.shape
    N = w.shape[1]
    return pl.pallas_call(
        _memkv_kernel,
        grid=(T // tm,),
        in_specs=[
            pl.BlockSpec((tm, D), lambda i: (i, 0)),
            pl.BlockSpec((1, D), lambda i: (0, 0)),
            pl.BlockSpec((D, N), lambda i: (0, 0)),
        ],
        out_specs=pl.BlockSpec((tm, N), lambda i: (i, 0)),
        out_shape=jax.ShapeDtypeStruct((T, N), BF16),
        compiler_params=_params(("parallel",), VMEM_LIMIT),
        name="memkv",
    )(mem2, g, w)


def _xa_kernel(q_ref, kv_ref, y_ref):
    Dh = XA_HEAD_DIM
    for h in range(XA_HEADS):
        q = q_ref[0, :, h * Dh:(h + 1) * Dh]
        k = kv_ref[0, :, h * Dh:(h + 1) * Dh]
        v = kv_ref[0, :, XA_WIDTH + h * Dh:XA_WIDTH + (h + 1) * Dh]
        s = lax.dot_general(q, k, (((1,), (1,)), ((), ())), preferred_element_type=F32) * (Dh ** -0.5)
        p = jnp.exp(s - jnp.max(s, axis=-1, keepdims=True))
        o = jnp.dot(p.astype(BF16), v, preferred_element_type=F32)
        y_ref[0, :, h * Dh:(h + 1) * Dh] = (o / jnp.sum(p, axis=-1, keepdims=True)).astype(BF16)


def _xa(proj3, kv3, q_blk, tq):
    B, S, _ = proj3.shape
    M = kv3.shape[1]
    return pl.pallas_call(
        _xa_kernel,
        grid=(B, S // tq),
        in_specs=[
            pl.BlockSpec((1, tq, XA_WIDTH), lambda b, i: (b, i, q_blk)),
            pl.BlockSpec((1, M, 2 * XA_WIDTH), lambda b, i: (b, 0, 0)),
        ],
        out_specs=pl.BlockSpec((1, tq, XA_WIDTH), lambda b, i: (b, i, 0)),
        out_shape=jax.ShapeDtypeStruct((B, S, XA_WIDTH), BF16),
        compiler_params=_params(("parallel", "parallel"), VMEM_LIMIT),
        name="memxattn",
    )(proj3, kv3)


def _merge_kernel(x_ref, yml_ref, yna_ref, yxa_ref, p0_ref, p1_ref, p2_ref, bg_ref,
                  wml_ref, wna_ref, wxa_ref, wo_ref, o_ref):
    D = x_ref.shape[1]
    merged = None
    for n, (y_ref, p_ref, w_ref) in enumerate(
            ((yml_ref, p0_ref, wml_ref), (yna_ref, p1_ref, wna_ref), (yxa_ref, p2_ref, wxa_ref))):
        gate = jax.nn.sigmoid(p_ref[...].astype(F32) + bg_ref[:, n * D:(n + 1) * D])
        term = gate * jnp.dot(y_ref[...], w_ref[...], preferred_element_type=F32)
        merged = term if merged is None else merged + term
    o_ref[...] = x_ref[...] + jnp.dot(merged.astype(BF16), wo_ref[...], preferred_element_type=F32)


def _merge(x2, yml, yna, yxa, proj2, b_gate, wml, wna, wxa, wo, gate_blk, tm):
    T, D = x2.shape
    row = lambda w: pl.BlockSpec((tm, w), lambda i: (i, 0))
    full = lambda a: pl.BlockSpec(a.shape, lambda i: (0, 0))
    return pl.pallas_call(
        _merge_kernel,
        grid=(T // tm,),
        in_specs=[
            row(D), row(ML_WIDTH), row(NA_WIDTH), row(XA_WIDTH),
            pl.BlockSpec((tm, D), lambda i: (i, gate_blk)),
            pl.BlockSpec((tm, D), lambda i: (i, gate_blk + 1)),
            pl.BlockSpec((tm, D), lambda i: (i, gate_blk + 2)),
            full(b_gate), full(wml), full(wna), full(wxa), full(wo),
        ],
        out_specs=row(D),
        out_shape=jax.ShapeDtypeStruct((T, D), F32),
        compiler_params=_params(("parallel",), VMEM_LIMIT),
        name="merge",
    )(x2, yml, yna, yxa, proj2, proj2, proj2, b_gate, wml, wna, wxa, wo)


def _gelu_tanh(x):
    return 0.5 * x * (1.0 + jnp.tanh(0.7978845608028654 * (x + 0.044715 * (x * x * x))))


def _ffn_kernel(x_ref, g_ref, wa_ref, wu_ref, wc_ref, bc_ref, wd_ref, gf_ref, o_ref, h_scr):
    j = pl.program_id(1)
    S = x_ref.shape[1]

    @pl.when(j == 0)
    def _():
        x = x_ref[0]
        h_scr[...] = _rms(x, g_ref[...]).astype(BF16)
        o_ref[0] = x

    h = h_scr[...]
    a = jnp.dot(h, wa_ref[...], preferred_element_type=F32)
    u = jnp.dot(h, wu_ref[...], preferred_element_type=F32)
    row = lax.broadcasted_iota(jnp.int32, a.shape, 0)
    a_prev = jnp.where(row == 0, 0.0, pltpu.roll(a, 1, 0))
    a_next = jnp.where(row == S - 1, 0.0, pltpu.roll(a, S - 1, 0))
    w = wc_ref[...]
    conv = a_prev * w[0:1] + a * w[1:2] + a_next * w[2:3] + bc_ref[...]
    act = (_gelu_tanh(conv) * u).astype(BF16)
    o_ref[0] += jnp.dot(act, wd_ref[...], preferred_element_type=F32)

    @pl.when(j == pl.num_programs(1) - 1)
    def _():
        o_ref[0] = _rms(o_ref[0], gf_ref[...])


def _ffn(x3, g, w_up, w_conv, b_conv, w_down, g_final, tf):
    B, S, D = x3.shape
    FF = w_down.shape[0]
    nf = FF // tf
    return pl.pallas_call(
        _ffn_kernel,
        grid=(B, nf),
        in_specs=[
            pl.BlockSpec((1, S, D), lambda b, j: (b, 0, 0)),
            pl.BlockSpec((1, D), lambda b, j: (0, 0)),
            pl.BlockSpec((D, tf), lambda b, j: (0, j)),
            pl.BlockSpec((D, tf), lambda b, j: (0, nf + j)),
            pl.BlockSpec((3, tf), lambda b, j: (0, j)),
            pl.BlockSpec((1, tf), lambda b, j: (0, j)),
            pl.BlockSpec((tf, D), lambda b, j: (j, 0)),
            pl.BlockSpec((1, D), lambda b, j: (0, 0)),
        ],
        out_specs=pl.BlockSpec((1, S, D), lambda b, j: (b, 0, 0)),
        out_shape=jax.ShapeDtypeStruct((B, S, D), F32),
        scratch_shapes=[pltpu.VMEM((S, D), BF16)],
        compiler_params=_params(("parallel", "arbitrary"), VMEM_LIMIT),
        name="ffn",
    )(x3, g, w_up, w_up, w_conv, b_conv, w_down, g_final)


def kernel(x, mem, mix_norm_g, w_in, b_ml_igate, b_ml_fgate, w_ml_conv, b_ml_conv, ml_norm_g, na_rpb, mem_norm_g, w_mem_kv, b_merge_gate, w_br_ml, w_br_na, w_br_xa, w_out, ffn_norm_g, w_ffn_up, w_ffn_conv, b_ffn_conv, w_ffn_down, final_norm_g):
    B, S, D = x.shape
    depth = w_in.shape[0]
    H = ML_HEADS
    M = mem.shape[1]
    T = B * S
    assert S % ML_CHUNK == 0 and S % GRID_W == 0 and S // GRID_W >= NA_WIN_ROWS
    n_gate = 4 * H
    g0 = 4 * ML_WIDTH
    row2 = lambda v: v.reshape(1, -1).astype(F32)

    for l in range(depth):
        w_main = jnp.concatenate([w_in[l][:, :g0], w_in[l][:, g0 + n_gate:]], axis=1).astype(BF16)
        w_gate = jnp.pad(w_in[l][:, g0:g0 + n_gate], ((0, 0), (0, LANES - n_gate))).astype(BF16)

        proj2, gates = _inproj(x.reshape(T, D), row2(mix_norm_g[l]), w_main, w_gate, tm=1024, tn=1024)
        proj3 = proj2.reshape(B, S, -1)

        g4 = gates[:, :n_gate].reshape(B, S, 4, H)
        gcol = jnp.transpose(g4, (0, 3, 1, 2))
        grow = jnp.transpose(g4, (0, 3, 2, 1))
        bias = jnp.stack([b_ml_igate[l][0], b_ml_fgate[l][0], b_ml_igate[l][1], b_ml_fgate[l][1]],
                         axis=-1).astype(F32)
        y_ml = _mlstm(proj3, gcol, grow, bias[:, None, :], bias[:, :, None],
                      w_ml_conv[l].astype(F32), row2(b_ml_conv[l]), row2(ml_norm_g[l]))

        nb = (4 * ML_WIDTH) // NA_WIDTH
        y_na = _na(proj3, _na_bias_table(na_rpb[l]), nb, nb + 1, nb + 2)

        kv = _memkv(mem.reshape(B * M, D), row2(mem_norm_g[l]), w_mem_kv[l].astype(BF16), tm=512)
        xb = (4 * ML_WIDTH + 3 * NA_WIDTH) // XA_WIDTH
        y_xa = _xa(proj3, kv.reshape(B, M, -1), xb, tq=1024)

        gate_blk = (4 * ML_WIDTH + 3 * NA_WIDTH + XA_WIDTH) // D
        x1 = _merge(x.reshape(T, D), y_ml.reshape(T, -1), y_na.reshape(T, -1), y_xa.reshape(T, -1),
                    proj2, row2(b_merge_gate[l]), w_br_ml[l].astype(BF16), w_br_na[l].astype(BF16),
                    w_br_xa[l].astype(BF16), w_out[l].astype(BF16), gate_blk, tm=512)

        last = l == depth - 1
        gf = row2(final_norm_g) if last else None
        assert last, "only the final layer fuses the output norm"
        x = _ffn(x1.reshape(B, S, D), row2(ffn_norm_g[l]), w_ffn_up[l].astype(BF16),
                 w_ffn_conv[l].astype(F32), row2(b_ffn_conv[l]), w_ffn_down[l].astype(BF16), gf, tf=256)
    return x
```

```python
import jax
import jax.numpy as jnp
from jax import lax
from jax.experimental import pallas as pl
from jax.experimental.pallas import tpu as pltpu

F32 = jnp.float32
BF16 = jnp.bfloat16
EPS = 1e-6
NEG_INIT = -1e30

GRID_W = 64
ML_HEADS = 4
ML_HEAD_DIM = 128
ML_WIDTH = ML_HEADS * ML_HEAD_DIM
NA_HEADS = 8
NA_HEAD_DIM = 64
NA_WIDTH = NA_HEADS * NA_HEAD_DIM
NA_WIN_ROWS = 8
NA_WIN_COLS = 16
XA_HEADS = 4
XA_HEAD_DIM = 128
XA_WIDTH = XA_HEADS * XA_HEAD_DIM

LANES = 128
BF16_ROWS = 16
ML_CHUNK = 256
NA_ROWS_PER_STEP = 8
VMEM_LIMIT = 56 * 1024 * 1024


def _params(dims, vmem=None):
    return pltpu.CompilerParams(dimension_semantics=dims, vmem_limit_bytes=vmem)


def _rms(x, g):
    return x * lax.rsqrt(jnp.mean(x * x, axis=-1, keepdims=True) + EPS) * g


def _inproj_kernel(x_ref, g_ref, w_ref, wg_ref, o_ref, gates_ref, h_scr):
    @pl.when(pl.program_id(1) == 0)
    def _():
        hb = _rms(x_ref[...], g_ref[...]).astype(BF16)
        h_scr[...] = hb
        gates_ref[...] = jnp.dot(hb, wg_ref[...], preferred_element_type=F32)

    o_ref[...] = jnp.dot(h_scr[...], w_ref[...], preferred_element_type=F32).astype(BF16)


def _inproj(x2, g, w_main, w_gate, tm, tn):
    T, D = x2.shape
    N = w_main.shape[1]
    return pl.pallas_call(
        _inproj_kernel,
        grid=(T // tm, N // tn),
        in_specs=[
            pl.BlockSpec((tm, D), lambda i, j: (i, 0)),
            pl.BlockSpec((1, D), lambda i, j: (0, 0)),
            pl.BlockSpec((D, tn), lambda i, j: (0, j)),
            pl.BlockSpec((D, LANES), lambda i, j: (0, 0)),
        ],
        out_specs=[
            pl.BlockSpec((tm, tn), lambda i, j: (i, j)),
            pl.BlockSpec((tm, LANES), lambda i, j: (i, 0)),
        ],
        out_shape=[
            jax.ShapeDtypeStruct((T, N), BF16),
            jax.ShapeDtypeStruct((T, LANES), F32),
        ],
        scratch_shapes=[pltpu.VMEM((tm, D), BF16)],
        compiler_params=_params(("parallel", "arbitrary"), VMEM_LIMIT),
        name="inproj",
    )(x2, g, w_main, w_gate)


def _log_sigmoid(x):
    return jnp.minimum(x, 0.0) - jnp.log(1.0 + jnp.exp(-jnp.abs(x)))


def _chunk_cumsum(x, pos, L, reverse):
    n = x.shape[1]
    d = 1
    while d < L:
        if reverse:
            x = x + jnp.where(pos < L - d, pltpu.roll(x, n - d, 1), 0.0)
        else:
            x = x + jnp.where(pos >= d, pltpu.roll(x, d, 1), 0.0)
        d *= 2
    return x


def _mlstm_kernel(q_ref, k_ref, v_ref, o_ref, gr_ref, br_ref,
                  wq_ref, wk_ref, bq_ref, bk_ref, ng_ref, y_ref,
                  q_scr, kt_scr, va_scr, g_scr, hf_scr, hb_scr, sf_scr, sb_scr):
    S = q_ref.shape[1]
    L = ML_CHUNK
    NC = S // L
    Dh = ML_HEAD_DIM

    gr = gr_ref[0, 0] + br_ref[0]
    rw = lax.broadcasted_iota(jnp.int32, gr.shape, 0)
    gr = jnp.where(rw % 2 == 1, _log_sigmoid(gr), gr)
    pos = lax.broadcasted_iota(jnp.int32, gr.shape, 1) % L
    g_scr[0:4, :] = gr
    g_scr[4:8, :] = jnp.where(rw < 2, _chunk_cumsum(gr, pos, L, False), _chunk_cumsum(gr, pos, L, True))

    def conv_silu(x_ref, w_ref, b_ref, c, off):
        x = x_ref[0, pl.ds(off, L), :].astype(F32)
        off_before = pl.multiple_of(jnp.maximum(off - BF16_ROWS, 0), BF16_ROWS)
        off_after = pl.multiple_of(jnp.minimum(off + L, S - BF16_ROWS), BF16_ROWS)
        before = x_ref[0, pl.ds(off_before, BF16_ROWS), :].astype(F32)
        after = x_ref[0, pl.ds(off_after, BF16_ROWS), :].astype(F32)
        prev_row = jnp.where(c > 0, before[BF16_ROWS - 1:BF16_ROWS, :], 0.0)
        next_row = jnp.where(c < NC - 1, after[0:1, :], 0.0)
        row = lax.broadcasted_iota(jnp.int32, x.shape, 0)
        x_prev = jnp.where(row == 0, prev_row, pltpu.roll(x, 1, 0))
        x_next = jnp.where(row == L - 1, next_row, pltpu.roll(x, L - 1, 0))
        w = w_ref[...]
        y = x_prev * w[0:1] + x * w[1:2] + x_next * w[2:3] + b_ref[...]
        return y * jax.nn.sigmoid(y)

    ones_col = jnp.where(lax.broadcasted_iota(jnp.int32, (L, Dh), 1) == 0, 1.0, 0.0).astype(BF16)

    def prep(c, _):
        off = pl.multiple_of(c * L, L)
        rows = pl.ds(off, L)
        q_scr[rows, :] = (conv_silu(q_ref, wq_ref, bq_ref, c, off) * (Dh ** -0.5)).astype(BF16)
        kt_scr[:, rows] = conv_silu(k_ref, wk_ref, bk_ref, c, off).T.astype(BF16)
        va_scr[rows, :Dh] = v_ref[0, rows, :]
        va_scr[rows, Dh:] = ones_col
        return 0

    lax.fori_loop(0, NC, prep, 0)

    sf_scr[...] = jnp.zeros_like(sf_scr)
    sb_scr[...] = jnp.zeros_like(sb_scr)

    jj = lax.broadcasted_iota(jnp.int32, (L, L), 0)
    ss = lax.broadcasted_iota(jnp.int32, (L, L), 1)

    def chunk(c, m, st_ref, h_scr, fwd):
        off = pl.multiple_of(c * L, L)
        rows = pl.ds(off, L)
        q = q_scr[rows, :]
        va = va_scr[rows, :]
        gi = 0 if fwd else 2
        i_row = g_scr[gi:gi + 1, rows]
        lf_row = g_scr[gi + 1:gi + 2, rows]
        a_row = g_scr[gi + 5:gi + 6, rows]
        mask = (ss <= jj) if fwd else (ss >= jj)
        b_row = i_row - a_row
        g = jnp.sum(lf_row, axis=1, keepdims=True)
        w_end = g + b_row
        m_new = jnp.maximum(g + m, jnp.max(w_end, axis=1, keepdims=True))
        decay = jnp.exp(g + m - m_new)
        wt = jnp.exp(w_end - m_new)
        a_col = jnp.sum(jnp.where(mask, lf_row, 0.0), axis=1, keepdims=True)
        c_col = jnp.maximum(m, jnp.max(jnp.where(mask, b_row, -jnp.inf), axis=1, keepdims=True))
        qk = jnp.dot(q, kt_scr[:, rows], preferred_element_type=F32)
        p = jnp.exp(jnp.where(mask, b_row - c_col, -jnp.inf)) * qk
        q_in = (q.astype(F32) * jnp.exp(m - c_col)).astype(BF16)
        st = st_ref[...]
        nd = (jnp.dot(p.astype(BF16), va, preferred_element_type=F32)
              + jnp.dot(q_in, st.astype(BF16), preferred_element_type=F32))
        num, den = nd[:, :Dh], nd[:, Dh:Dh + 1]
        h_scr[rows, :] = num / jnp.maximum(jnp.abs(den), jnp.exp(-(a_col + c_col)))
        kw = (kt_scr[:, rows].astype(F32) * wt).astype(BF16)
        st_ref[...] = decay * st + jnp.dot(kw, va, preferred_element_type=F32)
        return m_new

    def body(c, carry):
        m_f, m_b = carry
        m_f = chunk(c, m_f, sf_scr, hf_scr, True)
        m_b = chunk(NC - 1 - c, m_b, sb_scr, hb_scr, False)
        return m_f, m_b

    m0 = jnp.full((1, 1), NEG_INIT, F32)
    lax.fori_loop(0, NC, body, (m0, m0))

    def finish(c, _):
        rows = pl.ds(pl.multiple_of(c * L, L), L)
        h = _rms(hf_scr[rows, :] + hb_scr[rows, :], ng_ref[...])
        y_ref[0, rows, :] = (jax.nn.sigmoid(o_ref[0, rows, :].astype(F32)) * h).astype(BF16)
        return 0

    lax.fori_loop(0, NC, finish, 0)


def _mlstm(proj3, grow, brow, w_conv, b_conv, norm_g):
    B, S, _ = proj3.shape
    H, Dh = ML_HEADS, ML_HEAD_DIM
    blk = lambda off: pl.BlockSpec((1, S, Dh), lambda b, h: (b, 0, off + h))
    return pl.pallas_call(
        _mlstm_kernel,
        grid=(B, H),
        in_specs=[
            blk(0), blk(H), blk(2 * H), blk(3 * H),
            pl.BlockSpec((1, 1, 4, S), lambda b, h: (b, h, 0, 0)),
            pl.BlockSpec((1, 4, 1), lambda b, h: (h, 0, 0)),
            pl.BlockSpec((3, Dh), lambda b, h: (0, h)),
            pl.BlockSpec((3, Dh), lambda b, h: (0, H + h)),
            pl.BlockSpec((1, Dh), lambda b, h: (0, h)),
            pl.BlockSpec((1, Dh), lambda b, h: (0, H + h)),
            pl.BlockSpec((1, Dh), lambda b, h: (0, h)),
        ],
        out_specs=pl.BlockSpec((1, S, Dh), lambda b, h: (b, 0, h)),
        out_shape=jax.ShapeDtypeStruct((B, S, ML_WIDTH), BF16),
        scratch_shapes=[
            pltpu.VMEM((S, Dh), BF16), pltpu.VMEM((Dh, S), BF16), pltpu.VMEM((S, 2 * Dh), BF16),
            pltpu.VMEM((8, S), F32),
            pltpu.VMEM((S, Dh), F32), pltpu.VMEM((S, Dh), F32),
            pltpu.VMEM((Dh, 2 * Dh), F32), pltpu.VMEM((Dh, 2 * Dh), F32),
        ],
        compiler_params=_params(("parallel", "parallel"), VMEM_LIMIT),
        name="mlstm",
    )(proj3, proj3, proj3, proj3, grow, brow, w_conv, w_conv, b_conv, b_conv, norm_g)


def _na_kernel(q_ref, k_ref, v_ref, t_ref, y_ref):
    S = k_ref.shape[1]
    rows = S // GRID_W
    W = GRID_W
    n_pairs = NA_HEADS // 2
    r0 = pl.program_id(1) * NA_ROWS_PER_STEP
    low = lax.broadcasted_iota(jnp.int32, (W, LANES), 1) < NA_HEAD_DIM

    def row_body(i, _):
        r = r0 + i
        row_start = jnp.clip(r - NA_WIN_ROWS // 2, 0, rows - NA_WIN_ROWS)
        dr0 = row_start - r + (NA_WIN_ROWS - 1)
        band = pl.ds(pl.multiple_of(row_start * W, W), NA_WIN_ROWS * W)
        qrow = pl.ds(pl.multiple_of(i * W, W), W)
        scores = []
        for hp in range(n_pairs):
            cols = slice(hp * LANES, (hp + 1) * LANES)
            q2 = q_ref[0, qrow, cols] * (NA_HEAD_DIM ** -0.5)
            zero = jnp.zeros_like(q2)
            qs = jnp.concatenate([jnp.where(low, q2, zero), jnp.where(low, zero, q2)], axis=0)
            s = lax.dot_general(qs, k_ref[0, band, cols], (((1,), (1,)), ((), ())),
                                preferred_element_type=F32)
            bias = jnp.concatenate([t_ref[dr0 + kk, hp] for kk in range(0, NA_WIN_ROWS, 2)], axis=-1)
            scores.append(s + bias)
        probs, sums = [], []
        for s in scores:
            p = jnp.exp(s - jnp.max(s, axis=-1, keepdims=True))
            probs.append(p.astype(BF16))
            sums.append(jnp.sum(p, axis=-1, keepdims=True))
        for hp in range(n_pairs):
            cols = slice(hp * LANES, (hp + 1) * LANES)
            o = jnp.dot(probs[hp], v_ref[0, band, cols], preferred_element_type=F32) / sums[hp]
            y_ref[0, qrow, cols] = jnp.where(low, o[:W], o[W:]).astype(BF16)
        return 0

    lax.fori_loop(0, NA_ROWS_PER_STEP, row_body, 0)


def _na(proj3, table, q_blk, k_blk, v_blk):
    B, S, _ = proj3.shape
    rows = S // GRID_W
    rb = NA_ROWS_PER_STEP
    return pl.pallas_call(
        _na_kernel,
        grid=(B, rows // rb),
        in_specs=[
            pl.BlockSpec((1, rb * GRID_W, NA_WIDTH), lambda b, r: (b, r, q_blk)),
            pl.BlockSpec((1, S, NA_WIDTH), lambda b, r: (b, 0, k_blk)),
            pl.BlockSpec((1, S, NA_WIDTH), lambda b, r: (b, 0, v_blk)),
            pl.BlockSpec(table.shape, lambda b, r: (0, 0, 0, 0)),
        ],
        out_specs=pl.BlockSpec((1, rb * GRID_W, NA_WIDTH), lambda b, r: (b, r, 0)),
        out_shape=jax.ShapeDtypeStruct((B, S, NA_WIDTH), BF16),
        compiler_params=_params(("parallel", "arbitrary"), VMEM_LIMIT),
        name="natten",
    )(proj3, proj3, proj3, table)


def _na_bias_table(rpb):
    c = jnp.arange(GRID_W)
    dc = jnp.clip(c[None, :] - c[:, None] + (NA_WIN_COLS - 1), 0, 2 * NA_WIN_COLS - 2)
    col_start = jnp.clip(c - NA_WIN_COLS // 2, 0, GRID_W - NA_WIN_COLS)
    col_ok = (c[None, :] >= col_start[:, None]) & (c[None, :] < col_start[:, None] + NA_WIN_COLS)
    t = jnp.where(col_ok, rpb.astype(F32)[:, :, dc], -jnp.inf)
    t2 = jnp.concatenate([t[:, :-1], t[:, 1:]], axis=-1)
    t2 = jnp.transpose(t2, (1, 0, 2, 3))
    return t2.reshape(t2.shape[0], NA_HEADS // 2, 2 * GRID_W, 2 * GRID_W)


def _memkv_kernel(m_ref, g_ref, w_ref, o_ref):
    hb = _rms(m_ref[...], g_ref[...]).astype(BF16)
    o_ref[...] = jnp.dot(hb, w_ref[...], preferred_element_type=F32).astype(BF16)


def _memkv(mem2, g, w, tm):
    T, D = mem2.shape
    N = w.shape[1]
    return pl.pallas_call(
        _memkv_kernel,
        grid=(T // tm,),
        in_specs=[
            pl.BlockSpec((tm, D), lambda i: (i, 0)),
            pl.BlockSpec((1, D), lambda i: (0, 0)),
            pl.BlockSpec((D, N), lambda i: (0, 0)),
        ],
        out_specs=pl.BlockSpec((tm, N), lambda i: (i, 0)),
        out_shape=jax.ShapeDtypeStruct((T, N), BF16),
        compiler_params=_params(("parallel",), VMEM_LIMIT),
        name="memkv",
    )(mem2, g, w)


def _xa_kernel(q_ref, kv_ref, y_ref):
    Dh = XA_HEAD_DIM
    for h in range(XA_HEADS):
        q = q_ref[0, :, h * Dh:(h + 1) * Dh]
        k = kv_ref[0, :, h * Dh:(h + 1) * Dh]
        v = kv_ref[0, :, XA_WIDTH + h * Dh:XA_WIDTH + (h + 1) * Dh]
        s = lax.dot_general(q, k, (((1,), (1,)), ((), ())), preferred_element_type=F32) * (Dh ** -0.5)
        p = jnp.exp(s - jnp.max(s, axis=-1, keepdims=True))
        o = jnp.dot(p.astype(BF16), v, preferred_element_type=F32)
        y_ref[0, :, h * Dh:(h + 1) * Dh] = (o / jnp.sum(p, axis=-1, keepdims=True)).astype(BF16)


def _xa(proj3, kv3, q_blk, tq):
    B, S, _ = proj3.shape
    M = kv3.shape[1]
    return pl.pallas_call(
        _xa_kernel,
        grid=(B, S // tq),
        in_specs=[
            pl.BlockSpec((1, tq, XA_WIDTH), lambda b, i: (b, i, q_blk)),
            pl.BlockSpec((1, M, 2 * XA_WIDTH), lambda b, i: (b, 0, 0)),
        ],
        out_specs=pl.BlockSpec((1, tq, XA_WIDTH), lambda b, i: (b, i, 0)),
        out_shape=jax.ShapeDtypeStruct((B, S, XA_WIDTH), BF16),
        compiler_params=_params(("parallel", "parallel"), VMEM_LIMIT),
        name="memxattn",
    )(proj3, kv3)


def _merge_kernel(x_ref, yml_ref, yna_ref, yxa_ref, p0_ref, p1_ref, p2_ref, bg_ref,
                  wml_ref, wna_ref, wxa_ref, wo_ref, o_ref):
    D = x_ref.shape[1]
    merged = None
    for n, (y_ref, p_ref, w_ref) in enumerate(
            ((yml_ref, p0_ref, wml_ref), (yna_ref, p1_ref, wna_ref), (yxa_ref, p2_ref, wxa_ref))):
        gate = jax.nn.sigmoid(p_ref[...].astype(F32) + bg_ref[:, n * D:(n + 1) * D])
        term = gate * jnp.dot(y_ref[...], w_ref[...], preferred_element_type=F32)
        merged = term if merged is None else merged + term
    o_ref[...] = x_ref[...] + jnp.dot(merged.astype(BF16), wo_ref[...], preferred_element_type=F32)


def _merge(x2, yml, yna, yxa, proj2, b_gate, wml, wna, wxa, wo, gate_blk, tm):
    T, D = x2.shape
    row = lambda w: pl.BlockSpec((tm, w), lambda i: (i, 0))
    full = lambda a: pl.BlockSpec(a.shape, lambda i: (0, 0))
    return pl.pallas_call(
        _merge_kernel,
        grid=(T // tm,),
        in_specs=[
            row(D), row(ML_WIDTH), row(NA_WIDTH), row(XA_WIDTH),
            pl.BlockSpec((tm, D), lambda i: (i, gate_blk)),
            pl.BlockSpec((tm, D), lambda i: (i, gate_blk + 1)),
            pl.BlockSpec((tm, D), lambda i: (i, gate_blk + 2)),
            full(b_gate), full(wml), full(wna), full(wxa), full(wo),
        ],
        out_specs=row(D),
        out_shape=jax.ShapeDtypeStruct((T, D), F32),
        compiler_params=_params(("parallel",), VMEM_LIMIT),
        name="merge",
    )(x2, yml, yna, yxa, proj2, proj2, proj2, b_gate, wml, wna, wxa, wo)


def _gelu_tanh(x):
    return 0.5 * x * (1.0 + jnp.tanh(0.7978845608028654 * (x + 0.044715 * (x * x * x))))


def _ffn_kernel(x_ref, g_ref, wa_ref, wu_ref, wc_ref, bc_ref, wd_ref, gf_ref, o_ref, h_scr):
    j = pl.program_id(1)
    S = x_ref.shape[1]

    @pl.when(j == 0)
    def _():
        x = x_ref[0]
        h_scr[...] = _rms(x, g_ref[...]).astype(BF16)
        o_ref[0] = x

    h = h_scr[...]
    a = jnp.dot(h, wa_ref[...], preferred_element_type=F32)
    u = jnp.dot(h, wu_ref[...], preferred_element_type=F32)
    row = lax.broadcasted_iota(jnp.int32, a.shape, 0)
    a_prev = jnp.where(row == 0, 0.0, pltpu.roll(a, 1, 0))
    a_next = jnp.where(row == S - 1, 0.0, pltpu.roll(a, S - 1, 0))
    w = wc_ref[...]
    conv = a_prev * w[0:1] + a * w[1:2] + a_next * w[2:3] + bc_ref[...]
    act = (_gelu_tanh(conv) * u).astype(BF16)
    o_ref[0] += jnp.dot(act, wd_ref[...], preferred_element_type=F32)

    @pl.when(j == pl.num_programs(1) - 1)
    def _():
        o_ref[0] = _rms(o_ref[0], gf_ref[...])


def _ffn(x3, g, w_up, w_conv, b_conv, w_down, g_final, tf):
    B, S, D = x3.shape
    FF = w_down.shape[0]
    nf = FF // tf
    return pl.pallas_call(
        _ffn_kernel,
        grid=(B, nf),
        in_specs=[
            pl.BlockSpec((1, S, D), lambda b, j: (b, 0, 0)),
            pl.BlockSpec((1, D), lambda b, j: (0, 0)),
            pl.BlockSpec((D, tf), lambda b, j: (0, j)),
            pl.BlockSpec((D, tf), lambda b, j: (0, nf + j)),
            pl.BlockSpec((3, tf), lambda b, j: (0, j)),
            pl.BlockSpec((1, tf), lambda b, j: (0, j)),
            pl.BlockSpec((tf, D), lambda b, j: (j, 0)),
            pl.BlockSpec((1, D), lambda b, j: (0, 0)),
        ],
        out_specs=pl.BlockSpec((1, S, D), lambda b, j: (b, 0, 0)),
        out_shape=jax.ShapeDtypeStruct((B, S, D), F32),
        scratch_shapes=[pltpu.VMEM((S, D), BF16)],
        compiler_params=_params(("parallel", "arbitrary"), VMEM_LIMIT),
        name="ffn",
    )(x3, g, w_up, w_up, w_conv, b_conv, w_down, g_final)


def kernel(x, mem, mix_norm_g, w_in, b_ml_igate, b_ml_fgate, w_ml_conv, b_ml_conv, ml_norm_g, na_rpb, mem_norm_g, w_mem_kv, b_merge_gate, w_br_ml, w_br_na, w_br_xa, w_out, ffn_norm_g, w_ffn_up, w_ffn_conv, b_ffn_conv, w_ffn_down, final_norm_g):
    B, S, D = x.shape
    H = ML_HEADS
    M = mem.shape[1]
    T = B * S
    assert w_in.shape[0] == 1, "single-layer block: the FFN kernel also applies the final norm"
    assert S % ML_CHUNK == 0 and S % (GRID_W * NA_ROWS_PER_STEP) == 0
    l = 0
    n_gate = 4 * H
    g0 = 4 * ML_WIDTH
    row2 = lambda v: v.reshape(1, -1).astype(F32)

    w_main = jnp.concatenate([w_in[l][:, :g0], w_in[l][:, g0 + n_gate:]], axis=1).astype(BF16)
    w_gate = jnp.pad(w_in[l][:, g0:g0 + n_gate], ((0, 0), (0, LANES - n_gate))).astype(BF16)

    proj2, gates = _inproj(x.reshape(T, D), row2(mix_norm_g[l]), w_main, w_gate, tm=1024, tn=1024)
    proj3 = proj2.reshape(B, S, -1)

    grow = jnp.transpose(gates[:, :n_gate].reshape(B, S, 4, H), (0, 3, 2, 1))
    bias = jnp.stack([b_ml_igate[l][0], b_ml_fgate[l][0], b_ml_igate[l][1], b_ml_fgate[l][1]],
                     axis=-1).astype(F32)
    y_ml = _mlstm(proj3, grow, bias[:, :, None],
                  w_ml_conv[l].astype(F32), row2(b_ml_conv[l]), row2(ml_norm_g[l]))

    nb = (4 * ML_WIDTH) // NA_WIDTH
    y_na = _na(proj3, _na_bias_table(na_rpb[l]), nb, nb + 1, nb + 2)

    kv = _memkv(mem.reshape(B * M, D), row2(mem_norm_g[l]), w_mem_kv[l].astype(BF16), tm=512)
    xb = (4 * ML_WIDTH + 3 * NA_WIDTH) // XA_WIDTH
    y_xa = _xa(proj3, kv.reshape(B, M, -1), xb, tq=1024)

    gate_blk = (4 * ML_WIDTH + 3 * NA_WIDTH + XA_WIDTH) // D
    x1 = _merge(x.reshape(T, D), y_ml.reshape(T, -1), y_na.reshape(T, -1), y_xa.reshape(T, -1),
                proj2, row2(b_merge_gate[l]), w_br_ml[l].astype(BF16), w_br_na[l].astype(BF16),
                w_br_xa[l].astype(BF16), w_out[l].astype(BF16), gate_blk, tm=512)

    return _ffn(x1.reshape(B, S, D), row2(ffn_norm_g[l]), w_ffn_up[l].astype(BF16),
                w_ffn_conv[l].astype(F32), row2(b_ffn_conv[l]), w_ffn_down[l].astype(BF16),
                row2(final_norm_g), tf=256)
```

```python
import jax
import jax.numpy as jnp
from jax import lax
from jax.experimental import pallas as pl
from jax.experimental.pallas import tpu as pltpu

F32 = jnp.float32
BF16 = jnp.bfloat16
EPS = 1e-6
NEG_INIT = -1e30

GRID_W = 64
ML_HEADS = 4
ML_HEAD_DIM = 128
ML_WIDTH = ML_HEADS * ML_HEAD_DIM
NA_HEADS = 8
NA_HEAD_DIM = 64
NA_WIDTH = NA_HEADS * NA_HEAD_DIM
NA_WIN_ROWS = 8
NA_WIN_COLS = 16
XA_HEADS = 4
XA_HEAD_DIM = 128
XA_WIDTH = XA_HEADS * XA_HEAD_DIM

LANES = 128
BF16_ROWS = 16
ML_CHUNK = 256
NA_ROWS_PER_STEP = 8
FFN_ROW_BLOCK = 256
VMEM_LIMIT = 56 * 1024 * 1024


def _params(dims, vmem=None):
    return pltpu.CompilerParams(dimension_semantics=dims, vmem_limit_bytes=vmem)


def _rms(x, g):
    return x * lax.rsqrt(jnp.mean(x * x, axis=-1, keepdims=True) + EPS) * g


def _inproj_kernel(x_ref, g_ref, w_ref, wg_ref, o_ref, gates_ref, h_scr):
    @pl.when(pl.program_id(1) == 0)
    def _():
        hb = _rms(x_ref[...], g_ref[...]).astype(BF16)
        h_scr[...] = hb
        gt = lax.dot_general(wg_ref[...], hb, (((1,), (1,)), ((), ())), preferred_element_type=F32)
        for h in range(ML_HEADS):
            gates_ref[h] = gt[4 * h:4 * h + 4, :]

    o_ref[...] = jnp.dot(h_scr[...], w_ref[...], preferred_element_type=F32).astype(BF16)


def _inproj(x2, g, w_main, w_gate, tm, tn):
    T, D = x2.shape
    N = w_main.shape[1]
    return pl.pallas_call(
        _inproj_kernel,
        grid=(T // tm, N // tn),
        in_specs=[
            pl.BlockSpec((tm, D), lambda i, j: (i, 0)),
            pl.BlockSpec((1, D), lambda i, j: (0, 0)),
            pl.BlockSpec((D, tn), lambda i, j: (0, j)),
            pl.BlockSpec(w_gate.shape, lambda i, j: (0, 0)),
        ],
        out_specs=[
            pl.BlockSpec((tm, tn), lambda i, j: (i, j)),
            pl.BlockSpec((ML_HEADS, 4, tm), lambda i, j: (0, 0, i)),
        ],
        out_shape=[
            jax.ShapeDtypeStruct((T, N), BF16),
            jax.ShapeDtypeStruct((ML_HEADS, 4, T), F32),
        ],
        scratch_shapes=[pltpu.VMEM((tm, D), BF16)],
        compiler_params=_params(("parallel", "arbitrary"), VMEM_LIMIT),
        name="inproj",
    )(x2, g, w_main, w_gate)


def _log_sigmoid(x):
    return jnp.minimum(x, 0.0) - jnp.log(1.0 + jnp.exp(-jnp.abs(x)))


def _chunk_scan(x, pos, L, reverse, op, identity):
    n = x.shape[1]
    d = 1
    while d < L:
        if reverse:
            x = op(x, jnp.where(pos < L - d, pltpu.roll(x, n - d, 1), identity))
        else:
            x = op(x, jnp.where(pos >= d, pltpu.roll(x, d, 1), identity))
        d *= 2
    return x


def _mlstm_kernel(q_ref, k_ref, v_ref, o_ref, gr_ref, br_ref,
                  wq_ref, wk_ref, bq_ref, bk_ref, ng_ref, y_ref,
                  q_scr, kt_scr, va_scr, g_scr, hf_scr, hb_scr):
    S = q_ref.shape[1]
    L = ML_CHUNK
    NC = S // L
    Dh = ML_HEAD_DIM

    gr = gr_ref[0] + br_ref[0]
    pos = lax.broadcasted_iota(jnp.int32, (1, S), 1) % L
    g_scr[...] = jnp.zeros_like(g_scr)
    for d, rev in ((0, False), (1, True)):
        i_pre = gr[2 * d:2 * d + 1, :]
        lf = _log_sigmoid(gr[2 * d + 1:2 * d + 2, :])
        a = _chunk_scan(lf, pos, L, rev, jnp.add, 0.0)
        b = i_pre - a
        g_scr[8 * d + 0:8 * d + 1, :] = b
        g_scr[8 * d + 1:8 * d + 2, :] = a
        g_scr[8 * d + 2:8 * d + 3, :] = _chunk_scan(b, pos, L, rev, jnp.maximum, -jnp.inf)
        g_scr[8 * d + 3:8 * d + 4, :] = lf

    def conv_silu(x_ref, w_ref, b_ref, c, off):
        x = x_ref[0, pl.ds(off, L), :].astype(F32)
        off_before = pl.multiple_of(jnp.maximum(off - BF16_ROWS, 0), BF16_ROWS)
        off_after = pl.multiple_of(jnp.minimum(off + L, S - BF16_ROWS), BF16_ROWS)
        before = x_ref[0, pl.ds(off_before, BF16_ROWS), :].astype(F32)
        after = x_ref[0, pl.ds(off_after, BF16_ROWS), :].astype(F32)
        prev_row = jnp.where(c > 0, before[BF16_ROWS - 1:BF16_ROWS, :], 0.0)
        next_row = jnp.where(c < NC - 1, after[0:1, :], 0.0)
        row = lax.broadcasted_iota(jnp.int32, x.shape, 0)
        x_prev = jnp.where(row == 0, prev_row, pltpu.roll(x, 1, 0))
        x_next = jnp.where(row == L - 1, next_row, pltpu.roll(x, L - 1, 0))
        w = w_ref[...]
        y = x_prev * w[0:1] + x * w[1:2] + x_next * w[2:3] + b_ref[...]
        return y * jax.nn.sigmoid(y)

    def prep(c, _):
        off = pl.multiple_of(c * L, L)
        rows = pl.ds(off, L)
        q_scr[rows, :] = (conv_silu(q_ref, wq_ref, bq_ref, c, off) * (Dh ** -0.5)).astype(BF16)
        kt_scr[:, rows] = conv_silu(k_ref, wk_ref, bk_ref, c, off).T.astype(BF16)
        va_scr[rows, :Dh] = v_ref[0, rows, :]
        va_scr[rows, Dh:] = jnp.ones((L, Dh), BF16)
        return 0

    lax.fori_loop(0, NC, prep, 0)

    jj = lax.broadcasted_iota(jnp.int32, (L, L), 0)
    ss = lax.broadcasted_iota(jnp.int32, (L, L), 1)

    def chunk(c, m, st, h_scr, fwd):
        rows = slice(c * L, (c + 1) * L)
        q = q_scr[rows, :]
        va = va_scr[rows, :]
        gi = 0 if fwd else 8
        b_row = g_scr[gi:gi + 1, rows]
        a_row = g_scr[gi + 1:gi + 2, rows]
        bmax_row = g_scr[gi + 2:gi + 3, rows]
        lf_row = g_scr[gi + 3:gi + 4, rows]
        mask = (ss <= jj) if fwd else (ss >= jj)
        g = jnp.sum(lf_row, axis=1, keepdims=True)
        w_end = g + b_row
        m_new = jnp.maximum(g + m, jnp.max(w_end, axis=1, keepdims=True))
        decay = jnp.exp(g + m - m_new)
        wt = jnp.exp(w_end - m_new)
        c_row = jnp.maximum(m, bmax_row)
        per_query = jnp.concatenate(
            [c_row, jnp.exp(m - c_row), jnp.exp(-(a_row + c_row)), jnp.zeros((5, L), F32)], axis=0).T
        c_col, inter_w, floor = per_query[:, 0:1], per_query[:, 1:2], per_query[:, 2:3]
        qk = jnp.dot(q, kt_scr[:, rows], preferred_element_type=F32)
        p = jnp.exp(jnp.where(mask, b_row - c_col, -jnp.inf)) * qk
        q_in = (q.astype(F32) * inter_w).astype(BF16)
        nd = (jnp.dot(p.astype(BF16), va, preferred_element_type=F32)
              + jnp.dot(q_in, st.astype(BF16), preferred_element_type=F32))
        h_scr[rows, :] = nd[:, :Dh] / jnp.maximum(jnp.abs(nd[:, Dh:]), floor)
        kw = (kt_scr[:, rows].astype(F32) * wt).astype(BF16)
        return m_new, decay * st + jnp.dot(kw, va, preferred_element_type=F32)

    m_f = m_b = jnp.full((1, 1), NEG_INIT, F32)
    st_f = st_b = jnp.zeros((Dh, 2 * Dh), F32)
    for c in range(NC):
        m_f, st_f = chunk(c, m_f, st_f, hf_scr, True)
        m_b, st_b = chunk(NC - 1 - c, m_b, st_b, hb_scr, False)

    def finish(c, _):
        rows = pl.ds(pl.multiple_of(c * L, L), L)
        h = _rms(hf_scr[rows, :] + hb_scr[rows, :], ng_ref[...])
        y_ref[0, rows, :] = (jax.nn.sigmoid(o_ref[0, rows, :].astype(F32)) * h).astype(BF16)
        return 0

    lax.fori_loop(0, NC, finish, 0)


def _mlstm(proj3, grow, brow, w_conv, b_conv, norm_g):
    B, S, _ = proj3.shape
    H, Dh = ML_HEADS, ML_HEAD_DIM
    blk = lambda off: pl.BlockSpec((1, S, Dh), lambda b, h: (b, 0, off + h))
    return pl.pallas_call(
        _mlstm_kernel,
        grid=(B, H),
        in_specs=[
            blk(0), blk(H), blk(2 * H), blk(3 * H),
            pl.BlockSpec((1, 4, S), lambda b, h: (h, 0, b)),
            pl.BlockSpec((1, 4, 1), lambda b, h: (h, 0, 0)),
            pl.BlockSpec((3, Dh), lambda b, h: (0, h)),
            pl.BlockSpec((3, Dh), lambda b, h: (0, H + h)),
            pl.BlockSpec((1, Dh), lambda b, h: (0, h)),
            pl.BlockSpec((1, Dh), lambda b, h: (0, H + h)),
            pl.BlockSpec((1, Dh), lambda b, h: (0, h)),
        ],
        out_specs=pl.BlockSpec((1, S, Dh), lambda b, h: (b, 0, h)),
        out_shape=jax.ShapeDtypeStruct((B, S, ML_WIDTH), BF16),
        scratch_shapes=[
            pltpu.VMEM((S, Dh), BF16), pltpu.VMEM((Dh, S), BF16), pltpu.VMEM((S, 2 * Dh), BF16),
            pltpu.VMEM((16, S), F32),
            pltpu.VMEM((S, Dh), F32), pltpu.VMEM((S, Dh), F32),
        ],
        compiler_params=_params(("parallel", "parallel"), VMEM_LIMIT),
        name="mlstm",
    )(proj3, proj3, proj3, proj3, grow, brow, w_conv, w_conv, b_conv, b_conv, norm_g)


def _na_kernel(q_ref, k_ref, v_ref, t_ref, y_ref):
    S = k_ref.shape[1]
    rows = S // GRID_W
    W = GRID_W
    n_pairs = NA_HEADS // 2
    r0 = pl.program_id(1) * NA_ROWS_PER_STEP
    low = lax.broadcasted_iota(jnp.int32, (W, LANES), 1) < NA_HEAD_DIM

    def row_body(i, _):
        r = r0 + i
        row_start = jnp.clip(r - NA_WIN_ROWS // 2, 0, rows - NA_WIN_ROWS)
        dr0 = row_start - r + (NA_WIN_ROWS - 1)
        band = pl.ds(pl.multiple_of(row_start * W, W), NA_WIN_ROWS * W)
        qrow = pl.ds(pl.multiple_of(i * W, W), W)
        scores = []
        for hp in range(n_pairs):
            cols = slice(hp * LANES, (hp + 1) * LANES)
            q2 = q_ref[0, qrow, cols] * (NA_HEAD_DIM ** -0.5)
            zero = jnp.zeros_like(q2)
            qs = jnp.concatenate([jnp.where(low, q2, zero), jnp.where(low, zero, q2)], axis=0)
            s = lax.dot_general(qs, k_ref[0, band, cols], (((1,), (1,)), ((), ())),
                                preferred_element_type=F32)
            bias = jnp.concatenate([t_ref[dr0 + kk, hp] for kk in range(0, NA_WIN_ROWS, 2)], axis=-1)
            scores.append(s + bias)
        probs, sums = [], []
        for s in scores:
            p = jnp.exp(s - jnp.max(s, axis=-1, keepdims=True))
            probs.append(p.astype(BF16))
            sums.append(jnp.sum(p, axis=-1, keepdims=True))
        for hp in range(n_pairs):
            cols = slice(hp * LANES, (hp + 1) * LANES)
            o = jnp.dot(probs[hp], v_ref[0, band, cols], preferred_element_type=F32) / sums[hp]
            y_ref[0, qrow, cols] = jnp.where(low, o[:W], o[W:]).astype(BF16)
        return 0

    lax.fori_loop(0, NA_ROWS_PER_STEP, row_body, 0)


def _na(proj3, table, q_blk, k_blk, v_blk):
    B, S, _ = proj3.shape
    rows = S // GRID_W
    rb = NA_ROWS_PER_STEP
    return pl.pallas_call(
        _na_kernel,
        grid=(B, rows // rb),
        in_specs=[
            pl.BlockSpec((1, rb * GRID_W, NA_WIDTH), lambda b, r: (b, r, q_blk)),
            pl.BlockSpec((1, S, NA_WIDTH), lambda b, r: (b, 0, k_blk)),
            pl.BlockSpec((1, S, NA_WIDTH), lambda b, r: (b, 0, v_blk)),
            pl.BlockSpec(table.shape, lambda b, r: (0, 0, 0, 0)),
        ],
        out_specs=pl.BlockSpec((1, rb * GRID_W, NA_WIDTH), lambda b, r: (b, r, 0)),
        out_shape=jax.ShapeDtypeStruct((B, S, NA_WIDTH), BF16),
        compiler_params=_params(("parallel", "arbitrary"), VMEM_LIMIT),
        name="natten",
    )(proj3, proj3, proj3, table)


def _na_bias_table(rpb):
    W = GRID_W
    c = jnp.arange(W)
    col_start = jnp.clip(c - NA_WIN_COLS // 2, 0, W - NA_WIN_COLS)
    col_ok = (c[None, :] >= col_start[:, None]) & (c[None, :] < col_start[:, None] + NA_WIN_COLS)
    lead = W - NA_WIN_COLS
    padded = jnp.pad(rpb.astype(F32), ((0, 0), (0, 0), (lead, 2 * W - lead - rpb.shape[-1])))
    skew = jnp.tile(padded, (1, 1, W))[..., :W * (2 * W - 1)].reshape(*rpb.shape[:2], W, 2 * W - 1)
    t = jnp.where(col_ok, skew[..., W - 1:], -jnp.inf)
    t2 = jnp.concatenate([t[:, :-1], t[:, 1:]], axis=-1)
    t2 = jnp.transpose(t2, (1, 0, 2, 3))
    return t2.reshape(t2.shape[0], NA_HEADS // 2, 2 * GRID_W, 2 * GRID_W)


def _memkv_kernel(m_ref, g_ref, w_ref, o_ref):
    hb = _rms(m_ref[...], g_ref[...]).astype(BF16)
    o_ref[...] = jnp.dot(hb, w_ref[...], preferred_element_type=F32).astype(BF16)


def _memkv(mem2, g, w, tm):
    T, D = mem2.shape
    N = w.shape[1]
    return pl.pallas_call(
        _memkv_kernel,
        grid=(T // tm,),
        in_specs=[
            pl.BlockSpec((tm, D), lambda i: (i, 0)),
            pl.BlockSpec((1, D), lambda i: (0, 0)),
            pl.BlockSpec((D, N), lambda i: (0, 0)),
        ],
        out_specs=pl.BlockSpec((tm, N), lambda i: (i, 0)),
        out_shape=jax.ShapeDtypeStruct((T, N), BF16),
        compiler_params=_params(("parallel",), VMEM_LIMIT),
        name="memkv",
    )(mem2, g, w)


def _xa_kernel(q_ref, kv_ref, y_ref):
    Dh = XA_HEAD_DIM
    for h in range(XA_HEADS):
        q = q_ref[0, :, h * Dh:(h + 1) * Dh]
        k = kv_ref[0, :, h * Dh:(h + 1) * Dh]
        v = kv_ref[0, :, XA_WIDTH + h * Dh:XA_WIDTH + (h + 1) * Dh]
        s = lax.dot_general(q, k, (((1,), (1,)), ((), ())), preferred_element_type=F32) * (Dh ** -0.5)
        p = jnp.exp(s - jnp.max(s, axis=-1, keepdims=True))
        o = jnp.dot(p.astype(BF16), v, preferred_element_type=F32)
        y_ref[0, :, h * Dh:(h + 1) * Dh] = (o / jnp.sum(p, axis=-1, keepdims=True)).astype(BF16)


def _xa(proj3, kv3, q_blk, tq):
    B, S, _ = proj3.shape
    M = kv3.shape[1]
    return pl.pallas_call(
        _xa_kernel,
        grid=(B, S // tq),
        in_specs=[
            pl.BlockSpec((1, tq, XA_WIDTH), lambda b, i: (b, i, q_blk)),
            pl.BlockSpec((1, M, 2 * XA_WIDTH), lambda b, i: (b, 0, 0)),
        ],
        out_specs=pl.BlockSpec((1, tq, XA_WIDTH), lambda b, i: (b, i, 0)),
        out_shape=jax.ShapeDtypeStruct((B, S, XA_WIDTH), BF16),
        compiler_params=_params(("parallel", "parallel"), VMEM_LIMIT),
        name="memxattn",
    )(proj3, kv3)


def _merge_kernel(x_ref, yml_ref, yna_ref, yxa_ref, p0_ref, p1_ref, p2_ref, bg_ref,
                  wml_ref, wna_ref, wxa_ref, wo_ref, o_ref):
    D = x_ref.shape[1]
    merged = None
    for n, (y_ref, p_ref, w_ref) in enumerate(
            ((yml_ref, p0_ref, wml_ref), (yna_ref, p1_ref, wna_ref), (yxa_ref, p2_ref, wxa_ref))):
        gate = jax.nn.sigmoid(p_ref[...].astype(F32) + bg_ref[:, n * D:(n + 1) * D])
        term = gate * jnp.dot(y_ref[...], w_ref[...], preferred_element_type=F32)
        merged = term if merged is None else merged + term
    o_ref[...] = x_ref[...] + jnp.dot(merged.astype(BF16), wo_ref[...], preferred_element_type=F32)


def _merge(x2, yml, yna, yxa, proj2, b_gate, wml, wna, wxa, wo, gate_blk, tm):
    T, D = x2.shape
    row = lambda w: pl.BlockSpec((tm, w), lambda i: (i, 0))
    full = lambda a: pl.BlockSpec(a.shape, lambda i: (0, 0))
    return pl.pallas_call(
        _merge_kernel,
        grid=(T // tm,),
        in_specs=[
            row(D), row(ML_WIDTH), row(NA_WIDTH), row(XA_WIDTH),
            pl.BlockSpec((tm, D), lambda i: (i, gate_blk)),
            pl.BlockSpec((tm, D), lambda i: (i, gate_blk + 1)),
            pl.BlockSpec((tm, D), lambda i: (i, gate_blk + 2)),
            full(b_gate), full(wml), full(wna), full(wxa), full(wo),
        ],
        out_specs=row(D),
        out_shape=jax.ShapeDtypeStruct((T, D), F32),
        compiler_params=_params(("parallel",), VMEM_LIMIT),
        name="merge",
    )(x2, yml, yna, yxa, proj2, proj2, proj2, b_gate, wml, wna, wxa, wo)


def _gelu_tanh(x):
    return 0.5 * x * (1.0 + jnp.tanh(0.7978845608028654 * (x + 0.044715 * (x * x * x))))


def _ffn_kernel(x_ref, g_ref, wa_ref, wu_ref, wc_ref, bc_ref, wd_ref, gf_ref, o_ref, h_scr):
    j = pl.program_id(1)
    S = x_ref.shape[1]

    @pl.when(j == 0)
    def _():
        x = x_ref[0]
        h_scr[...] = _rms(x, g_ref[...]).astype(BF16)
        o_ref[0] = x

    RB = FFN_ROW_BLOCK
    nb = S // RB
    tf = wa_ref.shape[1]
    w = wc_ref[...]
    bc = bc_ref[...]
    row = lax.broadcasted_iota(jnp.int32, (RB, tf), 0)

    def up(r):
        h = h_scr[r * RB:(r + 1) * RB, :]
        return (jnp.dot(h, wa_ref[...], preferred_element_type=F32),
                jnp.dot(h, wu_ref[...], preferred_element_type=F32))

    cur = up(0)
    prev_row = jnp.zeros((1, tf), F32)
    for r in range(nb):
        nxt = up(r + 1) if r + 1 < nb else None
        a, u = cur
        next_row = nxt[0][0:1, :] if nxt is not None else jnp.zeros((1, tf), F32)
        a_prev = jnp.where(row == 0, prev_row, pltpu.roll(a, 1, 0))
        a_next = jnp.where(row == RB - 1, next_row, pltpu.roll(a, RB - 1, 0))
        conv = a_prev * w[0:1] + a * w[1:2] + a_next * w[2:3] + bc
        act = (_gelu_tanh(conv) * u).astype(BF16)
        o_ref[0, r * RB:(r + 1) * RB, :] += jnp.dot(act, wd_ref[...], preferred_element_type=F32)
        prev_row = a[RB - 1:RB, :]
        cur = nxt

    @pl.when(j == pl.num_programs(1) - 1)
    def _():
        o_ref[0] = _rms(o_ref[0], gf_ref[...])


def _ffn(x3, g, w_up, w_conv, b_conv, w_down, g_final, tf):
    B, S, D = x3.shape
    FF = w_down.shape[0]
    nf = FF // tf
    return pl.pallas_call(
        _ffn_kernel,
        grid=(B, nf),
        in_specs=[
            pl.BlockSpec((1, S, D), lambda b, j: (b, 0, 0)),
            pl.BlockSpec((1, D), lambda b, j: (0, 0)),
            pl.BlockSpec((D, tf), lambda b, j: (0, j)),
            pl.BlockSpec((D, tf), lambda b, j: (0, nf + j)),
            pl.BlockSpec((3, tf), lambda b, j: (0, j)),
            pl.BlockSpec((1, tf), lambda b, j: (0, j)),
            pl.BlockSpec((tf, D), lambda b, j: (j, 0)),
            pl.BlockSpec((1, D), lambda b, j: (0, 0)),
        ],
        out_specs=pl.BlockSpec((1, S, D), lambda b, j: (b, 0, 0)),
        out_shape=jax.ShapeDtypeStruct((B, S, D), F32),
        scratch_shapes=[pltpu.VMEM((S, D), BF16)],
        compiler_params=_params(("parallel", "arbitrary"), VMEM_LIMIT),
        name="ffn",
    )(x3, g, w_up, w_up, w_conv, b_conv, w_down, g_final)


def kernel(x, mem, mix_norm_g, w_in, b_ml_igate, b_ml_fgate, w_ml_conv, b_ml_conv, ml_norm_g, na_rpb, mem_norm_g, w_mem_kv, b_merge_gate, w_br_ml, w_br_na, w_br_xa, w_out, ffn_norm_g, w_ffn_up, w_ffn_conv, b_ffn_conv, w_ffn_down, final_norm_g):
    B, S, D = x.shape
    H = ML_HEADS
    M = mem.shape[1]
    T = B * S
    assert w_in.shape[0] == 1, "single-layer block: the FFN kernel also applies the final norm"
    assert S % ML_CHUNK == 0 and S % (GRID_W * NA_ROWS_PER_STEP) == 0
    l = 0
    n_gate = 4 * H
    g0 = 4 * ML_WIDTH
    row2 = lambda v: v.reshape(1, -1).astype(F32)

    w_main = jnp.concatenate([w_in[l][:, :g0], w_in[l][:, g0 + n_gate:]], axis=1).astype(BF16)
    w_gate = jnp.transpose(w_in[l][:, g0:g0 + n_gate].reshape(D, 4, H), (2, 1, 0)).reshape(n_gate, D)
    w_gate = w_gate.astype(BF16)

    proj2, grow = _inproj(x.reshape(T, D), row2(mix_norm_g[l]), w_main, w_gate, tm=1024, tn=1024)
    proj3 = proj2.reshape(B, S, -1)

    bias = jnp.stack([b_ml_igate[l][0], b_ml_fgate[l][0], b_ml_igate[l][1], b_ml_fgate[l][1]],
                     axis=-1).astype(F32)
    y_ml = _mlstm(proj3, grow, bias[:, :, None],
                  w_ml_conv[l].astype(F32), row2(b_ml_conv[l]), row2(ml_norm_g[l]))

    nb = (4 * ML_WIDTH) // NA_WIDTH
    y_na = _na(proj3, _na_bias_table(na_rpb[l]), nb, nb + 1, nb + 2)

    kv = _memkv(mem.reshape(B * M, D), row2(mem_norm_g[l]), w_mem_kv[l].astype(BF16), tm=512)
    xb = (4 * ML_WIDTH + 3 * NA_WIDTH) // XA_WIDTH
    y_xa = _xa(proj3, kv.reshape(B, M, -1), xb, tq=1024)

    gate_blk = (4 * ML_WIDTH + 3 * NA_WIDTH + XA_WIDTH) // D
    x1 = _merge(x.reshape(T, D), y_ml.reshape(T, -1), y_na.reshape(T, -1), y_xa.reshape(T, -1),
                proj2, row2(b_merge_gate[l]), w_br_ml[l].astype(BF16), w_br_na[l].astype(BF16),
                w_br_xa[l].astype(BF16), w_out[l].astype(BF16), gate_blk, tm=512)

    return _ffn(x1.reshape(B, S, D), row2(ffn_norm_g[l]), w_ffn_up[l].astype(BF16),
                w_ffn_conv[l].astype(F32), row2(b_ffn_conv[l]), w_ffn_down[l].astype(BF16),
                row2(final_norm_g), tf=256)
```

```python
import jax
import jax.numpy as jnp
from jax import lax
from jax.experimental import pallas as pl
from jax.experimental.pallas import tpu as pltpu

F32 = jnp.float32
BF16 = jnp.bfloat16
EPS = 1e-6
NEG_INIT = -1e30

GRID_W = 64
ML_HEADS = 4
ML_HEAD_DIM = 128
ML_WIDTH = ML_HEADS * ML_HEAD_DIM
NA_HEADS = 8
NA_HEAD_DIM = 64
NA_WIDTH = NA_HEADS * NA_HEAD_DIM
NA_WIN_ROWS = 8
NA_WIN_COLS = 16
XA_HEADS = 4
XA_HEAD_DIM = 128
XA_WIDTH = XA_HEADS * XA_HEAD_DIM

LANES = 128
BF16_ROWS = 16
ML_CHUNK = 256
NA_ROWS_PER_STEP = 8
VMEM_LIMIT = 56 * 1024 * 1024


def _params(dims, vmem=None):
    return pltpu.CompilerParams(dimension_semantics=dims, vmem_limit_bytes=vmem)


def _rms(x, g):
    return x * lax.rsqrt(jnp.mean(x * x, axis=-1, keepdims=True) + EPS) * g


def _inproj_kernel(x_ref, g_ref, w_ref, wg_ref, o_ref, gates_ref, h_scr):
    @pl.when(pl.program_id(1) == 0)
    def _():
        hb = _rms(x_ref[...], g_ref[...]).astype(BF16)
        h_scr[...] = hb
        gt = lax.dot_general(wg_ref[...], hb, (((1,), (1,)), ((), ())), preferred_element_type=F32)
        for h in range(ML_HEADS):
            gates_ref[h] = gt[4 * h:4 * h + 4, :]

    o_ref[...] = jnp.dot(h_scr[...], w_ref[...], preferred_element_type=F32).astype(BF16)


def _inproj(x2, g, w_main, w_gate, tm, tn):
    T, D = x2.shape
    N = w_main.shape[1]
    return pl.pallas_call(
        _inproj_kernel,
        grid=(T // tm, N // tn),
        in_specs=[
            pl.BlockSpec((tm, D), lambda i, j: (i, 0)),
            pl.BlockSpec((1, D), lambda i, j: (0, 0)),
            pl.BlockSpec((D, tn), lambda i, j: (0, j)),
            pl.BlockSpec(w_gate.shape, lambda i, j: (0, 0)),
        ],
        out_specs=[
            pl.BlockSpec((tm, tn), lambda i, j: (i, j)),
            pl.BlockSpec((ML_HEADS, 4, tm), lambda i, j: (0, 0, i)),
        ],
        out_shape=[
            jax.ShapeDtypeStruct((T, N), BF16),
            jax.ShapeDtypeStruct((ML_HEADS, 4, T), F32),
        ],
        scratch_shapes=[pltpu.VMEM((tm, D), BF16)],
        compiler_params=_params(("parallel", "arbitrary"), VMEM_LIMIT),
        name="inproj",
    )(x2, g, w_main, w_gate)


def _log_sigmoid(x):
    return jnp.minimum(x, 0.0) - jnp.log(1.0 + jnp.exp(-jnp.abs(x)))


def _chunk_scan(x, pos, L, reverse, op, identity):
    n = x.shape[1]
    d = 1
    while d < L:
        if reverse:
            x = op(x, jnp.where(pos < L - d, pltpu.roll(x, n - d, 1), identity))
        else:
            x = op(x, jnp.where(pos >= d, pltpu.roll(x, d, 1), identity))
        d *= 2
    return x


def _mlstm_kernel(q_ref, k_ref, v_ref, o_ref, gr_ref, br_ref,
                  wq_ref, wk_ref, bq_ref, bk_ref, ng_ref, y_ref,
                  q_scr, kt_scr, va_scr, g_scr, hf_scr, hb_scr):
    S = q_ref.shape[1]
    L = ML_CHUNK
    NC = S // L
    Dh = ML_HEAD_DIM

    gr = gr_ref[0] + br_ref[0]
    pos = lax.broadcasted_iota(jnp.int32, (1, S), 1) % L
    g_scr[...] = jnp.zeros_like(g_scr)
    for d, rev in ((0, False), (1, True)):
        i_pre = gr[2 * d:2 * d + 1, :]
        lf = _log_sigmoid(gr[2 * d + 1:2 * d + 2, :])
        a = _chunk_scan(lf, pos, L, rev, jnp.add, 0.0)
        b = i_pre - a
        g_scr[8 * d + 0:8 * d + 1, :] = b
        g_scr[8 * d + 1:8 * d + 2, :] = a
        g_scr[8 * d + 2:8 * d + 3, :] = _chunk_scan(b, pos, L, rev, jnp.maximum, -jnp.inf)
        g_scr[8 * d + 3:8 * d + 4, :] = lf

    def conv_silu(x_ref, w_ref, b_ref, c, off):
        x = x_ref[0, pl.ds(off, L), :].astype(F32)
        off_before = pl.multiple_of(jnp.maximum(off - BF16_ROWS, 0), BF16_ROWS)
        off_after = pl.multiple_of(jnp.minimum(off + L, S - BF16_ROWS), BF16_ROWS)
        before = x_ref[0, pl.ds(off_before, BF16_ROWS), :].astype(F32)
        after = x_ref[0, pl.ds(off_after, BF16_ROWS), :].astype(F32)
        prev_row = jnp.where(c > 0, before[BF16_ROWS - 1:BF16_ROWS, :], 0.0)
        next_row = jnp.where(c < NC - 1, after[0:1, :], 0.0)
        row = lax.broadcasted_iota(jnp.int32, x.shape, 0)
        x_prev = jnp.where(row == 0, prev_row, pltpu.roll(x, 1, 0))
        x_next = jnp.where(row == L - 1, next_row, pltpu.roll(x, L - 1, 0))
        w = w_ref[...]
        y = x_prev * w[0:1] + x * w[1:2] + x_next * w[2:3] + b_ref[...]
        return y * jax.nn.sigmoid(y)

    def prep(c, _):
        off = pl.multiple_of(c * L, L)
        rows = pl.ds(off, L)
        q_scr[rows, :] = (conv_silu(q_ref, wq_ref, bq_ref, c, off) * (Dh ** -0.5)).astype(BF16)
        kt_scr[:, rows] = conv_silu(k_ref, wk_ref, bk_ref, c, off).T.astype(BF16)
        va_scr[rows, :Dh] = v_ref[0, rows, :]
        va_scr[rows, Dh:] = jnp.ones((L, Dh), BF16)
        return 0

    lax.fori_loop(0, NC, prep, 0)

    jj = lax.broadcasted_iota(jnp.int32, (L, L), 0)
    ss = lax.broadcasted_iota(jnp.int32, (L, L), 1)

    def chunk(c, m, st, h_scr, fwd):
        rows = slice(c * L, (c + 1) * L)
        q = q_scr[rows, :]
        va = va_scr[rows, :]
        gi = 0 if fwd else 8
        b_row = g_scr[gi:gi + 1, rows]
        a_row = g_scr[gi + 1:gi + 2, rows]
        bmax_row = g_scr[gi + 2:gi + 3, rows]
        lf_row = g_scr[gi + 3:gi + 4, rows]
        mask = (ss <= jj) if fwd else (ss >= jj)
        g = jnp.sum(lf_row, axis=1, keepdims=True)
        w_end = g + b_row
        m_new = jnp.maximum(g + m, jnp.max(w_end, axis=1, keepdims=True))
        decay = jnp.exp(g + m - m_new)
        wt = jnp.exp(w_end - m_new)
        c_row = jnp.maximum(m, bmax_row)
        per_query = jnp.concatenate(
            [c_row, jnp.exp(m - c_row), jnp.exp(-(a_row + c_row)), jnp.zeros((5, L), F32)], axis=0).T
        c_col, inter_w, floor = per_query[:, 0:1], per_query[:, 1:2], per_query[:, 2:3]
        qk = jnp.dot(q, kt_scr[:, rows], preferred_element_type=F32)
        p = jnp.exp(jnp.where(mask, b_row - c_col, -jnp.inf)) * qk
        q_in = (q.astype(F32) * inter_w).astype(BF16)
        nd = (jnp.dot(p.astype(BF16), va, preferred_element_type=F32)
              + jnp.dot(q_in, st.astype(BF16), preferred_element_type=F32))
        h_scr[rows, :] = nd[:, :Dh] / jnp.maximum(jnp.abs(nd[:, Dh:]), floor)
        kw = (kt_scr[:, rows].astype(F32) * wt).astype(BF16)
        return m_new, decay * st + jnp.dot(kw, va, preferred_element_type=F32)

    m_f = m_b = jnp.full((1, 1), NEG_INIT, F32)
    st_f = st_b = jnp.zeros((Dh, 2 * Dh), F32)
    for c in range(NC):
        m_f, st_f = chunk(c, m_f, st_f, hf_scr, True)
        m_b, st_b = chunk(NC - 1 - c, m_b, st_b, hb_scr, False)

    def finish(c, _):
        rows = pl.ds(pl.multiple_of(c * L, L), L)
        h = _rms(hf_scr[rows, :] + hb_scr[rows, :], ng_ref[...])
        y_ref[0, rows, :] = (jax.nn.sigmoid(o_ref[0, rows, :].astype(F32)) * h).astype(BF16)
        return 0

    lax.fori_loop(0, NC, finish, 0)


def _mlstm(proj3, grow, brow, w_conv, b_conv, norm_g):
    B, S, _ = proj3.shape
    H, Dh = ML_HEADS, ML_HEAD_DIM
    blk = lambda off: pl.BlockSpec((1, S, Dh), lambda b, h: (b, 0, off + h))
    return pl.pallas_call(
        _mlstm_kernel,
        grid=(B, H),
        in_specs=[
            blk(0), blk(H), blk(2 * H), blk(3 * H),
            pl.BlockSpec((1, 4, S), lambda b, h: (h, 0, b)),
            pl.BlockSpec((1, 4, 1), lambda b, h: (h, 0, 0)),
            pl.BlockSpec((3, Dh), lambda b, h: (0, h)),
            pl.BlockSpec((3, Dh), lambda b, h: (0, H + h)),
            pl.BlockSpec((1, Dh), lambda b, h: (0, h)),
            pl.BlockSpec((1, Dh), lambda b, h: (0, H + h)),
            pl.BlockSpec((1, Dh), lambda b, h: (0, h)),
        ],
        out_specs=pl.BlockSpec((1, S, Dh), lambda b, h: (b, 0, h)),
        out_shape=jax.ShapeDtypeStruct((B, S, ML_WIDTH), BF16),
        scratch_shapes=[
            pltpu.VMEM((S, Dh), BF16), pltpu.VMEM((Dh, S), BF16), pltpu.VMEM((S, 2 * Dh), BF16),
            pltpu.VMEM((16, S), F32),
            pltpu.VMEM((S, Dh), F32), pltpu.VMEM((S, Dh), F32),
        ],
        compiler_params=_params(("parallel", "parallel"), VMEM_LIMIT),
        name="mlstm",
    )(proj3, proj3, proj3, proj3, grow, brow, w_conv, w_conv, b_conv, b_conv, norm_g)


def _na_kernel(q_ref, k_ref, v_ref, pat_ref, y_ref, t_scr):
    S = k_ref.shape[1]
    rows = S // GRID_W
    W = GRID_W
    n_pairs = NA_HEADS // 2
    n_dr = t_scr.shape[0]
    r0 = pl.program_id(1) * NA_ROWS_PER_STEP
    low = lax.broadcasted_iota(jnp.int32, (W, LANES), 1) < NA_HEAD_DIM

    @pl.when((pl.program_id(0) == 0) & (pl.program_id(1) == 0))
    def _():
        cq = lax.broadcasted_iota(jnp.int32, (W, LANES), 0)
        ck = lax.broadcasted_iota(jnp.int32, (W, LANES), 1) % W
        col_start = jnp.clip(cq - NA_WIN_COLS // 2, 0, W - NA_WIN_COLS)
        col_ok = (ck >= col_start) & (ck < col_start + NA_WIN_COLS)
        for d in range(n_dr):
            for h in range(NA_HEADS):
                pattern = jnp.broadcast_to(pat_ref[h, d:d + 1, :], (W, LANES))
                skewed = pltpu.roll(pattern, 0, 1, stride=1, stride_axis=0)
                t_scr[d, h // 2, (h % 2) * W:(h % 2 + 1) * W, :] = jnp.where(col_ok, skewed, -jnp.inf)

    ones = jnp.ones((NA_WIN_ROWS * W, LANES), BF16)

    def row_body(i, _):
        r = r0 + i
        row_start = jnp.clip(r - NA_WIN_ROWS // 2, 0, rows - NA_WIN_ROWS)
        dr0 = row_start - r + (NA_WIN_ROWS - 1)
        band = pl.ds(pl.multiple_of(row_start * W, W), NA_WIN_ROWS * W)
        qrow = pl.ds(pl.multiple_of(i * W, W), W)
        scores = []
        for hp in range(n_pairs):
            cols = slice(hp * LANES, (hp + 1) * LANES)
            q2 = q_ref[0, qrow, cols] * (NA_HEAD_DIM ** -0.5)
            zero = jnp.zeros_like(q2)
            qs = jnp.concatenate([jnp.where(low, q2, zero), jnp.where(low, zero, q2)], axis=0)
            s = lax.dot_general(qs, k_ref[0, band, cols], (((1,), (1,)), ((), ())),
                                preferred_element_type=F32)
            bias = jnp.concatenate([t_scr[dr0 + kk, hp] for kk in range(0, NA_WIN_ROWS, 2)], axis=-1)
            scores.append(s + bias)
        probs = [jnp.exp(s - jnp.max(s, axis=-1, keepdims=True)).astype(BF16) for s in scores]
        for hp in range(n_pairs):
            cols = slice(hp * LANES, (hp + 1) * LANES)
            v_aug = jnp.concatenate([v_ref[0, band, cols], ones], axis=1)
            o = jnp.dot(probs[hp], v_aug, preferred_element_type=F32)
            o = o[:, :LANES] / o[:, LANES:]
            y_ref[0, qrow, cols] = jnp.where(low, o[:W], o[W:]).astype(BF16)
        return 0

    lax.fori_loop(0, NA_ROWS_PER_STEP, row_body, 0, unroll=2)


def _na(proj3, patterns, q_blk, k_blk, v_blk):
    B, S, _ = proj3.shape
    rows = S // GRID_W
    rb = NA_ROWS_PER_STEP
    n_dr = patterns.shape[1]
    return pl.pallas_call(
        _na_kernel,
        grid=(B, rows // rb),
        in_specs=[
            pl.BlockSpec((1, rb * GRID_W, NA_WIDTH), lambda b, r: (b, r, q_blk)),
            pl.BlockSpec((1, S, NA_WIDTH), lambda b, r: (b, 0, k_blk)),
            pl.BlockSpec((1, S, NA_WIDTH), lambda b, r: (b, 0, v_blk)),
            pl.BlockSpec(patterns.shape, lambda b, r: (0, 0, 0)),
        ],
        out_specs=pl.BlockSpec((1, rb * GRID_W, NA_WIDTH), lambda b, r: (b, r, 0)),
        out_shape=jax.ShapeDtypeStruct((B, S, NA_WIDTH), BF16),
        scratch_shapes=[pltpu.VMEM((n_dr, NA_HEADS // 2, 2 * GRID_W, LANES), F32)],
        compiler_params=_params(("arbitrary", "arbitrary"), VMEM_LIMIT),
        name="natten",
    )(proj3, proj3, proj3, patterns)


def _na_bias_patterns(rpb):
    c0 = NA_WIN_COLS - 1
    r = rpb.astype(F32)
    gap = jnp.zeros(r.shape[:1] + (r.shape[1] - 1, LANES // 2 - NA_WIN_COLS - c0), F32)
    return jnp.concatenate([r[:, :-1, c0:], gap, r[:, 1:, :], gap, r[:, :-1, :c0]], axis=-1)


def _memkv_kernel(m_ref, g_ref, w_ref, o_ref):
    hb = _rms(m_ref[...], g_ref[...]).astype(BF16)
    o_ref[...] = jnp.dot(hb, w_ref[...], preferred_element_type=F32).astype(BF16)


def _memkv(mem2, g, w, tm):
    T, D = mem2.shape
    N = w.shape[1]
    return pl.pallas_call(
        _memkv_kernel,
        grid=(T // tm,),
        in_specs=[
            pl.BlockSpec((tm, D), lambda i: (i, 0)),
            pl.BlockSpec((1, D), lambda i: (0, 0)),
            pl.BlockSpec((D, N), lambda i: (0, 0)),
        ],
        out_specs=pl.BlockSpec((tm, N), lambda i: (i, 0)),
        out_shape=jax.ShapeDtypeStruct((T, N), BF16),
        compiler_params=_params(("parallel",), VMEM_LIMIT),
        name="memkv",
    )(mem2, g, w)


def _xa_kernel(q_ref, kv_ref, y_ref):
    Dh = XA_HEAD_DIM
    for h in range(XA_HEADS):
        q = q_ref[0, :, h * Dh:(h + 1) * Dh]
        k = kv_ref[0, :, h * Dh:(h + 1) * Dh]
        v = kv_ref[0, :, XA_WIDTH + h * Dh:XA_WIDTH + (h + 1) * Dh]
        s = lax.dot_general(q, k, (((1,), (1,)), ((), ())), preferred_element_type=F32) * (Dh ** -0.5)
        p = jnp.exp(s - jnp.max(s, axis=-1, keepdims=True))
        o = jnp.dot(p.astype(BF16), v, preferred_element_type=F32)
        y_ref[0, :, h * Dh:(h + 1) * Dh] = (o / jnp.sum(p, axis=-1, keepdims=True)).astype(BF16)


def _xa(proj3, kv3, q_blk, tq):
    B, S, _ = proj3.shape
    M = kv3.shape[1]
    return pl.pallas_call(
        _xa_kernel,
        grid=(B, S // tq),
        in_specs=[
            pl.BlockSpec((1, tq, XA_WIDTH), lambda b, i: (b, i, q_blk)),
            pl.BlockSpec((1, M, 2 * XA_WIDTH), lambda b, i: (b, 0, 0)),
        ],
        out_specs=pl.BlockSpec((1, tq, XA_WIDTH), lambda b, i: (b, i, 0)),
        out_shape=jax.ShapeDtypeStruct((B, S, XA_WIDTH), BF16),
        compiler_params=_params(("parallel", "parallel"), VMEM_LIMIT),
        name="memxattn",
    )(proj3, kv3)


def _merge_kernel(x_ref, yml_ref, yna_ref, yxa_ref, p0_ref, p1_ref, p2_ref, bg_ref,
                  wml_ref, wna_ref, wxa_ref, wo_ref, o_ref):
    D = x_ref.shape[1]
    merged = None
    for n, (y_ref, p_ref, w_ref) in enumerate(
            ((yml_ref, p0_ref, wml_ref), (yna_ref, p1_ref, wna_ref), (yxa_ref, p2_ref, wxa_ref))):
        gate = jax.nn.sigmoid(p_ref[...].astype(F32) + bg_ref[:, n * D:(n + 1) * D])
        term = gate * jnp.dot(y_ref[...], w_ref[...], preferred_element_type=F32)
        merged = term if merged is None else merged + term
    o_ref[...] = x_ref[...] + jnp.dot(merged.astype(BF16), wo_ref[...], preferred_element_type=F32)


def _merge(x2, yml, yna, yxa, proj2, b_gate, wml, wna, wxa, wo, gate_blk, tm):
    T, D = x2.shape
    row = lambda w: pl.BlockSpec((tm, w), lambda i: (i, 0))
    full = lambda a: pl.BlockSpec(a.shape, lambda i: (0, 0))
    return pl.pallas_call(
        _merge_kernel,
        grid=(T // tm,),
        in_specs=[
            row(D), row(ML_WIDTH), row(NA_WIDTH), row(XA_WIDTH),
            pl.BlockSpec((tm, D), lambda i: (i, gate_blk)),
            pl.BlockSpec((tm, D), lambda i: (i, gate_blk + 1)),
            pl.BlockSpec((tm, D), lambda i: (i, gate_blk + 2)),
            full(b_gate), full(wml), full(wna), full(wxa), full(wo),
        ],
        out_specs=row(D),
        out_shape=jax.ShapeDtypeStruct((T, D), F32),
        compiler_params=_params(("parallel",), VMEM_LIMIT),
        name="merge",
    )(x2, yml, yna, yxa, proj2, proj2, proj2, b_gate, wml, wna, wxa, wo)


def _gelu_tanh(x):
    return 0.5 * x * (1.0 + jnp.tanh(0.7978845608028654 * (x + 0.044715 * (x * x * x))))


def _ffn_kernel(x_ref, g_ref, wa_ref, wu_ref, wc_ref, bc_ref, wd_ref, gf_ref, o_ref, h_scr):
    j = pl.program_id(1)
    S = x_ref.shape[1]

    @pl.when(j == 0)
    def _():
        x = x_ref[0]
        h_scr[...] = _rms(x, g_ref[...]).astype(BF16)
        o_ref[0] = x

    h = h_scr[...]
    a = jnp.dot(h, wa_ref[...], preferred_element_type=F32)
    u = jnp.dot(h, wu_ref[...], preferred_element_type=F32)
    row = lax.broadcasted_iota(jnp.int32, a.shape, 0)
    a_prev = jnp.where(row == 0, 0.0, pltpu.roll(a, 1, 0))
    a_next = jnp.where(row == S - 1, 0.0, pltpu.roll(a, S - 1, 0))
    w = wc_ref[...]
    conv = a_prev * w[0:1] + a * w[1:2] + a_next * w[2:3] + bc_ref[...]
    act = (_gelu_tanh(conv) * u).astype(BF16)
    o_ref[0] += jnp.dot(act, wd_ref[...], preferred_element_type=F32)

    @pl.when(j == pl.num_programs(1) - 1)
    def _():
        o_ref[0] = _rms(o_ref[0], gf_ref[...])


def _ffn(x3, g, w_up, w_conv, b_conv, w_down, g_final, tf):
    B, S, D = x3.shape
    FF = w_down.shape[0]
    nf = FF // tf
    return pl.pallas_call(
        _ffn_kernel,
        grid=(B, nf),
        in_specs=[
            pl.BlockSpec((1, S, D), lambda b, j: (b, 0, 0)),
            pl.BlockSpec((1, D), lambda b, j: (0, 0)),
            pl.BlockSpec((D, tf), lambda b, j: (0, j)),
            pl.BlockSpec((D, tf), lambda b, j: (0, nf + j)),
            pl.BlockSpec((3, tf), lambda b, j: (0, j)),
            pl.BlockSpec((1, tf), lambda b, j: (0, j)),
            pl.BlockSpec((tf, D), lambda b, j: (j, 0)),
            pl.BlockSpec((1, D), lambda b, j: (0, 0)),
        ],
        out_specs=pl.BlockSpec((1, S, D), lambda b, j: (b, 0, 0)),
        out_shape=jax.ShapeDtypeStruct((B, S, D), F32),
        scratch_shapes=[pltpu.VMEM((S, D), BF16)],
        compiler_params=_params(("parallel", "arbitrary"), VMEM_LIMIT),
        name="ffn",
    )(x3, g, w_up, w_up, w_conv, b_conv, w_down, g_final)


def kernel(x, mem, mix_norm_g, w_in, b_ml_igate, b_ml_fgate, w_ml_conv, b_ml_conv, ml_norm_g, na_rpb, mem_norm_g, w_mem_kv, b_merge_gate, w_br_ml, w_br_na, w_br_xa, w_out, ffn_norm_g, w_ffn_up, w_ffn_conv, b_ffn_conv, w_ffn_down, final_norm_g):
    B, S, D = x.shape
    H = ML_HEADS
    M = mem.shape[1]
    T = B * S
    assert w_in.shape[0] == 1, "single-layer block: the FFN kernel also applies the final norm"
    assert S % ML_CHUNK == 0 and S % (GRID_W * NA_ROWS_PER_STEP) == 0
    l = 0
    n_gate = 4 * H
    g0 = 4 * ML_WIDTH
    row2 = lambda v: v.reshape(1, -1).astype(F32)

    w_main = jnp.concatenate([w_in[l][:, :g0], w_in[l][:, g0 + n_gate:]], axis=1).astype(BF16)
    w_gate = jnp.transpose(w_in[l][:, g0:g0 + n_gate].reshape(D, 4, H), (2, 1, 0)).reshape(n_gate, D)
    w_gate = w_gate.astype(BF16)

    proj2, grow = _inproj(x.reshape(T, D), row2(mix_norm_g[l]), w_main, w_gate, tm=1024, tn=3584)
    proj3 = proj2.reshape(B, S, -1)

    bias = jnp.stack([b_ml_igate[l][0], b_ml_fgate[l][0], b_ml_igate[l][1], b_ml_fgate[l][1]],
                     axis=-1).astype(F32)
    y_ml = _mlstm(proj3, grow, bias[:, :, None],
                  w_ml_conv[l].astype(F32), row2(b_ml_conv[l]), row2(ml_norm_g[l]))

    nb = (4 * ML_WIDTH) // NA_WIDTH
    y_na = _na(proj3, _na_bias_patterns(na_rpb[l]), nb, nb + 1, nb + 2)

    kv = _memkv(mem.reshape(B * M, D), row2(mem_norm_g[l]), w_mem_kv[l].astype(BF16), tm=512)
    xb = (4 * ML_WIDTH + 3 * NA_WIDTH) // XA_WIDTH
    y_xa = _xa(proj3, kv.reshape(B, M, -1), xb, tq=1024)

    gate_blk = (4 * ML_WIDTH + 3 * NA_WIDTH + XA_WIDTH) // D
    x1 = _merge(x.reshape(T, D), y_ml.reshape(T, -1), y_na.reshape(T, -1), y_xa.reshape(T, -1),
                proj2, row2(b_merge_gate[l]), w_br_ml[l].astype(BF16), w_br_na[l].astype(BF16),
                w_br_xa[l].astype(BF16), w_out[l].astype(BF16), gate_blk, tm=512)

    return _ffn(x1.reshape(B, S, D), row2(ffn_norm_g[l]), w_ffn_up[l].astype(BF16),
                w_ffn_conv[l].astype(F32), row2(b_ffn_conv[l]), w_ffn_down[l].astype(BF16),
                row2(final_norm_g), tf=256)
```

```python
import functools

import jax
import jax.numpy as jnp
from jax import lax
from jax.experimental import pallas as pl
from jax.experimental.pallas import tpu as pltpu

F32 = jnp.float32
BF16 = jnp.bfloat16
EPS = 1e-6
NEG_INIT = -1e30

GRID_W = 64
ML_HEADS = 4
ML_HEAD_DIM = 128
ML_WIDTH = ML_HEADS * ML_HEAD_DIM
NA_HEADS = 8
NA_HEAD_DIM = 64
NA_WIDTH = NA_HEADS * NA_HEAD_DIM
NA_WIN_ROWS = 8
NA_WIN_COLS = 16
XA_HEADS = 4
XA_HEAD_DIM = 128
XA_WIDTH = XA_HEADS * XA_HEAD_DIM

LANES = 128
BF16_ROWS = 16
ML_CHUNK = 256
NA_ROWS_PER_STEP = 8
VMEM_LIMIT = 56 * 1024 * 1024


def _params(dims, vmem=None):
    return pltpu.CompilerParams(dimension_semantics=dims, vmem_limit_bytes=vmem)


def _rms(x, g):
    return x * lax.rsqrt(jnp.mean(x * x, axis=-1, keepdims=True) + EPS) * g


def _wprep_kernel(w_ref, o_ref, *, g0, n_gate):
    o_ref[:, :g0] = w_ref[:, :g0].astype(BF16)
    o_ref[:, g0:] = w_ref[:, g0 + n_gate:].astype(BF16)


def _wprep(w, g0, n_gate, tr):
    D, N = w.shape
    return pl.pallas_call(
        functools.partial(_wprep_kernel, g0=g0, n_gate=n_gate),
        grid=(D // tr,),
        in_specs=[pl.BlockSpec((tr, N), lambda i: (i, 0))],
        out_specs=pl.BlockSpec((tr, N - n_gate), lambda i: (i, 0)),
        out_shape=jax.ShapeDtypeStruct((D, N - n_gate), BF16),
        compiler_params=_params(("parallel",), VMEM_LIMIT),
        name="wprep",
    )(w)


def _inproj_kernel(x_ref, g_ref, w_ref, wg_ref, o_ref, gates_ref, h_scr):
    @pl.when(pl.program_id(1) == 0)
    def _():
        hb = _rms(x_ref[...], g_ref[...]).astype(BF16)
        h_scr[...] = hb
        gt = lax.dot_general(wg_ref[...], hb, (((1,), (1,)), ((), ())), preferred_element_type=F32)
        for h in range(ML_HEADS):
            gates_ref[h] = gt[4 * h:4 * h + 4, :]

    o_ref[...] = jnp.dot(h_scr[...], w_ref[...], preferred_element_type=F32).astype(BF16)


def _inproj(x2, g, w_main, w_gate, tm, tn):
    T, D = x2.shape
    N = w_main.shape[1]
    return pl.pallas_call(
        _inproj_kernel,
        grid=(T // tm, N // tn),
        in_specs=[
            pl.BlockSpec((tm, D), lambda i, j: (i, 0)),
            pl.BlockSpec((1, D), lambda i, j: (0, 0)),
            pl.BlockSpec((D, tn), lambda i, j: (0, j)),
            pl.BlockSpec(w_gate.shape, lambda i, j: (0, 0)),
        ],
        out_specs=[
            pl.BlockSpec((tm, tn), lambda i, j: (i, j)),
            pl.BlockSpec((ML_HEADS, 4, tm), lambda i, j: (0, 0, i)),
        ],
        out_shape=[
            jax.ShapeDtypeStruct((T, N), BF16),
            jax.ShapeDtypeStruct((ML_HEADS, 4, T), F32),
        ],
        scratch_shapes=[pltpu.VMEM((tm, D), BF16)],
        compiler_params=_params(("parallel", "arbitrary"), VMEM_LIMIT),
        name="inproj",
    )(x2, g, w_main, w_gate)


def _log_sigmoid(x):
    return jnp.minimum(x, 0.0) - jnp.log(1.0 + jnp.exp(-jnp.abs(x)))


def _chunk_scan(x, pos, L, reverse, op, identity):
    n = x.shape[1]
    d = 1
    while d < L:
        if reverse:
            x = op(x, jnp.where(pos < L - d, pltpu.roll(x, n - d, 1), identity))
        else:
            x = op(x, jnp.where(pos >= d, pltpu.roll(x, d, 1), identity))
        d *= 2
    return x


def _mlstm_kernel(q_ref, k_ref, v_ref, o_ref, gr_ref, br_ref,
                  wq_ref, wk_ref, bq_ref, bk_ref, ng_ref, y_ref,
                  qt_scr, k_scr, vat_scr, bb_scr, hf_scr, hb_scr):
    S = q_ref.shape[1]
    L = ML_CHUNK
    NC = S // L
    Dh = ML_HEAD_DIM

    pos = lax.broadcasted_iota(jnp.int32, (NC, L), 1)
    gates = []
    for d, rev in ((0, False), (1, True)):
        i_pre = gr_ref[0, 2 * d] + br_ref[0, 2 * d:2 * d + 1, :]
        lf = _log_sigmoid(gr_ref[0, 2 * d + 1] + br_ref[0, 2 * d + 1:2 * d + 2, :])
        a = _chunk_scan(lf, pos, L, rev, jnp.add, 0.0)
        b = i_pre - a
        gates.append((b, a, _chunk_scan(b, pos, L, rev, jnp.maximum, -jnp.inf), lf))

    def conv_silu(x_ref, w_ref, b_ref, c):
        x = x_ref[0, c * L:(c + 1) * L, :].astype(F32)
        zero_row = jnp.zeros((1, Dh), F32)
        prev_row = x_ref[0, c * L - BF16_ROWS:c * L, :].astype(F32)[BF16_ROWS - 1:, :] if c > 0 else zero_row
        next_row = x_ref[0, (c + 1) * L:(c + 1) * L + BF16_ROWS, :].astype(F32)[0:1, :] if c < NC - 1 else zero_row
        row = lax.broadcasted_iota(jnp.int32, x.shape, 0)
        x_prev = jnp.where(row == 0, prev_row, pltpu.roll(x, 1, 0))
        x_next = jnp.where(row == L - 1, next_row, pltpu.roll(x, L - 1, 0))
        w = w_ref[...]
        y = x_prev * w[0:1] + x * w[1:2] + x_next * w[2:3] + b_ref[...]
        return y * jax.nn.sigmoid(y)

    vat_scr[Dh:, :] = jnp.ones((Dh, S), BF16)

    def prep(c):
        cs = slice(c * L, (c + 1) * L)
        qt_scr[:, cs] = (conv_silu(q_ref, wq_ref, bq_ref, c) * (Dh ** -0.5)).astype(BF16).T
        k_scr[cs, :] = conv_silu(k_ref, wk_ref, bk_ref, c).astype(BF16)
        vat_scr[:Dh, cs] = v_ref[0, cs, :].T
        for d in range(2):
            bb_scr[d, cs, :] = jnp.broadcast_to(gates[d][0][c:c + 1, :], (LANES, L)).T

    s_idx = lax.broadcasted_iota(jnp.int32, (L, L), 0)
    j_idx = lax.broadcasted_iota(jnp.int32, (L, L), 1)

    def chunk(c, m, st_t, h_scr, fwd):
        cs = slice(c * L, (c + 1) * L)
        d = 0 if fwd else 1
        b_row, a_row, bmax_row, lf_row = (t[c:c + 1, :] for t in gates[d])
        mask_t = (s_idx <= j_idx) if fwd else (s_idx >= j_idx)
        g = jnp.sum(lf_row, axis=1, keepdims=True)
        w_end = g + b_row
        m_new = jnp.maximum(g + m, jnp.max(w_end, axis=1, keepdims=True))
        decay = jnp.exp(g + m - m_new)
        wt = jnp.exp(w_end - m_new)
        c_row = jnp.maximum(m, bmax_row)
        inter_w = jnp.exp(m - c_row)
        floor = jnp.exp(-(a_row + c_row))
        k = k_scr[cs, :]
        q_t = qt_scr[:, cs]
        va_t = vat_scr[:, cs]
        qk_t = jnp.dot(k, q_t, preferred_element_type=F32)
        b_col = bb_scr[d, cs, :]
        expo = jnp.concatenate([b_col] * (L // LANES), axis=1) - c_row
        p_t = jnp.exp(jnp.where(mask_t, expo, -jnp.inf)) * qk_t
        q_in = (q_t.astype(F32) * inter_w).astype(BF16)
        nd_t = (jnp.dot(va_t, p_t.astype(BF16), preferred_element_type=F32)
                + jnp.dot(st_t.astype(BF16), q_in, preferred_element_type=F32))
        h_scr[:, cs] = nd_t[:Dh] / jnp.maximum(jnp.abs(nd_t[Dh:]), floor)
        vw = jnp.concatenate([(va_t[:Dh].astype(F32) * wt).astype(BF16),
                              jnp.broadcast_to(wt, (Dh, L)).astype(BF16)], axis=0)
        return m_new, decay * st_t + jnp.dot(vw, k, preferred_element_type=F32)

    def finish(c):
        cs = slice(c * L, (c + 1) * L)
        h_t = hf_scr[:, cs] + hb_scr[:, cs]
        h_t = h_t * lax.rsqrt(jnp.mean(h_t * h_t, axis=0, keepdims=True) + EPS)
        y = h_t.T * ng_ref[...] * jax.nn.sigmoid(o_ref[0, cs, :].astype(F32))
        y_ref[0, cs, :] = y.astype(BF16)

    m_f = m_b = jnp.full((1, 1), NEG_INIT, F32)
    st_f = st_b = jnp.zeros((2 * Dh, Dh), F32)
    prep(0)
    prep(NC - 1)
    for c in range(NC):
        if c + 1 < NC - 1 - (c + 1):
            prep(c + 1)
            prep(NC - 2 - c)
        elif c + 1 == NC - 1 - (c + 1):
            prep(c + 1)
        m_f, st_f = chunk(c, m_f, st_f, hf_scr, True)
        m_b, st_b = chunk(NC - 1 - c, m_b, st_b, hb_scr, False)
        if c >= NC - 1 - c:
            finish(c)
            if c != NC - 1 - c:
                finish(NC - 1 - c)


def _mlstm(proj3, grow, brow, w_conv, b_conv, norm_g):
    B, S, _ = proj3.shape
    H, Dh = ML_HEADS, ML_HEAD_DIM
    blk = lambda off: pl.BlockSpec((1, S, Dh), lambda b, h: (b, 0, off + h))
    return pl.pallas_call(
        _mlstm_kernel,
        grid=(B, H),
        in_specs=[
            blk(0), blk(H), blk(2 * H), blk(3 * H),
            pl.BlockSpec((1, 4, S // ML_CHUNK, ML_CHUNK), lambda b, h: (h, 0, b, 0)),
            pl.BlockSpec((1, 4, 1), lambda b, h: (h, 0, 0)),
            pl.BlockSpec((3, Dh), lambda b, h: (0, h)),
            pl.BlockSpec((3, Dh), lambda b, h: (0, H + h)),
            pl.BlockSpec((1, Dh), lambda b, h: (0, h)),
            pl.BlockSpec((1, Dh), lambda b, h: (0, H + h)),
            pl.BlockSpec((1, Dh), lambda b, h: (0, h)),
        ],
        out_specs=pl.BlockSpec((1, S, Dh), lambda b, h: (b, 0, h)),
        out_shape=jax.ShapeDtypeStruct((B, S, ML_WIDTH), BF16),
        scratch_shapes=[
            pltpu.VMEM((Dh, S), BF16), pltpu.VMEM((S, Dh), BF16), pltpu.VMEM((2 * Dh, S), BF16),
            pltpu.VMEM((2, S, LANES), F32),
            pltpu.VMEM((Dh, S), F32), pltpu.VMEM((Dh, S), F32),
        ],
        compiler_params=_params(("parallel", "parallel"), VMEM_LIMIT),
        name="mlstm",
    )(proj3, proj3, proj3, proj3, grow, brow, w_conv, w_conv, b_conv, b_conv, norm_g)


def _na_kernel(q_ref, k_ref, v_ref, pat_ref, y_ref, t_scr):
    S = k_ref.shape[1]
    rows = S // GRID_W
    W = GRID_W
    n_pairs = NA_HEADS // 2
    n_dr = t_scr.shape[0]
    r0 = pl.program_id(1) * NA_ROWS_PER_STEP
    low = lax.broadcasted_iota(jnp.int32, (W, LANES), 1) < NA_HEAD_DIM

    @pl.when((pl.program_id(0) == 0) & (pl.program_id(1) == 0))
    def _():
        cq = lax.broadcasted_iota(jnp.int32, (W, LANES), 0)
        ck = lax.broadcasted_iota(jnp.int32, (W, LANES), 1) % W
        col_start = jnp.clip(cq - NA_WIN_COLS // 2, 0, W - NA_WIN_COLS)
        col_ok = (ck >= col_start) & (ck < col_start + NA_WIN_COLS)
        for d in range(n_dr):
            for h in range(NA_HEADS):
                pattern = jnp.broadcast_to(pat_ref[h, d:d + 1, :], (W, LANES))
                skewed = pltpu.roll(pattern, 0, 1, stride=1, stride_axis=0)
                t_scr[d, h // 2, (h % 2) * W:(h % 2 + 1) * W, :] = jnp.where(col_ok, skewed, -jnp.inf)

    ones = jnp.ones((NA_WIN_ROWS * W, LANES), BF16)

    def row_body(i, _):
        r = r0 + i
        row_start = jnp.clip(r - NA_WIN_ROWS // 2, 0, rows - NA_WIN_ROWS)
        dr0 = row_start - r + (NA_WIN_ROWS - 1)
        band = pl.ds(pl.multiple_of(row_start * W, W), NA_WIN_ROWS * W)
        qrow = pl.ds(pl.multiple_of(i * W, W), W)
        scores = []
        for hp in range(n_pairs):
            cols = slice(hp * LANES, (hp + 1) * LANES)
            q2 = q_ref[0, qrow, cols] * (NA_HEAD_DIM ** -0.5)
            zero = jnp.zeros_like(q2)
            qs = jnp.concatenate([jnp.where(low, q2, zero), jnp.where(low, zero, q2)], axis=0)
            s = lax.dot_general(qs, k_ref[0, band, cols], (((1,), (1,)), ((), ())),
                                preferred_element_type=F32)
            bias = jnp.concatenate([t_scr[dr0 + kk, hp] for kk in range(0, NA_WIN_ROWS, 2)], axis=-1)
            scores.append(s + bias)
        probs = [jnp.exp(s - jnp.max(s, axis=-1, keepdims=True)).astype(BF16) for s in scores]
        for hp in range(n_pairs):
            cols = slice(hp * LANES, (hp + 1) * LANES)
            v_aug = jnp.concatenate([v_ref[0, band, cols], ones], axis=1)
            o = jnp.dot(probs[hp], v_aug, preferred_element_type=F32)
            o = o[:, :LANES] / o[:, LANES:]
            y_ref[0, qrow, cols] = jnp.where(low, o[:W], o[W:]).astype(BF16)
        return 0

    lax.fori_loop(0, NA_ROWS_PER_STEP, row_body, 0, unroll=2)


def _na(proj3, patterns, q_blk, k_blk, v_blk):
    B, S, _ = proj3.shape
    rows = S // GRID_W
    rb = NA_ROWS_PER_STEP
    n_dr = patterns.shape[1]
    return pl.pallas_call(
        _na_kernel,
        grid=(B, rows // rb),
        in_specs=[
            pl.BlockSpec((1, rb * GRID_W, NA_WIDTH), lambda b, r: (b, r, q_blk)),
            pl.BlockSpec((1, S, NA_WIDTH), lambda b, r: (b, 0, k_blk)),
            pl.BlockSpec((1, S, NA_WIDTH), lambda b, r: (b, 0, v_blk)),
            pl.BlockSpec(patterns.shape, lambda b, r: (0, 0, 0)),
        ],
        out_specs=pl.BlockSpec((1, rb * GRID_W, NA_WIDTH), lambda b, r: (b, r, 0)),
        out_shape=jax.ShapeDtypeStruct((B, S, NA_WIDTH), BF16),
        scratch_shapes=[pltpu.VMEM((n_dr, NA_HEADS // 2, 2 * GRID_W, LANES), F32)],
        compiler_params=_params(("arbitrary", "arbitrary"), VMEM_LIMIT),
        name="natten",
    )(proj3, proj3, proj3, patterns)


def _na_bias_patterns(rpb):
    c0 = NA_WIN_COLS - 1
    r = rpb.astype(F32)
    gap = jnp.zeros(r.shape[:1] + (r.shape[1] - 1, LANES // 2 - NA_WIN_COLS - c0), F32)
    return jnp.concatenate([r[:, :-1, c0:], gap, r[:, 1:, :], gap, r[:, :-1, :c0]], axis=-1)


def _memkv_kernel(m_ref, g_ref, w_ref, o_ref):
    hb = _rms(m_ref[...], g_ref[...]).astype(BF16)
    o_ref[...] = jnp.dot(hb, w_ref[...], preferred_element_type=F32).astype(BF16)


def _memkv(mem2, g, w, tm):
    T, D = mem2.shape
    N = w.shape[1]
    return pl.pallas_call(
        _memkv_kernel,
        grid=(T // tm,),
        in_specs=[
            pl.BlockSpec((tm, D), lambda i: (i, 0)),
            pl.BlockSpec((1, D), lambda i: (0, 0)),
            pl.BlockSpec((D, N), lambda i: (0, 0)),
        ],
        out_specs=pl.BlockSpec((tm, N), lambda i: (i, 0)),
        out_shape=jax.ShapeDtypeStruct((T, N), BF16),
        compiler_params=_params(("parallel",), VMEM_LIMIT),
        name="memkv",
    )(mem2, g, w)


def _xa_kernel(q_ref, kv_ref, y_ref):
    Dh = XA_HEAD_DIM
    for h in range(XA_HEADS):
        q = q_ref[0, :, h * Dh:(h + 1) * Dh]
        k = kv_ref[0, :, h * Dh:(h + 1) * Dh]
        v = kv_ref[0, :, XA_WIDTH + h * Dh:XA_WIDTH + (h + 1) * Dh]
        s = lax.dot_general(q, k, (((1,), (1,)), ((), ())), preferred_element_type=F32) * (Dh ** -0.5)
        p = jnp.exp(s - jnp.max(s, axis=-1, keepdims=True))
        o = jnp.dot(p.astype(BF16), v, preferred_element_type=F32)
        y_ref[0, :, h * Dh:(h + 1) * Dh] = (o / jnp.sum(p, axis=-1, keepdims=True)).astype(BF16)


def _xa(proj3, kv3, q_blk, tq):
    B, S, _ = proj3.shape
    M = kv3.shape[1]
    return pl.pallas_call(
        _xa_kernel,
        grid=(B, S // tq),
        in_specs=[
            pl.BlockSpec((1, tq, XA_WIDTH), lambda b, i: (b, i, q_blk)),
            pl.BlockSpec((1, M, 2 * XA_WIDTH), lambda b, i: (b, 0, 0)),
        ],
        out_specs=pl.BlockSpec((1, tq, XA_WIDTH), lambda b, i: (b, i, 0)),
        out_shape=jax.ShapeDtypeStruct((B, S, XA_WIDTH), BF16),
        compiler_params=_params(("parallel", "parallel"), VMEM_LIMIT),
        name="memxattn",
    )(proj3, kv3)


def _merge_kernel(x_ref, yml_ref, yna_ref, yxa_ref, p0_ref, p1_ref, p2_ref, bg_ref,
                  wml_ref, wna_ref, wxa_ref, wo_ref, o_ref):
    D = x_ref.shape[1]
    merged = None
    for n, (y_ref, p_ref, w_ref) in enumerate(
            ((yml_ref, p0_ref, wml_ref), (yna_ref, p1_ref, wna_ref), (yxa_ref, p2_ref, wxa_ref))):
        gate = jax.nn.sigmoid(p_ref[...].astype(F32) + bg_ref[:, n * D:(n + 1) * D])
        term = gate * jnp.dot(y_ref[...], w_ref[...], preferred_element_type=F32)
        merged = term if merged is None else merged + term
    o_ref[...] = x_ref[...] + jnp.dot(merged.astype(BF16), wo_ref[...], preferred_element_type=F32)


def _merge(x2, yml, yna, yxa, proj2, b_gate, wml, wna, wxa, wo, gate_blk, tm):
    T, D = x2.shape
    row = lambda w: pl.BlockSpec((tm, w), lambda i: (i, 0))
    full = lambda a: pl.BlockSpec(a.shape, lambda i: (0, 0))
    return pl.pallas_call(
        _merge_kernel,
        grid=(T // tm,),
        in_specs=[
            row(D), row(ML_WIDTH), row(NA_WIDTH), row(XA_WIDTH),
            pl.BlockSpec((tm, D), lambda i: (i, gate_blk)),
            pl.BlockSpec((tm, D), lambda i: (i, gate_blk + 1)),
            pl.BlockSpec((tm, D), lambda i: (i, gate_blk + 2)),
            full(b_gate), full(wml), full(wna), full(wxa), full(wo),
        ],
        out_specs=row(D),
        out_shape=jax.ShapeDtypeStruct((T, D), F32),
        compiler_params=_params(("parallel",), VMEM_LIMIT),
        name="merge",
    )(x2, yml, yna, yxa, proj2, proj2, proj2, b_gate, wml, wna, wxa, wo)


def _gelu_tanh(x):
    return 0.5 * x * (1.0 + jnp.tanh(0.7978845608028654 * (x + 0.044715 * (x * x * x))))


def _ffn_kernel(x_ref, g_ref, wa_ref, wu_ref, wc_ref, bc_ref, wd_ref, gf_ref, o_ref, h_scr):
    j = pl.program_id(1)
    S = x_ref.shape[1]

    @pl.when(j == 0)
    def _():
        x = x_ref[0]
        h_scr[...] = _rms(x, g_ref[...]).astype(BF16)
        o_ref[0] = x

    h = h_scr[...]
    a = jnp.dot(h, wa_ref[...], preferred_element_type=F32)
    u = jnp.dot(h, wu_ref[...], preferred_element_type=F32)
    row = lax.broadcasted_iota(jnp.int32, a.shape, 0)
    a_prev = jnp.where(row == 0, 0.0, pltpu.roll(a, 1, 0))
    a_next = jnp.where(row == S - 1, 0.0, pltpu.roll(a, S - 1, 0))
    w = wc_ref[...]
    conv = a_prev * w[0:1] + a * w[1:2] + a_next * w[2:3] + bc_ref[...]
    act = (_gelu_tanh(conv) * u).astype(BF16)
    o_ref[0] += jnp.dot(act, wd_ref[...], preferred_element_type=F32)

    @pl.when(j == pl.num_programs(1) - 1)
    def _():
        o_ref[0] = _rms(o_ref[0], gf_ref[...])


def _ffn(x3, g, w_up, w_conv, b_conv, w_down, g_final, tf):
    B, S, D = x3.shape
    FF = w_down.shape[0]
    nf = FF // tf
    return pl.pallas_call(
        _ffn_kernel,
        grid=(B, nf),
        in_specs=[
            pl.BlockSpec((1, S, D), lambda b, j: (b, 0, 0)),
            pl.BlockSpec((1, D), lambda b, j: (0, 0)),
            pl.BlockSpec((D, tf), lambda b, j: (0, j)),
            pl.BlockSpec((D, tf), lambda b, j: (0, nf + j)),
            pl.BlockSpec((3, tf), lambda b, j: (0, j)),
            pl.BlockSpec((1, tf), lambda b, j: (0, j)),
            pl.BlockSpec((tf, D), lambda b, j: (j, 0)),
            pl.BlockSpec((1, D), lambda b, j: (0, 0)),
        ],
        out_specs=pl.BlockSpec((1, S, D), lambda b, j: (b, 0, 0)),
        out_shape=jax.ShapeDtypeStruct((B, S, D), F32),
        scratch_shapes=[pltpu.VMEM((S, D), BF16)],
        compiler_params=_params(("parallel", "arbitrary"), VMEM_LIMIT),
        name="ffn",
    )(x3, g, w_up, w_up, w_conv, b_conv, w_down, g_final)


def kernel(x, mem, mix_norm_g, w_in, b_ml_igate, b_ml_fgate, w_ml_conv, b_ml_conv, ml_norm_g, na_rpb, mem_norm_g, w_mem_kv, b_merge_gate, w_br_ml, w_br_na, w_br_xa, w_out, ffn_norm_g, w_ffn_up, w_ffn_conv, b_ffn_conv, w_ffn_down, final_norm_g):
    B, S, D = x.shape
    H = ML_HEADS
    M = mem.shape[1]
    T = B * S
    assert w_in.shape[0] == 1, "single-layer block: the FFN kernel also applies the final norm"
    assert S % ML_CHUNK == 0 and S % (GRID_W * NA_ROWS_PER_STEP) == 0
    l = 0
    n_gate = 4 * H
    g0 = 4 * ML_WIDTH
    row2 = lambda v: v.reshape(1, -1).astype(F32)

    w_main = _wprep(w_in[l], g0, n_gate, tr=128)
    w_gate = jnp.transpose(w_in[l][:, g0:g0 + n_gate].reshape(D, 4, H), (2, 1, 0)).reshape(n_gate, D)
    w_gate = w_gate.astype(BF16)

    proj2, grow = _inproj(x.reshape(T, D), row2(mix_norm_g[l]), w_main, w_gate, tm=1024, tn=3584)
    proj3 = proj2.reshape(B, S, -1)

    bias = jnp.stack([b_ml_igate[l][0], b_ml_fgate[l][0], b_ml_igate[l][1], b_ml_fgate[l][1]],
                     axis=-1).astype(F32)
    y_ml = _mlstm(proj3, grow.reshape(H, 4, T // ML_CHUNK, ML_CHUNK), bias[:, :, None],
                  w_ml_conv[l].astype(F32), row2(b_ml_conv[l]), row2(ml_norm_g[l]))

    nb = (4 * ML_WIDTH) // NA_WIDTH
    y_na = _na(proj3, _na_bias_patterns(na_rpb[l]), nb, nb + 1, nb + 2)

    kv = _memkv(mem.reshape(B * M, D), row2(mem_norm_g[l]), w_mem_kv[l].astype(BF16), tm=512)
    xb = (4 * ML_WIDTH + 3 * NA_WIDTH) // XA_WIDTH
    y_xa = _xa(proj3, kv.reshape(B, M, -1), xb, tq=1024)

    gate_blk = (4 * ML_WIDTH + 3 * NA_WIDTH + XA_WIDTH) // D
    x1 = _merge(x.reshape(T, D), y_ml.reshape(T, -1), y_na.reshape(T, -1), y_xa.reshape(T, -1),
                proj2, row2(b_merge_gate[l]), w_br_ml[l].astype(BF16), w_br_na[l].astype(BF16),
                w_br_xa[l].astype(BF16), w_out[l].astype(BF16), gate_blk, tm=512)

    return _ffn(x1.reshape(B, S, D), row2(ffn_norm_g[l]), w_ffn_up[l].astype(BF16),
                w_ffn_conv[l].astype(F32), row2(b_ffn_conv[l]), w_ffn_down[l].astype(BF16),
                row2(final_norm_g), tf=256)
```

```python
import functools

import jax
import jax.numpy as jnp
from jax import lax
from jax.experimental import pallas as pl
from jax.experimental.pallas import tpu as pltpu

F32 = jnp.float32
BF16 = jnp.bfloat16
EPS = 1e-6
NEG_INIT = -1e30

GRID_W = 64
ML_HEADS = 4
ML_HEAD_DIM = 128
ML_WIDTH = ML_HEADS * ML_HEAD_DIM
NA_HEADS = 8
NA_HEAD_DIM = 64
NA_WIDTH = NA_HEADS * NA_HEAD_DIM
NA_WIN_ROWS = 8
NA_WIN_COLS = 16
XA_HEADS = 4
XA_HEAD_DIM = 128
XA_WIDTH = XA_HEADS * XA_HEAD_DIM

LANES = 128
BF16_ROWS = 16
ML_CHUNK = 256
NA_ROWS_PER_STEP = 8
VMEM_LIMIT = 60 * 1024 * 1024


def _params(dims, vmem=None):
    return pltpu.CompilerParams(dimension_semantics=dims, vmem_limit_bytes=vmem)


def _rms(x, g):
    return x * lax.rsqrt(jnp.mean(x * x, axis=-1, keepdims=True) + EPS) * g


def _wprep_kernel(w_ref, o_ref, og_ref, *, g0, n_gate):
    o_ref[:, :g0] = w_ref[:, :g0].astype(BF16)
    o_ref[:, g0:] = w_ref[:, g0 + n_gate:].astype(BF16)
    og_ref[...] = w_ref[:, g0:g0 + n_gate]


def _wprep(w, g0, n_gate, tr):
    D, N = w.shape
    return pl.pallas_call(
        functools.partial(_wprep_kernel, g0=g0, n_gate=n_gate),
        grid=(D // tr,),
        in_specs=[pl.BlockSpec((tr, N), lambda i: (i, 0))],
        out_specs=[pl.BlockSpec((tr, N - n_gate), lambda i: (i, 0)),
                   pl.BlockSpec((tr, n_gate), lambda i: (i, 0))],
        out_shape=[jax.ShapeDtypeStruct((D, N - n_gate), BF16),
                   jax.ShapeDtypeStruct((D, n_gate), F32)],
        compiler_params=_params(("parallel",), VMEM_LIMIT),
        name="wprep",
    )(w)


def _inproj_kernel(x_ref, g_ref, w_ref, wg_ref, o_ref, gates_ref, h_scr):
    @pl.when(pl.program_id(1) == 0)
    def _():
        hb = _rms(x_ref[...], g_ref[...]).astype(BF16)
        h_scr[...] = hb
        gt = lax.dot_general(wg_ref[...], hb, (((1,), (1,)), ((), ())), preferred_element_type=F32)
        for h in range(ML_HEADS):
            gates_ref[h] = gt[4 * h:4 * h + 4, :]

    o_ref[...] = jnp.dot(h_scr[...], w_ref[...], preferred_element_type=F32).astype(BF16)


def _inproj(x2, g, w_main, w_gate, tm, tn):
    T, D = x2.shape
    N = w_main.shape[1]
    return pl.pallas_call(
        _inproj_kernel,
        grid=(T // tm, N // tn),
        in_specs=[
            pl.BlockSpec((tm, D), lambda i, j: (i, 0)),
            pl.BlockSpec((1, D), lambda i, j: (0, 0)),
            pl.BlockSpec((D, tn), lambda i, j: (0, j)),
            pl.BlockSpec(w_gate.shape, lambda i, j: (0, 0)),
        ],
        out_specs=[
            pl.BlockSpec((tm, tn), lambda i, j: (i, j)),
            pl.BlockSpec((ML_HEADS, 4, tm), lambda i, j: (0, 0, i)),
        ],
        out_shape=[
            jax.ShapeDtypeStruct((T, N), BF16),
            jax.ShapeDtypeStruct((ML_HEADS, 4, T), F32),
        ],
        scratch_shapes=[pltpu.VMEM((tm, D), BF16)],
        compiler_params=_params(("parallel", "arbitrary"), VMEM_LIMIT),
        name="inproj",
    )(x2, g, w_main, w_gate)


def _log_sigmoid(x):
    return jnp.minimum(x, 0.0) - jnp.log(1.0 + jnp.exp(-jnp.abs(x)))


def _chunk_scan(x, pos, L, reverse, op, identity):
    n = x.shape[1]
    d = 1
    while d < L:
        if reverse:
            x = op(x, jnp.where(pos < L - d, pltpu.roll(x, n - d, 1), identity))
        else:
            x = op(x, jnp.where(pos >= d, pltpu.roll(x, d, 1), identity))
        d *= 2
    return x


def _mlstm_kernel(q_ref, k_ref, v_ref, o_ref, gr_ref, br_ref,
                  wq_ref, wk_ref, bq_ref, bk_ref, ng_ref, y_ref,
                  qt_scr, k_scr, vat_scr, bb_scr, hf_scr, hb_scr):
    S = q_ref.shape[1]
    L = ML_CHUNK
    NC = S // L
    Dh = ML_HEAD_DIM

    pos = lax.broadcasted_iota(jnp.int32, (NC, L), 1)
    gates = []
    for d, rev in ((0, False), (1, True)):
        i_pre = gr_ref[0, 2 * d] + br_ref[0, 2 * d:2 * d + 1, :]
        lf = _log_sigmoid(gr_ref[0, 2 * d + 1] + br_ref[0, 2 * d + 1:2 * d + 2, :])
        a = _chunk_scan(lf, pos, L, rev, jnp.add, 0.0)
        b = i_pre - a
        gates.append((b, a, _chunk_scan(b, pos, L, rev, jnp.maximum, -jnp.inf), lf))

    def conv_silu(x_ref, w_ref, b_ref, c):
        x = x_ref[0, c * L:(c + 1) * L, :].astype(F32)
        zero_row = jnp.zeros((1, Dh), F32)
        prev_row = x_ref[0, c * L - BF16_ROWS:c * L, :].astype(F32)[BF16_ROWS - 1:, :] if c > 0 else zero_row
        next_row = x_ref[0, (c + 1) * L:(c + 1) * L + BF16_ROWS, :].astype(F32)[0:1, :] if c < NC - 1 else zero_row
        row = lax.broadcasted_iota(jnp.int32, x.shape, 0)
        x_prev = jnp.where(row == 0, prev_row, pltpu.roll(x, 1, 0))
        x_next = jnp.where(row == L - 1, next_row, pltpu.roll(x, L - 1, 0))
        w = w_ref[...]
        y = x_prev * w[0:1] + x * w[1:2] + x_next * w[2:3] + b_ref[...]
        return y * jax.nn.sigmoid(y)

    vat_scr[Dh:, :] = jnp.ones((Dh, S), BF16)

    def prep(c):
        cs = slice(c * L, (c + 1) * L)
        qt_scr[:, cs] = (conv_silu(q_ref, wq_ref, bq_ref, c) * (Dh ** -0.5)).astype(BF16).T
        k_scr[cs, :] = conv_silu(k_ref, wk_ref, bk_ref, c).astype(BF16)
        vat_scr[:Dh, cs] = v_ref[0, cs, :].T
        for d in range(2):
            bb_scr[d, cs, :] = jnp.broadcast_to(gates[d][0][c:c + 1, :], (LANES, L)).T

    s_idx = lax.broadcasted_iota(jnp.int32, (L, L), 0)
    j_idx = lax.broadcasted_iota(jnp.int32, (L, L), 1)

    def chunk(c, m, st_t, h_scr, fwd):
        cs = slice(c * L, (c + 1) * L)
        d = 0 if fwd else 1
        b_row, a_row, bmax_row, lf_row = (t[c:c + 1, :] for t in gates[d])
        mask_t = (s_idx <= j_idx) if fwd else (s_idx >= j_idx)
        g = jnp.sum(lf_row, axis=1, keepdims=True)
        w_end = g + b_row
        m_new = jnp.maximum(g + m, jnp.max(w_end, axis=1, keepdims=True))
        decay = jnp.exp(g + m - m_new)
        wt = jnp.exp(w_end - m_new)
        c_row = jnp.maximum(m, bmax_row)
        inter_w = jnp.exp(m - c_row)
        floor = jnp.exp(-(a_row + c_row))
        k = k_scr[cs, :]
        q_t = qt_scr[:, cs]
        va_t = vat_scr[:, cs]
        qk_t = jnp.dot(k, q_t, preferred_element_type=F32)
        b_col = bb_scr[d, cs, :]
        expo = jnp.concatenate([b_col] * (L // LANES), axis=1) - c_row
        p_t = jnp.exp(jnp.where(mask_t, expo, -jnp.inf)) * qk_t
        q_in = (q_t.astype(F32) * inter_w).astype(BF16)
        nd_t = (jnp.dot(va_t, p_t.astype(BF16), preferred_element_type=F32)
                + jnp.dot(st_t.astype(BF16), q_in, preferred_element_type=F32))
        h_scr[:, cs] = nd_t[:Dh] / jnp.maximum(jnp.abs(nd_t[Dh:]), floor)
        vw = jnp.concatenate([(va_t[:Dh].astype(F32) * wt).astype(BF16),
                              jnp.broadcast_to(wt, (Dh, L)).astype(BF16)], axis=0)
        return m_new, decay * st_t + jnp.dot(vw, k, preferred_element_type=F32)

    def finish(c):
        cs = slice(c * L, (c + 1) * L)
        h_t = hf_scr[:, cs] + hb_scr[:, cs]
        h_t = h_t * lax.rsqrt(jnp.mean(h_t * h_t, axis=0, keepdims=True) + EPS)
        y = h_t.T * ng_ref[...] * jax.nn.sigmoid(o_ref[0, cs, :].astype(F32))
        y_ref[0, cs, :] = y.astype(BF16)

    m_f = m_b = jnp.full((1, 1), NEG_INIT, F32)
    st_f = st_b = jnp.zeros((2 * Dh, Dh), F32)
    prep(0)
    prep(NC - 1)
    for c in range(NC):
        if c + 1 < NC - 1 - (c + 1):
            prep(c + 1)
            prep(NC - 2 - c)
        elif c + 1 == NC - 1 - (c + 1):
            prep(c + 1)
        m_f, st_f = chunk(c, m_f, st_f, hf_scr, True)
        m_b, st_b = chunk(NC - 1 - c, m_b, st_b, hb_scr, False)
        if c >= NC - 1 - c:
            finish(c)
            if c != NC - 1 - c:
                finish(NC - 1 - c)


def _mlstm(proj3, grow, brow, w_conv, b_conv, norm_g):
    B, S, _ = proj3.shape
    H, Dh = ML_HEADS, ML_HEAD_DIM
    blk = lambda off: pl.BlockSpec((1, S, Dh), lambda b, h: (b, 0, off + h))
    return pl.pallas_call(
        _mlstm_kernel,
        grid=(B, H),
        in_specs=[
            blk(0), blk(H), blk(2 * H), blk(3 * H),
            pl.BlockSpec((1, 4, S // ML_CHUNK, ML_CHUNK), lambda b, h: (h, 0, b, 0)),
            pl.BlockSpec((1, 4, 1), lambda b, h: (h, 0, 0)),
            pl.BlockSpec((3, Dh), lambda b, h: (0, h)),
            pl.BlockSpec((3, Dh), lambda b, h: (0, H + h)),
            pl.BlockSpec((1, Dh), lambda b, h: (0, h)),
            pl.BlockSpec((1, Dh), lambda b, h: (0, H + h)),
            pl.BlockSpec((1, Dh), lambda b, h: (0, h)),
        ],
        out_specs=pl.BlockSpec((1, S, Dh), lambda b, h: (b, 0, h)),
        out_shape=jax.ShapeDtypeStruct((B, S, ML_WIDTH), BF16),
        scratch_shapes=[
            pltpu.VMEM((Dh, S), BF16), pltpu.VMEM((S, Dh), BF16), pltpu.VMEM((2 * Dh, S), BF16),
            pltpu.VMEM((2, S, LANES), F32),
            pltpu.VMEM((Dh, S), F32), pltpu.VMEM((Dh, S), F32),
        ],
        compiler_params=_params(("parallel", "parallel"), VMEM_LIMIT),
        name="mlstm",
    )(proj3, proj3, proj3, proj3, grow, brow, w_conv, w_conv, b_conv, b_conv, norm_g)


def _na_kernel(q_ref, k_ref, v_ref, pat_ref, y_ref, t_scr):
    S = k_ref.shape[1]
    rows = S // GRID_W
    W = GRID_W
    n_pairs = NA_HEADS // 2
    n_dr = t_scr.shape[0]
    r0 = pl.program_id(1) * NA_ROWS_PER_STEP
    low = lax.broadcasted_iota(jnp.int32, (W, LANES), 1) < NA_HEAD_DIM

    @pl.when((pl.program_id(0) == 0) & (pl.program_id(1) == 0))
    def _():
        cq = lax.broadcasted_iota(jnp.int32, (W, LANES), 0)
        ck = lax.broadcasted_iota(jnp.int32, (W, LANES), 1) % W
        col_start = jnp.clip(cq - NA_WIN_COLS // 2, 0, W - NA_WIN_COLS)
        col_ok = (ck >= col_start) & (ck < col_start + NA_WIN_COLS)
        for d in range(n_dr):
            for h in range(NA_HEADS):
                pattern = jnp.broadcast_to(pat_ref[h, d:d + 1, :], (W, LANES))
                skewed = pltpu.roll(pattern, 0, 1, stride=1, stride_axis=0)
                t_scr[d, h // 2, (h % 2) * W:(h % 2 + 1) * W, :] = jnp.where(col_ok, skewed, -jnp.inf)

    ones = jnp.ones((NA_WIN_ROWS * W, LANES), BF16)

    def row_body(i, _):
        r = r0 + i
        row_start = jnp.clip(r - NA_WIN_ROWS // 2, 0, rows - NA_WIN_ROWS)
        dr0 = row_start - r + (NA_WIN_ROWS - 1)
        band = pl.ds(pl.multiple_of(row_start * W, W), NA_WIN_ROWS * W)
        qrow = pl.ds(pl.multiple_of(i * W, W), W)
        scores = []
        for hp in range(n_pairs):
            cols = slice(hp * LANES, (hp + 1) * LANES)
            q2 = q_ref[0, qrow, cols] * (NA_HEAD_DIM ** -0.5)
            zero = jnp.zeros_like(q2)
            qs = jnp.concatenate([jnp.where(low, q2, zero), jnp.where(low, zero, q2)], axis=0)
            s = lax.dot_general(qs, k_ref[0, band, cols], (((1,), (1,)), ((), ())),
                                preferred_element_type=F32)
            bias = jnp.concatenate([t_scr[dr0 + kk, hp] for kk in range(0, NA_WIN_ROWS, 2)], axis=-1)
            scores.append(s + bias)
        probs = [jnp.exp(s - jnp.max(s, axis=-1, keepdims=True)).astype(BF16) for s in scores]
        for hp in range(n_pairs):
            cols = slice(hp * LANES, (hp + 1) * LANES)
            v_aug = jnp.concatenate([v_ref[0, band, cols], ones], axis=1)
            o = jnp.dot(probs[hp], v_aug, preferred_element_type=F32)
            o = o[:, :LANES] / o[:, LANES:]
            y_ref[0, qrow, cols] = jnp.where(low, o[:W], o[W:]).astype(BF16)
        return 0

    lax.fori_loop(0, NA_ROWS_PER_STEP, row_body, 0, unroll=2)


def _na(proj3, patterns, q_blk, k_blk, v_blk):
    B, S, _ = proj3.shape
    rows = S // GRID_W
    rb = NA_ROWS_PER_STEP
    n_dr = patterns.shape[1]
    return pl.pallas_call(
        _na_kernel,
        grid=(B, rows // rb),
        in_specs=[
            pl.BlockSpec((1, rb * GRID_W, NA_WIDTH), lambda b, r: (b, r, q_blk)),
            pl.BlockSpec((1, S, NA_WIDTH), lambda b, r: (b, 0, k_blk)),
            pl.BlockSpec((1, S, NA_WIDTH), lambda b, r: (b, 0, v_blk)),
            pl.BlockSpec(patterns.shape, lambda b, r: (0, 0, 0)),
        ],
        out_specs=pl.BlockSpec((1, rb * GRID_W, NA_WIDTH), lambda b, r: (b, r, 0)),
        out_shape=jax.ShapeDtypeStruct((B, S, NA_WIDTH), BF16),
        scratch_shapes=[pltpu.VMEM((n_dr, NA_HEADS // 2, 2 * GRID_W, LANES), F32)],
        compiler_params=_params(("arbitrary", "arbitrary"), VMEM_LIMIT),
        name="natten",
    )(proj3, proj3, proj3, patterns)


def _na_bias_patterns(rpb):
    c0 = NA_WIN_COLS - 1
    r = rpb.astype(F32)
    gap = jnp.zeros(r.shape[:1] + (r.shape[1] - 1, LANES // 2 - NA_WIN_COLS - c0), F32)
    return jnp.concatenate([r[:, :-1, c0:], gap, r[:, 1:, :], gap, r[:, :-1, :c0]], axis=-1)


def _memkv_kernel(m_ref, g_ref, w_ref, o_ref):
    hb = _rms(m_ref[...], g_ref[...]).astype(BF16)
    o_ref[...] = jnp.dot(hb, w_ref[...], preferred_element_type=F32).astype(BF16)


def _memkv(mem2, g, w, tm):
    T, D = mem2.shape
    N = w.shape[1]
    return pl.pallas_call(
        _memkv_kernel,
        grid=(T // tm,),
        in_specs=[
            pl.BlockSpec((tm, D), lambda i: (i, 0)),
            pl.BlockSpec((1, D), lambda i: (0, 0)),
            pl.BlockSpec((D, N), lambda i: (0, 0)),
        ],
        out_specs=pl.BlockSpec((tm, N), lambda i: (i, 0)),
        out_shape=jax.ShapeDtypeStruct((T, N), BF16),
        compiler_params=_params(("parallel",), VMEM_LIMIT),
        name="memkv",
    )(mem2, g, w)


def _xa_kernel(q_ref, kv_ref, y_ref):
    Dh = XA_HEAD_DIM
    for h in range(XA_HEADS):
        q = q_ref[0, :, h * Dh:(h + 1) * Dh]
        k = kv_ref[0, :, h * Dh:(h + 1) * Dh]
        v = kv_ref[0, :, XA_WIDTH + h * Dh:XA_WIDTH + (h + 1) * Dh]
        s = lax.dot_general(q, k, (((1,), (1,)), ((), ())), preferred_element_type=F32) * (Dh ** -0.5)
        p = jnp.exp(s - jnp.max(s, axis=-1, keepdims=True))
        o = jnp.dot(p.astype(BF16), v, preferred_element_type=F32)
        y_ref[0, :, h * Dh:(h + 1) * Dh] = (o / jnp.sum(p, axis=-1, keepdims=True)).astype(BF16)


def _xa(proj3, kv3, q_blk, tq):
    B, S, _ = proj3.shape
    M = kv3.shape[1]
    return pl.pallas_call(
        _xa_kernel,
        grid=(B, S // tq),
        in_specs=[
            pl.BlockSpec((1, tq, XA_WIDTH), lambda b, i: (b, i, q_blk)),
            pl.BlockSpec((1, M, 2 * XA_WIDTH), lambda b, i: (b, 0, 0)),
        ],
        out_specs=pl.BlockSpec((1, tq, XA_WIDTH), lambda b, i: (b, i, 0)),
        out_shape=jax.ShapeDtypeStruct((B, S, XA_WIDTH), BF16),
        compiler_params=_params(("parallel", "parallel"), VMEM_LIMIT),
        name="memxattn",
    )(proj3, kv3)


def _merge_kernel(x_ref, yml_ref, yna_ref, yxa_ref, p0_ref, p1_ref, p2_ref, bg_ref,
                  wml_ref, wna_ref, wxa_ref, wo_ref, o_ref):
    D = x_ref.shape[1]
    merged = None
    for n, (y_ref, p_ref, w_ref) in enumerate(
            ((yml_ref, p0_ref, wml_ref), (yna_ref, p1_ref, wna_ref), (yxa_ref, p2_ref, wxa_ref))):
        gate = jax.nn.sigmoid(p_ref[...].astype(F32) + bg_ref[:, n * D:(n + 1) * D])
        term = gate * jnp.dot(y_ref[...], w_ref[...], preferred_element_type=F32)
        merged = term if merged is None else merged + term
    o_ref[...] = x_ref[...] + jnp.dot(merged.astype(BF16), wo_ref[...], preferred_element_type=F32)


def _merge(x2, yml, yna, yxa, proj2, b_gate, wml, wna, wxa, wo, gate_blk, tm):
    T, D = x2.shape
    row = lambda w: pl.BlockSpec((tm, w), lambda i: (i, 0))
    full = lambda a: pl.BlockSpec(a.shape, lambda i: (0, 0))
    return pl.pallas_call(
        _merge_kernel,
        grid=(T // tm,),
        in_specs=[
            row(D), row(ML_WIDTH), row(NA_WIDTH), row(XA_WIDTH),
            pl.BlockSpec((tm, D), lambda i: (i, gate_blk)),
            pl.BlockSpec((tm, D), lambda i: (i, gate_blk + 1)),
            pl.BlockSpec((tm, D), lambda i: (i, gate_blk + 2)),
            full(b_gate), full(wml), full(wna), full(wxa), full(wo),
        ],
        out_specs=row(D),
        out_shape=jax.ShapeDtypeStruct((T, D), F32),
        compiler_params=_params(("parallel",), VMEM_LIMIT),
        name="merge",
    )(x2, yml, yna, yxa, proj2, proj2, proj2, b_gate, wml, wna, wxa, wo)


def _gelu_tanh(x):
    return 0.5 * x * (1.0 + jnp.tanh(0.7978845608028654 * (x + 0.044715 * (x * x * x))))


def _ffn_kernel(x_ref, g_ref, wa0_ref, wa1_ref, wu0_ref, wu1_ref, wc0_ref, wc1_ref, bc0_ref, bc1_ref,
                wd0_ref, wd1_ref, gf_ref, o_ref, h_scr, a0_scr, u0_scr, a1_scr, u1_scr):
    t = pl.program_id(1)
    last = pl.num_programs(1) - 1
    S = x_ref.shape[1]

    def produce(wa_ref, wu_ref, a_scr, u_scr):
        h = h_scr[...]
        a_scr[...] = jnp.dot(h, wa_ref[...], preferred_element_type=F32)
        u_scr[...] = jnp.dot(h, wu_ref[...], preferred_element_type=F32)

    def consume(a_scr, u_scr, wc_ref, bc_ref, wd_ref):
        a = a_scr[...]
        row = lax.broadcasted_iota(jnp.int32, a.shape, 0)
        a_prev = jnp.where(row == 0, 0.0, pltpu.roll(a, 1, 0))
        a_next = jnp.where(row == S - 1, 0.0, pltpu.roll(a, S - 1, 0))
        w = wc_ref[...]
        conv = a_prev * w[0:1] + a * w[1:2] + a_next * w[2:3] + bc_ref[...]
        act = (_gelu_tanh(conv) * u_scr[...]).astype(BF16)
        o_ref[0] += jnp.dot(act, wd_ref[...], preferred_element_type=F32)

    @pl.when(t == 0)
    def _():
        x = x_ref[0]
        h_scr[...] = _rms(x, g_ref[...]).astype(BF16)
        o_ref[0] = x
        produce(wa0_ref, wu0_ref, a0_scr, u0_scr)
        produce(wa1_ref, wu1_ref, a1_scr, u1_scr)
        consume(a0_scr, u0_scr, wc1_ref, bc1_ref, wd1_ref)

    @pl.when((t > 0) & (t < last))
    def _():
        produce(wa0_ref, wu0_ref, a0_scr, u0_scr)
        consume(a1_scr, u1_scr, wc0_ref, bc0_ref, wd0_ref)
        produce(wa1_ref, wu1_ref, a1_scr, u1_scr)
        consume(a0_scr, u0_scr, wc1_ref, bc1_ref, wd1_ref)

    @pl.when(t == last)
    def _():
        produce(wa0_ref, wu0_ref, a0_scr, u0_scr)
        consume(a1_scr, u1_scr, wc0_ref, bc0_ref, wd0_ref)
        consume(a0_scr, u0_scr, wc1_ref, bc1_ref, wd1_ref)
        o_ref[0] = _rms(o_ref[0], gf_ref[...])


def _ffn(x3, g, w_up, w_conv, b_conv, w_down, g_final, tf):
    B, S, D = x3.shape
    FF = w_down.shape[0]
    nf = FF // tf
    assert nf % 2 == 1, "the two-chunk pipeline ends on a produce-one / consume-two step"
    steps = (nf + 1) // 2
    prod0 = lambda t: 2 * t
    prod1 = lambda t: jnp.minimum(2 * t + 1, nf - 1)
    cons0 = lambda t: jnp.maximum(2 * t - 1, 0)
    cons1 = lambda t: 2 * t
    col = lambda f, off=0: pl.BlockSpec((D, tf), lambda b, t: (0, off + f(t)))
    vec = lambda rows, f: pl.BlockSpec((rows, tf), lambda b, t: (0, f(t)))
    dn = lambda f: pl.BlockSpec((tf, D), lambda b, t: (f(t), 0))
    buf = pltpu.VMEM((S, tf), F32)
    return pl.pallas_call(
        _ffn_kernel,
        grid=(B, steps),
        in_specs=[
            pl.BlockSpec((1, S, D), lambda b, t: (b, 0, 0)),
            pl.BlockSpec((1, D), lambda b, t: (0, 0)),
            col(prod0), col(prod1), col(prod0, nf), col(prod1, nf),
            vec(3, cons0), vec(3, cons1), vec(1, cons0), vec(1, cons1),
            dn(cons0), dn(cons1),
            pl.BlockSpec((1, D), lambda b, t: (0, 0)),
        ],
        out_specs=pl.BlockSpec((1, S, D), lambda b, t: (b, 0, 0)),
        out_shape=jax.ShapeDtypeStruct((B, S, D), F32),
        scratch_shapes=[pltpu.VMEM((S, D), BF16), buf, buf, buf, buf],
        compiler_params=_params(("parallel", "arbitrary"), VMEM_LIMIT),
        name="ffn",
    )(x3, g, w_up, w_up, w_up, w_up, w_conv, w_conv, b_conv, b_conv, w_down, w_down, g_final)


def kernel(x, mem, mix_norm_g, w_in, b_ml_igate, b_ml_fgate, w_ml_conv, b_ml_conv, ml_norm_g, na_rpb, mem_norm_g, w_mem_kv, b_merge_gate, w_br_ml, w_br_na, w_br_xa, w_out, ffn_norm_g, w_ffn_up, w_ffn_conv, b_ffn_conv, w_ffn_down, final_norm_g):
    B, S, D = x.shape
    H = ML_HEADS
    M = mem.shape[1]
    T = B * S
    assert w_in.shape[0] == 1, "single-layer block: the FFN kernel also applies the final norm"
    assert S % ML_CHUNK == 0 and S % (GRID_W * NA_ROWS_PER_STEP) == 0
    l = 0
    n_gate = 4 * H
    g0 = 4 * ML_WIDTH
    row2 = lambda v: v.reshape(1, -1).astype(F32)

    w_main, w_gate = _wprep(w_in[l], g0, n_gate, tr=128)
    w_gate = jnp.transpose(w_gate.reshape(D, 4, H), (2, 1, 0)).reshape(n_gate, D).astype(BF16)

    proj2, grow = _inproj(x.reshape(T, D), row2(mix_norm_g[l]), w_main, w_gate, tm=1024, tn=3584)
    proj3 = proj2.reshape(B, S, -1)

    bias = jnp.stack([b_ml_igate[l][0], b_ml_fgate[l][0], b_ml_igate[l][1], b_ml_fgate[l][1]],
                     axis=-1).astype(F32)
    y_ml = _mlstm(proj3, grow.reshape(H, 4, T // ML_CHUNK, ML_CHUNK), bias[:, :, None],
                  w_ml_conv[l].astype(F32), row2(b_ml_conv[l]), row2(ml_norm_g[l]))

    nb = (4 * ML_WIDTH) // NA_WIDTH
    y_na = _na(proj3, _na_bias_patterns(na_rpb[l]), nb, nb + 1, nb + 2)

    kv = _memkv(mem.reshape(B * M, D), row2(mem_norm_g[l]), w_mem_kv[l].astype(BF16), tm=512)
    xb = (4 * ML_WIDTH + 3 * NA_WIDTH) // XA_WIDTH
    y_xa = _xa(proj3, kv.reshape(B, M, -1), xb, tq=1024)

    gate_blk = (4 * ML_WIDTH + 3 * NA_WIDTH + XA_WIDTH) // D
    x1 = _merge(x.reshape(T, D), y_ml.reshape(T, -1), y_na.reshape(T, -1), y_xa.reshape(T, -1),
                proj2, row2(b_merge_gate[l]), w_br_ml[l].astype(BF16), w_br_na[l].astype(BF16),
                w_br_xa[l].astype(BF16), w_out[l].astype(BF16), gate_blk, tm=512)

    return _ffn(x1.reshape(B, S, D), row2(ffn_norm_g[l]), w_ffn_up[l].astype(BF16),
                w_ffn_conv[l].astype(F32), row2(b_ffn_conv[l]), w_ffn_down[l].astype(BF16),
                row2(final_norm_g), tf=256)
```

```python
import functools

import jax
import jax.numpy as jnp
from jax import lax
from jax.experimental import pallas as pl
from jax.experimental.pallas import tpu as pltpu

F32 = jnp.float32
BF16 = jnp.bfloat16
EPS = 1e-6
NEG_INIT = -1e30

GRID_W = 64
ML_HEADS = 4
ML_HEAD_DIM = 128
ML_WIDTH = ML_HEADS * ML_HEAD_DIM
NA_HEADS = 8
NA_HEAD_DIM = 64
NA_WIDTH = NA_HEADS * NA_HEAD_DIM
NA_WIN_ROWS = 8
NA_WIN_COLS = 16
XA_HEADS = 4
XA_HEAD_DIM = 128
XA_WIDTH = XA_HEADS * XA_HEAD_DIM

LANES = 128
BF16_ROWS = 16
ML_CHUNK = 256
NA_ROWS_PER_STEP = 8
VMEM_LIMIT = 60 * 1024 * 1024


def _params(dims, vmem=None):
    return pltpu.CompilerParams(dimension_semantics=dims, vmem_limit_bytes=vmem)


def _rms(x, g):
    return x * lax.rsqrt(jnp.mean(x * x, axis=-1, keepdims=True) + EPS) * g


def _wprep_kernel(w_ref, wg_ref, o_ref, og_ref):
    o_ref[...] = w_ref[...].astype(BF16)
    og_ref[...] = wg_ref[...]


def _wprep(w_t, g0, n_gate, tr):
    N, D = w_t.shape
    assert g0 % tr == 0 and g0 % n_gate == 0 and (N - n_gate) % tr == 0
    return pl.pallas_call(
        _wprep_kernel,
        grid=((N - n_gate) // tr,),
        in_specs=[
            pl.BlockSpec((pl.Element(tr), pl.Element(D)),
                         lambda i: (pl.multiple_of(i * tr + jnp.where(i * tr >= g0, n_gate, 0), n_gate), 0)),
            pl.BlockSpec((n_gate, D), lambda i: (g0 // n_gate, 0)),
        ],
        out_specs=[pl.BlockSpec((tr, D), lambda i: (i, 0)),
                   pl.BlockSpec((n_gate, D), lambda i: (0, 0))],
        out_shape=[jax.ShapeDtypeStruct((N - n_gate, D), BF16),
                   jax.ShapeDtypeStruct((n_gate, D), F32)],
        compiler_params=_params(("arbitrary",), VMEM_LIMIT),
        name="wprep",
    )(w_t, w_t)


def _inproj_kernel(x_ref, g_ref, w_ref, wg_ref, o_ref, gates_ref, h_scr):
    @pl.when(pl.program_id(1) == 0)
    def _():
        hb = _rms(x_ref[...], g_ref[...]).astype(BF16)
        h_scr[...] = hb
        gt = lax.dot_general(wg_ref[...], hb, (((1,), (1,)), ((), ())), preferred_element_type=F32)
        for h in range(ML_HEADS):
            gates_ref[h] = gt[4 * h:4 * h + 4, :]

    o_ref[...] = lax.dot_general(h_scr[...], w_ref[...], (((1,), (1,)), ((), ())),
                                 preferred_element_type=F32).astype(BF16)


def _inproj(x2, g, w_main, w_gate, tm, tn):
    T, D = x2.shape
    N = w_main.shape[0]
    return pl.pallas_call(
        _inproj_kernel,
        grid=(T // tm, N // tn),
        in_specs=[
            pl.BlockSpec((tm, D), lambda i, j: (i, 0)),
            pl.BlockSpec((1, D), lambda i, j: (0, 0)),
            pl.BlockSpec((tn, D), lambda i, j: (j, 0)),
            pl.BlockSpec(w_gate.shape, lambda i, j: (0, 0)),
        ],
        out_specs=[
            pl.BlockSpec((tm, tn), lambda i, j: (i, j)),
            pl.BlockSpec((ML_HEADS, 4, tm), lambda i, j: (0, 0, i)),
        ],
        out_shape=[
            jax.ShapeDtypeStruct((T, N), BF16),
            jax.ShapeDtypeStruct((ML_HEADS, 4, T), F32),
        ],
        scratch_shapes=[pltpu.VMEM((tm, D), BF16)],
        compiler_params=_params(("parallel", "arbitrary"), VMEM_LIMIT),
        name="inproj",
    )(x2, g, w_main, w_gate)


def _log_sigmoid(x):
    return jnp.minimum(x, 0.0) - jnp.log(1.0 + jnp.exp(-jnp.abs(x)))


def _chunk_scan(x, pos, L, reverse, op, identity):
    n = x.shape[1]
    d = 1
    while d < L:
        if reverse:
            x = op(x, jnp.where(pos < L - d, pltpu.roll(x, n - d, 1), identity))
        else:
            x = op(x, jnp.where(pos >= d, pltpu.roll(x, d, 1), identity))
        d *= 2
    return x


def _mlstm_kernel(q_ref, k_ref, v_ref, o_ref, gr_ref, br_ref,
                  wq_ref, wk_ref, bq_ref, bk_ref, ng_ref, y_ref,
                  qt_scr, k_scr, vat_scr, bb_scr, hf_scr, hb_scr):
    S = q_ref.shape[1]
    L = ML_CHUNK
    NC = S // L
    Dh = ML_HEAD_DIM

    pos = lax.broadcasted_iota(jnp.int32, (NC, L), 1)
    gates = []
    for d, rev in ((0, False), (1, True)):
        i_pre = gr_ref[0, 2 * d] + br_ref[0, 2 * d:2 * d + 1, :]
        lf = _log_sigmoid(gr_ref[0, 2 * d + 1] + br_ref[0, 2 * d + 1:2 * d + 2, :])
        a = _chunk_scan(lf, pos, L, rev, jnp.add, 0.0)
        b = i_pre - a
        gates.append((b, a, _chunk_scan(b, pos, L, rev, jnp.maximum, -jnp.inf), lf))

    def conv_silu(x_ref, w_ref, b_ref, c):
        x = x_ref[0, c * L:(c + 1) * L, :].astype(F32)
        zero_row = jnp.zeros((1, Dh), F32)
        prev_row = x_ref[0, c * L - BF16_ROWS:c * L, :].astype(F32)[BF16_ROWS - 1:, :] if c > 0 else zero_row
        next_row = x_ref[0, (c + 1) * L:(c + 1) * L + BF16_ROWS, :].astype(F32)[0:1, :] if c < NC - 1 else zero_row
        row = lax.broadcasted_iota(jnp.int32, x.shape, 0)
        x_prev = jnp.where(row == 0, prev_row, pltpu.roll(x, 1, 0))
        x_next = jnp.where(row == L - 1, next_row, pltpu.roll(x, L - 1, 0))
        w = w_ref[...]
        y = x_prev * w[0:1] + x * w[1:2] + x_next * w[2:3] + b_ref[...]
        return y * jax.nn.sigmoid(y)

    vat_scr[Dh:, :] = jnp.ones((Dh, S), BF16)

    def prep(c):
        cs = slice(c * L, (c + 1) * L)
        qt_scr[:, cs] = (conv_silu(q_ref, wq_ref, bq_ref, c) * (Dh ** -0.5)).astype(BF16).T
        k_scr[cs, :] = conv_silu(k_ref, wk_ref, bk_ref, c).astype(BF16)
        vat_scr[:Dh, cs] = v_ref[0, cs, :].T
        for d in range(2):
            bb_scr[d, cs, :] = jnp.broadcast_to(gates[d][0][c:c + 1, :], (LANES, L)).T

    s_idx = lax.broadcasted_iota(jnp.int32, (L, L), 0)
    j_idx = lax.broadcasted_iota(jnp.int32, (L, L), 1)

    def chunk(c, m, st_t, h_scr, fwd):
        cs = slice(c * L, (c + 1) * L)
        d = 0 if fwd else 1
        b_row, a_row, bmax_row, lf_row = (t[c:c + 1, :] for t in gates[d])
        mask_t = (s_idx <= j_idx) if fwd else (s_idx >= j_idx)
        g = jnp.sum(lf_row, axis=1, keepdims=True)
        w_end = g + b_row
        m_new = jnp.maximum(g + m, jnp.max(w_end, axis=1, keepdims=True))
        decay = jnp.exp(g + m - m_new)
        wt = jnp.exp(w_end - m_new)
        c_row = jnp.maximum(m, bmax_row)
        inter_w = jnp.exp(m - c_row)
        floor = jnp.exp(-(a_row + c_row))
        k = k_scr[cs, :]
        q_t = qt_scr[:, cs]
        va_t = vat_scr[:, cs]
        qk_t = jnp.dot(k, q_t, preferred_element_type=F32)
        b_col = bb_scr[d, cs, :]
        expo = jnp.concatenate([b_col] * (L // LANES), axis=1) - c_row
        p_t = jnp.exp(jnp.where(mask_t, expo, -jnp.inf)) * qk_t
        q_in = (q_t.astype(F32) * inter_w).astype(BF16)
        nd_t = (jnp.dot(va_t, p_t.astype(BF16), preferred_element_type=F32)
                + jnp.dot(st_t.astype(BF16), q_in, preferred_element_type=F32))
        h_scr[:, cs] = nd_t[:Dh] / jnp.maximum(jnp.abs(nd_t[Dh:]), floor)
        vw = jnp.concatenate([(va_t[:Dh].astype(F32) * wt).astype(BF16),
                              jnp.broadcast_to(wt, (Dh, L)).astype(BF16)], axis=0)
        return m_new, decay * st_t + jnp.dot(vw, k, preferred_element_type=F32)

    def finish(c):
        cs = slice(c * L, (c + 1) * L)
        h_t = hf_scr[:, cs] + hb_scr[:, cs]
        h_t = h_t * lax.rsqrt(jnp.mean(h_t * h_t, axis=0, keepdims=True) + EPS)
        y = h_t.T * ng_ref[...] * jax.nn.sigmoid(o_ref[0, cs, :].astype(F32))
        y_ref[0, cs, :] = y.astype(BF16)

    m_f = m_b = jnp.full((1, 1), NEG_INIT, F32)
    st_f = st_b = jnp.zeros((2 * Dh, Dh), F32)
    prep(0)
    prep(NC - 1)
    for c in range(NC):
        if c + 1 < NC - 1 - (c + 1):
            prep(c + 1)
            prep(NC - 2 - c)
        elif c + 1 == NC - 1 - (c + 1):
            prep(c + 1)
        m_f, st_f = chunk(c, m_f, st_f, hf_scr, True)
        m_b, st_b = chunk(NC - 1 - c, m_b, st_b, hb_scr, False)
        if c >= NC - 1 - c:
            finish(c)
            if c != NC - 1 - c:
                finish(NC - 1 - c)


def _mlstm(proj3, grow, brow, w_conv, b_conv, norm_g):
    B, S, _ = proj3.shape
    H, Dh = ML_HEADS, ML_HEAD_DIM
    blk = lambda off: pl.BlockSpec((1, S, Dh), lambda b, h: (b, 0, off + h))
    return pl.pallas_call(
        _mlstm_kernel,
        grid=(B, H),
        in_specs=[
            blk(0), blk(H), blk(2 * H), blk(3 * H),
            pl.BlockSpec((1, 4, S // ML_CHUNK, ML_CHUNK), lambda b, h: (h, 0, b, 0)),
            pl.BlockSpec((1, 4, 1), lambda b, h: (h, 0, 0)),
            pl.BlockSpec((3, Dh), lambda b, h: (0, h)),
            pl.BlockSpec((3, Dh), lambda b, h: (0, H + h)),
            pl.BlockSpec((1, Dh), lambda b, h: (0, h)),
            pl.BlockSpec((1, Dh), lambda b, h: (0, H + h)),
            pl.BlockSpec((1, Dh), lambda b, h: (0, h)),
        ],
        out_specs=pl.BlockSpec((1, S, Dh), lambda b, h: (b, 0, h)),
        out_shape=jax.ShapeDtypeStruct((B, S, ML_WIDTH), BF16),
        scratch_shapes=[
            pltpu.VMEM((Dh, S), BF16), pltpu.VMEM((S, Dh), BF16), pltpu.VMEM((2 * Dh, S), BF16),
            pltpu.VMEM((2, S, LANES), F32),
            pltpu.VMEM((Dh, S), F32), pltpu.VMEM((Dh, S), F32),
        ],
        compiler_params=_params(("parallel", "parallel"), VMEM_LIMIT),
        name="mlstm",
    )(proj3, proj3, proj3, proj3, grow, brow, w_conv, w_conv, b_conv, b_conv, norm_g)


def _na_kernel(q_ref, k_ref, v_ref, pat_ref, y_ref, t_scr):
    S = k_ref.shape[1]
    rows = S // GRID_W
    W = GRID_W
    n_pairs = NA_HEADS // 2
    n_dr = t_scr.shape[0]
    r0 = pl.program_id(1) * NA_ROWS_PER_STEP
    low = lax.broadcasted_iota(jnp.int32, (W, LANES), 1) < NA_HEAD_DIM

    @pl.when((pl.program_id(0) == 0) & (pl.program_id(1) == 0))
    def _():
        cq = lax.broadcasted_iota(jnp.int32, (W, LANES), 0)
        ck = lax.broadcasted_iota(jnp.int32, (W, LANES), 1) % W
        col_start = jnp.clip(cq - NA_WIN_COLS // 2, 0, W - NA_WIN_COLS)
        col_ok = (ck >= col_start) & (ck < col_start + NA_WIN_COLS)
        for d in range(n_dr):
            for h in range(NA_HEADS):
                pattern = jnp.broadcast_to(pat_ref[h, d:d + 1, :], (W, LANES))
                skewed = pltpu.roll(pattern, 0, 1, stride=1, stride_axis=0)
                t_scr[d, h // 2, (h % 2) * W:(h % 2 + 1) * W, :] = jnp.where(col_ok, skewed, -jnp.inf)

    ones = jnp.ones((NA_WIN_ROWS * W, LANES), BF16)

    def row_body(i, _):
        r = r0 + i
        row_start = jnp.clip(r - NA_WIN_ROWS // 2, 0, rows - NA_WIN_ROWS)
        dr0 = row_start - r + (NA_WIN_ROWS - 1)
        band = pl.ds(pl.multiple_of(row_start * W, W), NA_WIN_ROWS * W)
        qrow = pl.ds(pl.multiple_of(i * W, W), W)
        scores = []
        for hp in range(n_pairs):
            cols = slice(hp * LANES, (hp + 1) * LANES)
            q2 = q_ref[0, qrow, cols] * (NA_HEAD_DIM ** -0.5)
            zero = jnp.zeros_like(q2)
            qs = jnp.concatenate([jnp.where(low, q2, zero), jnp.where(low, zero, q2)], axis=0)
            s = lax.dot_general(qs, k_ref[0, band, cols], (((1,), (1,)), ((), ())),
                                preferred_element_type=F32)
            bias = jnp.concatenate([t_scr[dr0 + kk, hp] for kk in range(0, NA_WIN_ROWS, 2)], axis=-1)
            scores.append(s + bias)
        probs = [jnp.exp(s - jnp.max(s, axis=-1, keepdims=True)).astype(BF16) for s in scores]
        for hp in range(n_pairs):
            cols = slice(hp * LANES, (hp + 1) * LANES)
            v_aug = jnp.concatenate([v_ref[0, band, cols], ones], axis=1)
            o = jnp.dot(probs[hp], v_aug, preferred_element_type=F32)
            o = o[:, :LANES] / o[:, LANES:]
            y_ref[0, qrow, cols] = jnp.where(low, o[:W], o[W:]).astype(BF16)
        return 0

    lax.fori_loop(0, NA_ROWS_PER_STEP, row_body, 0, unroll=2)


def _na(proj3, patterns, q_blk, k_blk, v_blk):
    B, S, _ = proj3.shape
    rows = S // GRID_W
    rb = NA_ROWS_PER_STEP
    n_dr = patterns.shape[1]
    return pl.pallas_call(
        _na_kernel,
        grid=(B, rows // rb),
        in_specs=[
            pl.BlockSpec((1, rb * GRID_W, NA_WIDTH), lambda b, r: (b, r, q_blk)),
            pl.BlockSpec((1, S, NA_WIDTH), lambda b, r: (b, 0, k_blk)),
            pl.BlockSpec((1, S, NA_WIDTH), lambda b, r: (b, 0, v_blk)),
            pl.BlockSpec(patterns.shape, lambda b, r: (0, 0, 0)),
        ],
        out_specs=pl.BlockSpec((1, rb * GRID_W, NA_WIDTH), lambda b, r: (b, r, 0)),
        out_shape=jax.ShapeDtypeStruct((B, S, NA_WIDTH), BF16),
        scratch_shapes=[pltpu.VMEM((n_dr, NA_HEADS // 2, 2 * GRID_W, LANES), F32)],
        compiler_params=_params(("arbitrary", "arbitrary"), VMEM_LIMIT),
        name="natten",
    )(proj3, proj3, proj3, patterns)


def _na_bias_patterns(rpb):
    c0 = NA_WIN_COLS - 1
    r = rpb.astype(F32)
    gap = jnp.zeros(r.shape[:1] + (r.shape[1] - 1, LANES // 2 - NA_WIN_COLS - c0), F32)
    return jnp.concatenate([r[:, :-1, c0:], gap, r[:, 1:, :], gap, r[:, :-1, :c0]], axis=-1)


def _memkv_kernel(m_ref, g_ref, w_ref, o_ref):
    hb = _rms(m_ref[...], g_ref[...]).astype(BF16)
    o_ref[...] = jnp.dot(hb, w_ref[...], preferred_element_type=F32).astype(BF16)


def _memkv(mem2, g, w, tm):
    T, D = mem2.shape
    N = w.shape[1]
    return pl.pallas_call(
        _memkv_kernel,
        grid=(T // tm,),
        in_specs=[
            pl.BlockSpec((tm, D), lambda i: (i, 0)),
            pl.BlockSpec((1, D), lambda i: (0, 0)),
            pl.BlockSpec((D, N), lambda i: (0, 0)),
        ],
        out_specs=pl.BlockSpec((tm, N), lambda i: (i, 0)),
        out_shape=jax.ShapeDtypeStruct((T, N), BF16),
        compiler_params=_params(("parallel",), VMEM_LIMIT),
        name="memkv",
    )(mem2, g, w)


def _xa_kernel(q_ref, kv_ref, y_ref):
    Dh = XA_HEAD_DIM
    for h in range(XA_HEADS):
        q = q_ref[0, :, h * Dh:(h + 1) * Dh]
        k = kv_ref[0, :, h * Dh:(h + 1) * Dh]
        v = kv_ref[0, :, XA_WIDTH + h * Dh:XA_WIDTH + (h + 1) * Dh]
        s = lax.dot_general(q, k, (((1,), (1,)), ((), ())), preferred_element_type=F32) * (Dh ** -0.5)
        p = jnp.exp(s - jnp.max(s, axis=-1, keepdims=True))
        o = jnp.dot(p.astype(BF16), v, preferred_element_type=F32)
        y_ref[0, :, h * Dh:(h + 1) * Dh] = (o / jnp.sum(p, axis=-1, keepdims=True)).astype(BF16)


def _xa(proj3, kv3, q_blk, tq):
    B, S, _ = proj3.shape
    M = kv3.shape[1]
    return pl.pallas_call(
        _xa_kernel,
        grid=(B, S // tq),
        in_specs=[
            pl.BlockSpec((1, tq, XA_WIDTH), lambda b, i: (b, i, q_blk)),
            pl.BlockSpec((1, M, 2 * XA_WIDTH), lambda b, i: (b, 0, 0)),
        ],
        out_specs=pl.BlockSpec((1, tq, XA_WIDTH), lambda b, i: (b, i, 0)),
        out_shape=jax.ShapeDtypeStruct((B, S, XA_WIDTH), BF16),
        compiler_params=_params(("parallel", "parallel"), VMEM_LIMIT),
        name="memxattn",
    )(proj3, kv3)


def _merge_kernel(x_ref, yml_ref, yna_ref, yxa_ref, p0_ref, p1_ref, p2_ref, bg_ref,
                  wml_ref, wna_ref, wxa_ref, wo_ref, o_ref):
    D = x_ref.shape[1]
    merged = None
    for n, (y_ref, p_ref, w_ref) in enumerate(
            ((yml_ref, p0_ref, wml_ref), (yna_ref, p1_ref, wna_ref), (yxa_ref, p2_ref, wxa_ref))):
        gate = jax.nn.sigmoid(p_ref[...].astype(F32) + bg_ref[:, n * D:(n + 1) * D])
        term = gate * jnp.dot(y_ref[...], w_ref[...], preferred_element_type=F32)
        merged = term if merged is None else merged + term
    o_ref[...] = x_ref[...] + jnp.dot(merged.astype(BF16), wo_ref[...], preferred_element_type=F32)


def _merge(x2, yml, yna, yxa, proj2, b_gate, wml, wna, wxa, wo, gate_blk, tm):
    T, D = x2.shape
    row = lambda w: pl.BlockSpec((tm, w), lambda i: (i, 0))
    full = lambda a: pl.BlockSpec(a.shape, lambda i: (0, 0))
    return pl.pallas_call(
        _merge_kernel,
        grid=(T // tm,),
        in_specs=[
            row(D), row(ML_WIDTH), row(NA_WIDTH), row(XA_WIDTH),
            pl.BlockSpec((tm, D), lambda i: (i, gate_blk)),
            pl.BlockSpec((tm, D), lambda i: (i, gate_blk + 1)),
            pl.BlockSpec((tm, D), lambda i: (i, gate_blk + 2)),
            full(b_gate), full(wml), full(wna), full(wxa), full(wo),
        ],
        out_specs=row(D),
        out_shape=jax.ShapeDtypeStruct((T, D), F32),
        compiler_params=_params(("parallel",), VMEM_LIMIT),
        name="merge",
    )(x2, yml, yna, yxa, proj2, proj2, proj2, b_gate, wml, wna, wxa, wo)


def _gelu_tanh(x):
    return 0.5 * x * (1.0 + jnp.tanh(0.7978845608028654 * (x + 0.044715 * (x * x * x))))


def _ffn_kernel(x_ref, g_ref, wa0_ref, wa1_ref, wu0_ref, wu1_ref, wc0_ref, wc1_ref, bc0_ref, bc1_ref,
                wd0_ref, wd1_ref, gf_ref, o_ref, h_scr, a0_scr, u0_scr, a1_scr, u1_scr):
    t = pl.program_id(1)
    last = pl.num_programs(1) - 1
    S = x_ref.shape[1]

    def produce(wa_ref, wu_ref, a_scr, u_scr):
        h = h_scr[...]
        a_scr[...] = jnp.dot(h, wa_ref[...], preferred_element_type=F32)
        u_scr[...] = jnp.dot(h, wu_ref[...], preferred_element_type=F32)

    def consume(a_scr, u_scr, wc_ref, bc_ref, wd_ref):
        a = a_scr[...]
        row = lax.broadcasted_iota(jnp.int32, a.shape, 0)
        a_prev = jnp.where(row == 0, 0.0, pltpu.roll(a, 1, 0))
        a_next = jnp.where(row == S - 1, 0.0, pltpu.roll(a, S - 1, 0))
        w = wc_ref[...]
        conv = a_prev * w[0:1] + a * w[1:2] + a_next * w[2:3] + bc_ref[...]
        act = (_gelu_tanh(conv) * u_scr[...]).astype(BF16)
        o_ref[0] += jnp.dot(act, wd_ref[...], preferred_element_type=F32)

    @pl.when(t == 0)
    def _():
        x = x_ref[0]
        h_scr[...] = _rms(x, g_ref[...]).astype(BF16)
        o_ref[0] = x
        produce(wa0_ref, wu0_ref, a0_scr, u0_scr)
        produce(wa1_ref, wu1_ref, a1_scr, u1_scr)
        consume(a0_scr, u0_scr, wc1_ref, bc1_ref, wd1_ref)

    @pl.when((t > 0) & (t < last))
    def _():
        produce(wa0_ref, wu0_ref, a0_scr, u0_scr)
        consume(a1_scr, u1_scr, wc0_ref, bc0_ref, wd0_ref)
        produce(wa1_ref, wu1_ref, a1_scr, u1_scr)
        consume(a0_scr, u0_scr, wc1_ref, bc1_ref, wd1_ref)

    @pl.when(t == last)
    def _():
        produce(wa0_ref, wu0_ref, a0_scr, u0_scr)
        consume(a1_scr, u1_scr, wc0_ref, bc0_ref, wd0_ref)
        consume(a0_scr, u0_scr, wc1_ref, bc1_ref, wd1_ref)
        o_ref[0] = _rms(o_ref[0], gf_ref[...])


def _ffn(x3, g, w_up, w_conv, b_conv, w_down, g_final, tf):
    B, S, D = x3.shape
    FF = w_down.shape[0]
    nf = FF // tf
    assert nf % 2 == 1, "the two-chunk pipeline ends on a produce-one / consume-two step"
    steps = (nf + 1) // 2
    prod0 = lambda t: 2 * t
    prod1 = lambda t: jnp.minimum(2 * t + 1, nf - 1)
    cons0 = lambda t: jnp.maximum(2 * t - 1, 0)
    cons1 = lambda t: 2 * t
    col = lambda f, off=0: pl.BlockSpec((D, tf), lambda b, t: (0, off + f(t)))
    vec = lambda rows, f: pl.BlockSpec((rows, tf), lambda b, t: (0, f(t)))
    dn = lambda f: pl.BlockSpec((tf, D), lambda b, t: (f(t), 0))
    buf = pltpu.VMEM((S, tf), F32)
    return pl.pallas_call(
        _ffn_kernel,
        grid=(B, steps),
        in_specs=[
            pl.BlockSpec((1, S, D), lambda b, t: (b, 0, 0)),
            pl.BlockSpec((1, D), lambda b, t: (0, 0)),
            col(prod0), col(prod1), col(prod0, nf), col(prod1, nf),
            vec(3, cons0), vec(3, cons1), vec(1, cons0), vec(1, cons1),
            dn(cons0), dn(cons1),
            pl.BlockSpec((1, D), lambda b, t: (0, 0)),
        ],
        out_specs=pl.BlockSpec((1, S, D), lambda b, t: (b, 0, 0)),
        out_shape=jax.ShapeDtypeStruct((B, S, D), F32),
        scratch_shapes=[pltpu.VMEM((S, D), BF16), buf, buf, buf, buf],
        compiler_params=_params(("parallel", "arbitrary"), VMEM_LIMIT),
        name="ffn",
    )(x3, g, w_up, w_up, w_up, w_up, w_conv, w_conv, b_conv, b_conv, w_down, w_down, g_final)


def kernel(x, mem, mix_norm_g, w_in, b_ml_igate, b_ml_fgate, w_ml_conv, b_ml_conv, ml_norm_g, na_rpb, mem_norm_g, w_mem_kv, b_merge_gate, w_br_ml, w_br_na, w_br_xa, w_out, ffn_norm_g, w_ffn_up, w_ffn_conv, b_ffn_conv, w_ffn_down, final_norm_g):
    B, S, D = x.shape
    H = ML_HEADS
    M = mem.shape[1]
    T = B * S
    assert w_in.shape[0] == 1, "single-layer block: the FFN kernel also applies the final norm"
    assert S % ML_CHUNK == 0 and S % (GRID_W * NA_ROWS_PER_STEP) == 0
    l = 0
    n_gate = 4 * H
    g0 = 4 * ML_WIDTH
    row2 = lambda v: v.reshape(1, -1).astype(F32)

    w_main, w_gate = _wprep(jnp.transpose(w_in[l]), g0, n_gate, tr=256)
    w_gate = jnp.transpose(w_gate.reshape(4, H, D), (1, 0, 2)).reshape(n_gate, D).astype(BF16)

    proj2, grow = _inproj(x.reshape(T, D), row2(mix_norm_g[l]), w_main, w_gate, tm=1024, tn=3584)
    proj3 = proj2.reshape(B, S, -1)

    bias = jnp.stack([b_ml_igate[l][0], b_ml_fgate[l][0], b_ml_igate[l][1], b_ml_fgate[l][1]],
                     axis=-1).astype(F32)
    y_ml = _mlstm(proj3, grow.reshape(H, 4, T // ML_CHUNK, ML_CHUNK), bias[:, :, None],
                  w_ml_conv[l].astype(F32), row2(b_ml_conv[l]), row2(ml_norm_g[l]))

    nb = (4 * ML_WIDTH) // NA_WIDTH
    y_na = _na(proj3, _na_bias_patterns(na_rpb[l]), nb, nb + 1, nb + 2)

    kv = _memkv(mem.reshape(B * M, D), row2(mem_norm_g[l]), w_mem_kv[l].astype(BF16), tm=512)
    xb = (4 * ML_WIDTH + 3 * NA_WIDTH) // XA_WIDTH
    y_xa = _xa(proj3, kv.reshape(B, M, -1), xb, tq=1024)

    gate_blk = (4 * ML_WIDTH + 3 * NA_WIDTH + XA_WIDTH) // D
    x1 = _merge(x.reshape(T, D), y_ml.reshape(T, -1), y_na.reshape(T, -1), y_xa.reshape(T, -1),
                proj2, row2(b_merge_gate[l]), w_br_ml[l].astype(BF16), w_br_na[l].astype(BF16),
                w_br_xa[l].astype(BF16), w_out[l].astype(BF16), gate_blk, tm=512)

    return _ffn(x1.reshape(B, S, D), row2(ffn_norm_g[l]), w_ffn_up[l].astype(BF16),
                w_ffn_conv[l].astype(F32), row2(b_ffn_conv[l]), w_ffn_down[l].astype(BF16),
                row2(final_norm_g), tf=256)
```

```python
import functools

import jax
import jax.numpy as jnp
from jax import lax
from jax.experimental import pallas as pl
from jax.experimental.pallas import tpu as pltpu

F32 = jnp.float32
BF16 = jnp.bfloat16
EPS = 1e-6
NEG_INIT = -1e30
LOG2E = 1.4426950408889634

GRID_W = 64
ML_HEADS = 4
ML_HEAD_DIM = 128
ML_WIDTH = ML_HEADS * ML_HEAD_DIM
NA_HEADS = 8
NA_HEAD_DIM = 64
NA_WIDTH = NA_HEADS * NA_HEAD_DIM
NA_WIN_ROWS = 8
NA_WIN_COLS = 16
XA_HEADS = 4
XA_HEAD_DIM = 128
XA_WIDTH = XA_HEADS * XA_HEAD_DIM

LANES = 128
F32_ROWS = 8
BF16_ROWS = 16
ML_CHUNK = 256
NA_ROWS_PER_STEP = 8
VMEM_LIMIT = 60 * 1024 * 1024


def _params(dims, vmem=None):
    return pltpu.CompilerParams(dimension_semantics=dims, vmem_limit_bytes=vmem)


def _rms(x, g):
    return x * lax.rsqrt(jnp.mean(x * x, axis=-1, keepdims=True) + EPS) * g


def _wprep_kernel(w_ref, wg_ref, o_ref, og_ref):
    o_ref[...] = w_ref[...].astype(BF16)
    og_ref[...] = wg_ref[...]


def _wprep(w_t, g0, n_gate, tr):
    N, D = w_t.shape
    assert g0 % tr == 0 and g0 % n_gate == 0 and (N - n_gate) % tr == 0
    return pl.pallas_call(
        _wprep_kernel,
        grid=((N - n_gate) // tr,),
        in_specs=[
            pl.BlockSpec((pl.Element(tr), pl.Element(D)),
                         lambda i: (pl.multiple_of(i * tr + jnp.where(i * tr >= g0, n_gate, 0), n_gate), 0)),
            pl.BlockSpec((n_gate, D), lambda i: (g0 // n_gate, 0)),
        ],
        out_specs=[pl.BlockSpec((tr, D), lambda i: (i, 0)),
                   pl.BlockSpec((n_gate, D), lambda i: (0, 0))],
        out_shape=[jax.ShapeDtypeStruct((N - n_gate, D), BF16),
                   jax.ShapeDtypeStruct((n_gate, D), F32)],
        compiler_params=_params(("arbitrary",), VMEM_LIMIT),
        name="wprep",
    )(w_t, w_t)


def _inproj_kernel(x_ref, g_ref, w_ref, wg_ref, o_ref, gates_ref, h_scr):
    @pl.when(pl.program_id(1) == 0)
    def _():
        hb = _rms(x_ref[...], g_ref[...]).astype(BF16)
        h_scr[...] = hb
        gt = lax.dot_general(wg_ref[...], hb, (((1,), (1,)), ((), ())), preferred_element_type=F32)
        for h in range(ML_HEADS):
            gates_ref[h] = gt[4 * h:4 * h + 4, :]

    o_ref[...] = lax.dot_general(h_scr[...], w_ref[...], (((1,), (1,)), ((), ())),
                                 preferred_element_type=F32).astype(BF16)


def _inproj(x2, g, w_main, w_gate, tm, tn):
    T, D = x2.shape
    N = w_main.shape[0]
    return pl.pallas_call(
        _inproj_kernel,
        grid=(T // tm, N // tn),
        in_specs=[
            pl.BlockSpec((tm, D), lambda i, j: (i, 0)),
            pl.BlockSpec((1, D), lambda i, j: (0, 0)),
            pl.BlockSpec((tn, D), lambda i, j: (j, 0)),
            pl.BlockSpec(w_gate.shape, lambda i, j: (0, 0)),
        ],
        out_specs=[
            pl.BlockSpec((tm, tn), lambda i, j: (i, j)),
            pl.BlockSpec((ML_HEADS, 4, tm), lambda i, j: (0, 0, i)),
        ],
        out_shape=[
            jax.ShapeDtypeStruct((T, N), BF16),
            jax.ShapeDtypeStruct((ML_HEADS, 4, T), F32),
        ],
        scratch_shapes=[pltpu.VMEM((tm, D), BF16)],
        compiler_params=_params(("parallel", "arbitrary"), VMEM_LIMIT),
        name="inproj",
    )(x2, g, w_main, w_gate)


def _log_sigmoid(x):
    return jnp.minimum(x, 0.0) - jnp.log(1.0 + jnp.exp(-jnp.abs(x)))


def _chunk_scan(x, pos, L, reverse, op, identity):
    n = x.shape[1]
    d = 1
    while d < L:
        if reverse:
            x = op(x, jnp.where(pos < L - d, pltpu.roll(x, n - d, 1), identity))
        else:
            x = op(x, jnp.where(pos >= d, pltpu.roll(x, d, 1), identity))
        d *= 2
    return x


def _mlstm_kernel(q_ref, k_ref, v_ref, o_ref, gr_ref, br_ref,
                  wq_ref, wk_ref, bq_ref, bk_ref, ng_ref, y_ref,
                  qt_scr, k_scr, vat_scr, bb_scr, hf_scr, hb_scr):
    S = q_ref.shape[1]
    L = ML_CHUNK
    NC = S // L
    Dh = ML_HEAD_DIM

    pos = lax.broadcasted_iota(jnp.int32, (NC, L), 1)
    gates = []
    for d, rev in ((0, False), (1, True)):
        i_pre = gr_ref[0, 2 * d] + br_ref[0, 2 * d:2 * d + 1, :]
        lf = _log_sigmoid(gr_ref[0, 2 * d + 1] + br_ref[0, 2 * d + 1:2 * d + 2, :])
        a = _chunk_scan(lf, pos, L, rev, jnp.add, 0.0)
        b = i_pre - a
        gates.append((b, a, _chunk_scan(b, pos, L, rev, jnp.maximum, -jnp.inf), lf))

    def conv_silu(x_ref, w_ref, b_ref, c):
        x = x_ref[0, c * L:(c + 1) * L, :].astype(F32)
        zero_row = jnp.zeros((1, Dh), F32)
        prev_row = x_ref[0, c * L - BF16_ROWS:c * L, :].astype(F32)[BF16_ROWS - 1:, :] if c > 0 else zero_row
        next_row = x_ref[0, (c + 1) * L:(c + 1) * L + BF16_ROWS, :].astype(F32)[0:1, :] if c < NC - 1 else zero_row
        tile = F32_ROWS
        row = lax.broadcasted_iota(jnp.int32, (tile, Dh), 0)
        x_prev = pltpu.roll(x, 1, 0)
        x_prev = jnp.concatenate([jnp.where(row == 0, prev_row, x_prev[:tile]), x_prev[tile:]], axis=0)
        x_next = pltpu.roll(x, L - 1, 0)
        x_next = jnp.concatenate([x_next[:L - tile], jnp.where(row == tile - 1, next_row, x_next[L - tile:])], axis=0)
        w = w_ref[...]
        y = x_prev * w[0:1] + x * w[1:2] + x_next * w[2:3] + b_ref[...]
        return y / (1.0 + jnp.exp2(y * (-LOG2E)))

    vat_scr[Dh:, :] = jnp.ones((Dh, S), BF16)

    def prep(c):
        cs = slice(c * L, (c + 1) * L)
        qt_scr[:, cs] = (conv_silu(q_ref, wq_ref, bq_ref, c) * (Dh ** -0.5)).astype(BF16).T
        k_scr[cs, :] = conv_silu(k_ref, wk_ref, bk_ref, c).astype(BF16)
        vat_scr[:Dh, cs] = v_ref[0, cs, :].T
        for d in range(2):
            bb_scr[d, cs, :] = jnp.broadcast_to(gates[d][0][c:c + 1, :] * LOG2E, (LANES, L)).T

    s_idx = lax.broadcasted_iota(jnp.int32, (L, L), 0)
    j_idx = lax.broadcasted_iota(jnp.int32, (L, L), 1)

    def chunk(c, m, st_t, h_scr, fwd):
        cs = slice(c * L, (c + 1) * L)
        d = 0 if fwd else 1
        b_row, a_row, bmax_row, lf_row = (t[c:c + 1, :] for t in gates[d])
        mask_t = (s_idx <= j_idx) if fwd else (s_idx >= j_idx)
        g = jnp.sum(lf_row, axis=1, keepdims=True)
        w_end = g + b_row
        m_new = jnp.maximum(g + m, jnp.max(w_end, axis=1, keepdims=True))
        decay = jnp.exp(g + m - m_new)
        wt = jnp.exp(w_end - m_new)
        c_row = jnp.maximum(m, bmax_row)
        inter_w = jnp.exp(m - c_row)
        floor = jnp.exp(-(a_row + c_row))
        k = k_scr[cs, :]
        q_t = qt_scr[:, cs]
        va_t = vat_scr[:, cs]
        qk_t = jnp.dot(k, q_t, preferred_element_type=F32)
        b_col = bb_scr[d, cs, :]
        expo = jnp.concatenate([b_col] * (L // LANES), axis=1) - c_row * LOG2E
        p_t = jnp.exp2(jnp.where(mask_t, expo, -jnp.inf)) * qk_t
        q_in = q_t * inter_w.astype(BF16)
        nd_t = (jnp.dot(va_t, p_t.astype(BF16), preferred_element_type=F32)
                + jnp.dot(st_t.astype(BF16), q_in, preferred_element_type=F32))
        h_scr[:, cs] = nd_t[:Dh] / jnp.maximum(jnp.abs(nd_t[Dh:]), floor)
        wt_b = wt.astype(BF16)
        vw = jnp.concatenate([va_t[:Dh] * wt_b, jnp.broadcast_to(wt_b, (Dh, L))], axis=0)
        return m_new, decay * st_t + jnp.dot(vw, k, preferred_element_type=F32)

    def finish(c):
        cs = slice(c * L, (c + 1) * L)
        h_t = hf_scr[:, cs] + hb_scr[:, cs]
        h_t = h_t * lax.rsqrt(jnp.mean(h_t * h_t, axis=0, keepdims=True) + EPS)
        y = h_t.T * ng_ref[...] * jax.nn.sigmoid(o_ref[0, cs, :].astype(F32))
        y_ref[0, cs, :] = y.astype(BF16)

    m_f = m_b = jnp.full((1, 1), NEG_INIT, F32)
    st_f = st_b = jnp.zeros((2 * Dh, Dh), F32)
    prep(0)
    prep(NC - 1)
    for c in range(NC):
        if c + 1 < NC - 1 - (c + 1):
            prep(c + 1)
            prep(NC - 2 - c)
        elif c + 1 == NC - 1 - (c + 1):
            prep(c + 1)
        m_f, st_f = chunk(c, m_f, st_f, hf_scr, True)
        m_b, st_b = chunk(NC - 1 - c, m_b, st_b, hb_scr, False)
        if c >= NC - 1 - c:
            finish(c)
            if c != NC - 1 - c:
                finish(NC - 1 - c)


def _mlstm(proj3, grow, brow, w_conv, b_conv, norm_g):
    B, S, _ = proj3.shape
    H, Dh = ML_HEADS, ML_HEAD_DIM
    blk = lambda off: pl.BlockSpec((1, S, Dh), lambda b, h: (b, 0, off + h))
    return pl.pallas_call(
        _mlstm_kernel,
        grid=(B, H),
        in_specs=[
            blk(0), blk(H), blk(2 * H), blk(3 * H),
            pl.BlockSpec((1, 4, S // ML_CHUNK, ML_CHUNK), lambda b, h: (h, 0, b, 0)),
            pl.BlockSpec((1, 4, 1), lambda b, h: (h, 0, 0)),
            pl.BlockSpec((3, Dh), lambda b, h: (0, h)),
            pl.BlockSpec((3, Dh), lambda b, h: (0, H + h)),
            pl.BlockSpec((1, Dh), lambda b, h: (0, h)),
            pl.BlockSpec((1, Dh), lambda b, h: (0, H + h)),
            pl.BlockSpec((1, Dh), lambda b, h: (0, h)),
        ],
        out_specs=pl.BlockSpec((1, S, Dh), lambda b, h: (b, 0, h)),
        out_shape=jax.ShapeDtypeStruct((B, S, ML_WIDTH), BF16),
        scratch_shapes=[
            pltpu.VMEM((Dh, S), BF16), pltpu.VMEM((S, Dh), BF16), pltpu.VMEM((2 * Dh, S), BF16),
            pltpu.VMEM((2, S, LANES), F32),
            pltpu.VMEM((Dh, S), F32), pltpu.VMEM((Dh, S), F32),
        ],
        compiler_params=_params(("parallel", "parallel"), VMEM_LIMIT),
        name="mlstm",
    )(proj3, proj3, proj3, proj3, grow, brow, w_conv, w_conv, b_conv, b_conv, norm_g)


def _na_kernel(q_ref, k_ref, v_ref, pat_ref, y_ref, t_scr):
    S = k_ref.shape[1]
    rows = S // GRID_W
    W = GRID_W
    n_pairs = NA_HEADS // 2
    n_dr = t_scr.shape[0]
    r0 = pl.program_id(1) * NA_ROWS_PER_STEP
    low = lax.broadcasted_iota(jnp.int32, (W, LANES), 1) < NA_HEAD_DIM

    @pl.when((pl.program_id(0) == 0) & (pl.program_id(1) == 0))
    def _():
        cq = lax.broadcasted_iota(jnp.int32, (W, LANES), 0)
        ck = lax.broadcasted_iota(jnp.int32, (W, LANES), 1) % W
        col_start = jnp.clip(cq - NA_WIN_COLS // 2, 0, W - NA_WIN_COLS)
        col_ok = (ck >= col_start) & (ck < col_start + NA_WIN_COLS)
        for d in range(n_dr):
            for h in range(NA_HEADS):
                pattern = jnp.broadcast_to(pat_ref[h, d:d + 1, :], (W, LANES))
                skewed = pltpu.roll(pattern, 0, 1, stride=1, stride_axis=0)
                t_scr[d, h // 2, (h % 2) * W:(h % 2 + 1) * W, :] = jnp.where(col_ok, skewed, -jnp.inf)

    ones = jnp.ones((NA_WIN_ROWS * W, LANES), BF16)

    def row_body(i, _):
        r = r0 + i
        row_start = jnp.clip(r - NA_WIN_ROWS // 2, 0, rows - NA_WIN_ROWS)
        dr0 = row_start - r + (NA_WIN_ROWS - 1)
        band = pl.ds(pl.multiple_of(row_start * W, W), NA_WIN_ROWS * W)
        qrow = pl.ds(pl.multiple_of(i * W, W), W)
        scores = []
        for hp in range(n_pairs):
            cols = slice(hp * LANES, (hp + 1) * LANES)
            q2 = q_ref[0, qrow, cols] * (NA_HEAD_DIM ** -0.5)
            zero = jnp.zeros_like(q2)
            qs = jnp.concatenate([jnp.where(low, q2, zero), jnp.where(low, zero, q2)], axis=0)
            s = lax.dot_general(qs, k_ref[0, band, cols], (((1,), (1,)), ((), ())),
                                preferred_element_type=F32)
            bias = jnp.concatenate([t_scr[dr0 + kk, hp] for kk in range(0, NA_WIN_ROWS, 2)], axis=-1)
            scores.append(s + bias)
        probs = [jnp.exp(s - jnp.max(s, axis=-1, keepdims=True)).astype(BF16) for s in scores]
        for hp in range(n_pairs):
            cols = slice(hp * LANES, (hp + 1) * LANES)
            v_aug = jnp.concatenate([v_ref[0, band, cols], ones], axis=1)
            o = jnp.dot(probs[hp], v_aug, preferred_element_type=F32)
            o = o[:, :LANES] / o[:, LANES:]
            y_ref[0, qrow, cols] = jnp.where(low, o[:W], o[W:]).astype(BF16)
        return 0

    lax.fori_loop(0, NA_ROWS_PER_STEP, row_body, 0, unroll=4)


def _na(proj3, patterns, q_blk, k_blk, v_blk):
    B, S, _ = proj3.shape
    rows = S // GRID_W
    rb = NA_ROWS_PER_STEP
    n_dr = patterns.shape[1]
    return pl.pallas_call(
        _na_kernel,
        grid=(B, rows // rb),
        in_specs=[
            pl.BlockSpec((1, rb * GRID_W, NA_WIDTH), lambda b, r: (b, r, q_blk)),
            pl.BlockSpec((1, S, NA_WIDTH), lambda b, r: (b, 0, k_blk)),
            pl.BlockSpec((1, S, NA_WIDTH), lambda b, r: (b, 0, v_blk)),
            pl.BlockSpec(patterns.shape, lambda b, r: (0, 0, 0)),
        ],
        out_specs=pl.BlockSpec((1, rb * GRID_W, NA_WIDTH), lambda b, r: (b, r, 0)),
        out_shape=jax.ShapeDtypeStruct((B, S, NA_WIDTH), BF16),
        scratch_shapes=[pltpu.VMEM((n_dr, NA_HEADS // 2, 2 * GRID_W, LANES), F32)],
        compiler_params=_params(("arbitrary", "arbitrary"), VMEM_LIMIT),
        name="natten",
    )(proj3, proj3, proj3, patterns)


def _na_bias_patterns(rpb):
    c0 = NA_WIN_COLS - 1
    r = rpb.astype(F32)
    gap = jnp.zeros(r.shape[:1] + (r.shape[1] - 1, LANES // 2 - NA_WIN_COLS - c0), F32)
    return jnp.concatenate([r[:, :-1, c0:], gap, r[:, 1:, :], gap, r[:, :-1, :c0]], axis=-1)


def _memkv_kernel(m_ref, g_ref, w_ref, o_ref):
    hb = _rms(m_ref[...], g_ref[...]).astype(BF16)
    o_ref[...] = jnp.dot(hb, w_ref[...], preferred_element_type=F32).astype(BF16)


def _memkv(mem2, g, w, tm):
    T, D = mem2.shape
    N = w.shape[1]
    return pl.pallas_call(
        _memkv_kernel,
        grid=(T // tm,),
        in_specs=[
            pl.BlockSpec((tm, D), lambda i: (i, 0)),
            pl.BlockSpec((1, D), lambda i: (0, 0)),
            pl.BlockSpec((D, N), lambda i: (0, 0)),
        ],
        out_specs=pl.BlockSpec((tm, N), lambda i: (i, 0)),
        out_shape=jax.ShapeDtypeStruct((T, N), BF16),
        compiler_params=_params(("parallel",), VMEM_LIMIT),
        name="memkv",
    )(mem2, g, w)


def _xa_kernel(q_ref, kv_ref, y_ref):
    Dh = XA_HEAD_DIM
    for h in range(XA_HEADS):
        q = q_ref[0, :, h * Dh:(h + 1) * Dh]
        k = kv_ref[0, :, h * Dh:(h + 1) * Dh]
        v = kv_ref[0, :, XA_WIDTH + h * Dh:XA_WIDTH + (h + 1) * Dh]
        s = lax.dot_general(q, k, (((1,), (1,)), ((), ())), preferred_element_type=F32) * (Dh ** -0.5)
        p = jnp.exp(s - jnp.max(s, axis=-1, keepdims=True))
        o = jnp.dot(p.astype(BF16), v, preferred_element_type=F32)
        y_ref[0, :, h * Dh:(h + 1) * Dh] = (o / jnp.sum(p, axis=-1, keepdims=True)).astype(BF16)


def _xa(proj3, kv3, q_blk, tq):
    B, S, _ = proj3.shape
    M = kv3.shape[1]
    return pl.pallas_call(
        _xa_kernel,
        grid=(B, S // tq),
        in_specs=[
            pl.BlockSpec((1, tq, XA_WIDTH), lambda b, i: (b, i, q_blk)),
            pl.BlockSpec((1, M, 2 * XA_WIDTH), lambda b, i: (b, 0, 0)),
        ],
        out_specs=pl.BlockSpec((1, tq, XA_WIDTH), lambda b, i: (b, i, 0)),
        out_shape=jax.ShapeDtypeStruct((B, S, XA_WIDTH), BF16),
        compiler_params=_params(("parallel", "parallel"), VMEM_LIMIT),
        name="memxattn",
    )(proj3, kv3)


def _merge_kernel(x_ref, yml_ref, yna_ref, yxa_ref, p0_ref, p1_ref, p2_ref, bg_ref,
                  wml_ref, wna_ref, wxa_ref, wo_ref, o_ref):
    D = x_ref.shape[1]
    merged = None
    for n, (y_ref, p_ref, w_ref) in enumerate(
            ((yml_ref, p0_ref, wml_ref), (yna_ref, p1_ref, wna_ref), (yxa_ref, p2_ref, wxa_ref))):
        gate = 0.5 + 0.5 * jnp.tanh(0.5 * (p_ref[...].astype(F32) + bg_ref[:, n * D:(n + 1) * D]))
        term = gate * jnp.dot(y_ref[...], w_ref[...], preferred_element_type=F32)
        merged = term if merged is None else merged + term
    o_ref[...] = x_ref[...] + jnp.dot(merged.astype(BF16), wo_ref[...], preferred_element_type=F32)


def _merge(x2, yml, yna, yxa, proj2, b_gate, wml, wna, wxa, wo, gate_blk, tm):
    T, D = x2.shape
    row = lambda w: pl.BlockSpec((tm, w), lambda i: (i, 0))
    full = lambda a: pl.BlockSpec(a.shape, lambda i: (0, 0))
    return pl.pallas_call(
        _merge_kernel,
        grid=(T // tm,),
        in_specs=[
            row(D), row(ML_WIDTH), row(NA_WIDTH), row(XA_WIDTH),
            pl.BlockSpec((tm, D), lambda i: (i, gate_blk)),
            pl.BlockSpec((tm, D), lambda i: (i, gate_blk + 1)),
            pl.BlockSpec((tm, D), lambda i: (i, gate_blk + 2)),
            full(b_gate), full(wml), full(wna), full(wxa), full(wo),
        ],
        out_specs=row(D),
        out_shape=jax.ShapeDtypeStruct((T, D), F32),
        compiler_params=_params(("parallel",), VMEM_LIMIT),
        name="merge",
    )(x2, yml, yna, yxa, proj2, proj2, proj2, b_gate, wml, wna, wxa, wo)


def _gelu_tanh(x):
    return 0.5 * x * (1.0 + jnp.tanh(0.7978845608028654 * (x + 0.044715 * (x * x * x))))


def _ffn_kernel(x_ref, g_ref, wa0_ref, wa1_ref, wu0_ref, wu1_ref, wc0_ref, wc1_ref, bc0_ref, bc1_ref,
                wd0_ref, wd1_ref, gf_ref, o_ref, h_scr, a0_scr, u0_scr, a1_scr, u1_scr):
    t = pl.program_id(1)
    last = pl.num_programs(1) - 1
    S = x_ref.shape[1]

    def produce(wa_ref, wu_ref, a_scr, u_scr):
        h = h_scr[...]
        a_scr[...] = jnp.dot(h, wa_ref[...], preferred_element_type=F32)
        u_scr[...] = jnp.dot(h, wu_ref[...], preferred_element_type=F32)

    def consume(a_scr, u_scr, wc_ref, bc_ref, wd_ref):
        a = a_scr[...]
        row = lax.broadcasted_iota(jnp.int32, a.shape, 0)
        a_prev = jnp.where(row == 0, 0.0, pltpu.roll(a, 1, 0))
        a_next = jnp.where(row == S - 1, 0.0, pltpu.roll(a, S - 1, 0))
        w = wc_ref[...]
        conv = a_prev * w[0:1] + a * w[1:2] + a_next * w[2:3] + bc_ref[...]
        act = (_gelu_tanh(conv) * u_scr[...]).astype(BF16)
        o_ref[0] += jnp.dot(act, wd_ref[...], preferred_element_type=F32)

    @pl.when(t == 0)
    def _():
        x = x_ref[0]
        h_scr[...] = _rms(x, g_ref[...]).astype(BF16)
        o_ref[0] = x
        produce(wa0_ref, wu0_ref, a0_scr, u0_scr)
        produce(wa1_ref, wu1_ref, a1_scr, u1_scr)
        consume(a0_scr, u0_scr, wc1_ref, bc1_ref, wd1_ref)

    @pl.when((t > 0) & (t < last))
    def _():
        produce(wa0_ref, wu0_ref, a0_scr, u0_scr)
        consume(a1_scr, u1_scr, wc0_ref, bc0_ref, wd0_ref)
        produce(wa1_ref, wu1_ref, a1_scr, u1_scr)
        consume(a0_scr, u0_scr, wc1_ref, bc1_ref, wd1_ref)

    @pl.when(t == last)
    def _():
        produce(wa0_ref, wu0_ref, a0_scr, u0_scr)
        consume(a1_scr, u1_scr, wc0_ref, bc0_ref, wd0_ref)
        consume(a0_scr, u0_scr, wc1_ref, bc1_ref, wd1_ref)
        o_ref[0] = _rms(o_ref[0], gf_ref[...])


def _ffn(x3, g, w_up, w_conv, b_conv, w_down, g_final, tf):
    B, S, D = x3.shape
    FF = w_down.shape[0]
    nf = FF // tf
    assert nf % 2 == 1, "the two-chunk pipeline ends on a produce-one / consume-two step"
    steps = (nf + 1) // 2
    prod0 = lambda t: 2 * t
    prod1 = lambda t: jnp.minimum(2 * t + 1, nf - 1)
    cons0 = lambda t: jnp.maximum(2 * t - 1, 0)
    cons1 = lambda t: 2 * t
    col = lambda f, off=0: pl.BlockSpec((D, tf), lambda b, t: (0, off + f(t)))
    vec = lambda rows, f: pl.BlockSpec((rows, tf), lambda b, t: (0, f(t)))
    dn = lambda f: pl.BlockSpec((tf, D), lambda b, t: (f(t), 0))
    buf = pltpu.VMEM((S, tf), F32)
    return pl.pallas_call(
        _ffn_kernel,
        grid=(B, steps),
        in_specs=[
            pl.BlockSpec((1, S, D), lambda b, t: (b, 0, 0)),
            pl.BlockSpec((1, D), lambda b, t: (0, 0)),
            col(prod0), col(prod1), col(prod0, nf), col(prod1, nf),
            vec(3, cons0), vec(3, cons1), vec(1, cons0), vec(1, cons1),
            dn(cons0), dn(cons1),
            pl.BlockSpec((1, D), lambda b, t: (0, 0)),
        ],
        out_specs=pl.BlockSpec((1, S, D), lambda b, t: (b, 0, 0)),
        out_shape=jax.ShapeDtypeStruct((B, S, D), F32),
        scratch_shapes=[pltpu.VMEM((S, D), BF16), buf, buf, buf, buf],
        compiler_params=_params(("parallel", "arbitrary"), VMEM_LIMIT),
        name="ffn",
    )(x3, g, w_up, w_up, w_up, w_up, w_conv, w_conv, b_conv, b_conv, w_down, w_down, g_final)


def kernel(x, mem, mix_norm_g, w_in, b_ml_igate, b_ml_fgate, w_ml_conv, b_ml_conv, ml_norm_g, na_rpb, mem_norm_g, w_mem_kv, b_merge_gate, w_br_ml, w_br_na, w_br_xa, w_out, ffn_norm_g, w_ffn_up, w_ffn_conv, b_ffn_conv, w_ffn_down, final_norm_g):
    B, S, D = x.shape
    H = ML_HEADS
    M = mem.shape[1]
    T = B * S
    assert w_in.shape[0] == 1, "single-layer block: the FFN kernel also applies the final norm"
    assert S % ML_CHUNK == 0 and S % (GRID_W * NA_ROWS_PER_STEP) == 0
    l = 0
    n_gate = 4 * H
    g0 = 4 * ML_WIDTH
    row2 = lambda v: v.reshape(1, -1).astype(F32)

    w_main, w_gate = _wprep(jnp.transpose(w_in[l]), g0, n_gate, tr=1024)
    w_gate = jnp.transpose(w_gate.reshape(4, H, D), (1, 0, 2)).reshape(n_gate, D).astype(BF16)

    proj2, grow = _inproj(x.reshape(T, D), row2(mix_norm_g[l]), w_main, w_gate, tm=1024, tn=3584)
    proj3 = proj2.reshape(B, S, -1)

    bias = jnp.stack([b_ml_igate[l][0], b_ml_fgate[l][0], b_ml_igate[l][1], b_ml_fgate[l][1]],
                     axis=-1).astype(F32)
    y_ml = _mlstm(proj3, grow.reshape(H, 4, T // ML_CHUNK, ML_CHUNK), bias[:, :, None],
                  w_ml_conv[l].astype(F32), row2(b_ml_conv[l]), row2(ml_norm_g[l]))

    nb = (4 * ML_WIDTH) // NA_WIDTH
    y_na = _na(proj3, _na_bias_patterns(na_rpb[l]), nb, nb + 1, nb + 2)

    kv = _memkv(mem.reshape(B * M, D), row2(mem_norm_g[l]), w_mem_kv[l].astype(BF16), tm=512)
    xb = (4 * ML_WIDTH + 3 * NA_WIDTH) // XA_WIDTH
    y_xa = _xa(proj3, kv.reshape(B, M, -1), xb, tq=1024)

    gate_blk = (4 * ML_WIDTH + 3 * NA_WIDTH + XA_WIDTH) // D
    x1 = _merge(x.reshape(T, D), y_ml.reshape(T, -1), y_na.reshape(T, -1), y_xa.reshape(T, -1),
                proj2, row2(b_merge_gate[l]), w_br_ml[l].astype(BF16), w_br_na[l].astype(BF16),
                w_br_xa[l].astype(BF16), w_out[l].astype(BF16), gate_blk, tm=512)

    return _ffn(x1.reshape(B, S, D), row2(ffn_norm_g[l]), w_ffn_up[l].astype(BF16),
                w_ffn_conv[l].astype(F32), row2(b_ffn_conv[l]), w_ffn_down[l].astype(BF16),
                row2(final_norm_g), tf=256)
```

```python
import jax
import jax.numpy as jnp
from jax import lax
from jax.experimental import pallas as pl
from jax.experimental.pallas import tpu as pltpu

F32 = jnp.float32
BF16 = jnp.bfloat16
EPS = 1e-6
NEG_INIT = -1e30
LOG2E = 1.4426950408889634

GRID_W = 64
ML_HEADS = 4
ML_HEAD_DIM = 128
ML_WIDTH = ML_HEADS * ML_HEAD_DIM
NA_HEADS = 8
NA_HEAD_DIM = 64
NA_WIDTH = NA_HEADS * NA_HEAD_DIM
NA_WIN_ROWS = 8
NA_WIN_COLS = 16
XA_HEADS = 4
XA_HEAD_DIM = 128
XA_WIDTH = XA_HEADS * XA_HEAD_DIM

LANES = 128
F32_ROWS = 8
BF16_ROWS = 16
ML_CHUNK = 256
NA_ROWS_PER_STEP = 16
VMEM_LIMIT = 60 * 1024 * 1024


def _params(dims, vmem=None):
    return pltpu.CompilerParams(dimension_semantics=dims, vmem_limit_bytes=vmem)


def _rms(x, g):
    return x * lax.rsqrt(jnp.mean(x * x, axis=-1, keepdims=True) + EPS) * g


def _wprep_kernel(w_ref, wg_ref, o_ref, og_ref):
    o_ref[...] = w_ref[...].astype(BF16)
    og_ref[...] = wg_ref[...]


def _wprep(w_t, g0, n_gate, tr):
    N, D = w_t.shape
    assert g0 % tr == 0 and g0 % n_gate == 0 and (N - n_gate) % tr == 0
    return pl.pallas_call(
        _wprep_kernel,
        grid=((N - n_gate) // tr,),
        in_specs=[
            pl.BlockSpec((pl.Element(tr), pl.Element(D)),
                         lambda i: (pl.multiple_of(i * tr + jnp.where(i * tr >= g0, n_gate, 0), n_gate), 0)),
            pl.BlockSpec((n_gate, D), lambda i: (g0 // n_gate, 0)),
        ],
        out_specs=[pl.BlockSpec((tr, D), lambda i: (i, 0)),
                   pl.BlockSpec((n_gate, D), lambda i: (0, 0))],
        out_shape=[jax.ShapeDtypeStruct((N - n_gate, D), BF16),
                   jax.ShapeDtypeStruct((n_gate, D), F32)],
        compiler_params=_params(("arbitrary",), VMEM_LIMIT),
        name="wprep",
    )(w_t, w_t)


def _inproj_kernel(x_ref, g_ref, w_ref, wg_ref, o_ref, gates_ref, h_scr):
    @pl.when(pl.program_id(1) == 0)
    def _():
        hb = _rms(x_ref[...], g_ref[...]).astype(BF16)
        h_scr[...] = hb
        gt = lax.dot_general(wg_ref[...], hb, (((1,), (1,)), ((), ())), preferred_element_type=F32)
        for h in range(ML_HEADS):
            gates_ref[h] = gt[4 * h:4 * h + 4, :]

    o_ref[...] = lax.dot_general(h_scr[...], w_ref[...], (((1,), (1,)), ((), ())),
                                 preferred_element_type=F32).astype(BF16)


def _inproj(x2, g, w_main, w_gate, tm, tn):
    T, D = x2.shape
    N = w_main.shape[0]
    return pl.pallas_call(
        _inproj_kernel,
        grid=(T // tm, N // tn),
        in_specs=[
            pl.BlockSpec((tm, D), lambda i, j: (i, 0)),
            pl.BlockSpec((1, D), lambda i, j: (0, 0)),
            pl.BlockSpec((tn, D), lambda i, j: (j, 0)),
            pl.BlockSpec(w_gate.shape, lambda i, j: (0, 0)),
        ],
        out_specs=[
            pl.BlockSpec((tm, tn), lambda i, j: (i, j)),
            pl.BlockSpec((ML_HEADS, 4, tm), lambda i, j: (0, 0, i)),
        ],
        out_shape=[
            jax.ShapeDtypeStruct((T, N), BF16),
            jax.ShapeDtypeStruct((ML_HEADS, 4, T), F32),
        ],
        scratch_shapes=[pltpu.VMEM((tm, D), BF16)],
        compiler_params=_params(("parallel", "arbitrary"), VMEM_LIMIT),
        name="inproj",
    )(x2, g, w_main, w_gate)


def _log_sigmoid(x):
    return jnp.minimum(x, 0.0) - jnp.log(1.0 + jnp.exp(-jnp.abs(x)))


def _chunk_scan(x, pos, L, reverse, op, identity):
    n = x.shape[1]
    d = 1
    while d < L:
        if reverse:
            x = op(x, jnp.where(pos < L - d, pltpu.roll(x, n - d, 1), identity))
        else:
            x = op(x, jnp.where(pos >= d, pltpu.roll(x, d, 1), identity))
        d *= 2
    return x


def _mlstm_kernel(q_ref, k_ref, v_ref, o_ref, gr_ref, br_ref,
                  wq_ref, wk_ref, bq_ref, bk_ref, ng_ref, y_ref,
                  qt_scr, k_scr, vat_scr, bb_scr, hf_scr, hb_scr):
    S = q_ref.shape[1]
    L = ML_CHUNK
    NC = S // L
    Dh = ML_HEAD_DIM

    pos = lax.broadcasted_iota(jnp.int32, (NC, L), 1)
    gates = []
    for d, rev in ((0, False), (1, True)):
        i_pre = gr_ref[0, 2 * d] + br_ref[0, 2 * d:2 * d + 1, :]
        lf = _log_sigmoid(gr_ref[0, 2 * d + 1] + br_ref[0, 2 * d + 1:2 * d + 2, :])
        a = _chunk_scan(lf, pos, L, rev, jnp.add, 0.0)
        b = i_pre - a
        gates.append((b, a, _chunk_scan(b, pos, L, rev, jnp.maximum, -jnp.inf), lf))

    def conv_silu(x_ref, w_ref, b_ref, c):
        x = x_ref[0, c * L:(c + 1) * L, :].astype(F32)
        zero_row = jnp.zeros((1, Dh), F32)
        prev_row = x_ref[0, c * L - BF16_ROWS:c * L, :].astype(F32)[BF16_ROWS - 1:, :] if c > 0 else zero_row
        next_row = x_ref[0, (c + 1) * L:(c + 1) * L + BF16_ROWS, :].astype(F32)[0:1, :] if c < NC - 1 else zero_row
        tile = F32_ROWS
        row = lax.broadcasted_iota(jnp.int32, (tile, Dh), 0)
        x_prev = pltpu.roll(x, 1, 0)
        x_prev = jnp.concatenate([jnp.where(row == 0, prev_row, x_prev[:tile]), x_prev[tile:]], axis=0)
        x_next = pltpu.roll(x, L - 1, 0)
        x_next = jnp.concatenate([x_next[:L - tile], jnp.where(row == tile - 1, next_row, x_next[L - tile:])], axis=0)
        w = w_ref[...]
        y = x_prev * w[0:1] + x * w[1:2] + x_next * w[2:3] + b_ref[...]
        return y / (1.0 + jnp.exp2(y * (-LOG2E)))

    vat_scr[Dh:, :] = jnp.ones((Dh, S), BF16)

    def prep(c):
        cs = slice(c * L, (c + 1) * L)
        qt_scr[:, cs] = (conv_silu(q_ref, wq_ref, bq_ref, c) * (Dh ** -0.5)).astype(BF16).T
        k_scr[cs, :] = conv_silu(k_ref, wk_ref, bk_ref, c).astype(BF16)
        vat_scr[:Dh, cs] = v_ref[0, cs, :].T
        for d in range(2):
            bb_scr[d, cs, :] = jnp.broadcast_to(gates[d][0][c:c + 1, :] * LOG2E, (LANES, L)).T

    s_idx = lax.broadcasted_iota(jnp.int32, (L, L), 0)
    j_idx = lax.broadcasted_iota(jnp.int32, (L, L), 1)

    def chunk(c, m, st_t, h_scr, fwd):
        cs = slice(c * L, (c + 1) * L)
        d = 0 if fwd else 1
        b_row, a_row, bmax_row, lf_row = (t[c:c + 1, :] for t in gates[d])
        mask_t = (s_idx <= j_idx) if fwd else (s_idx >= j_idx)
        g = jnp.sum(lf_row, axis=1, keepdims=True)
        w_end = g + b_row
        m_new = jnp.maximum(g + m, jnp.max(w_end, axis=1, keepdims=True))
        decay = jnp.exp(g + m - m_new)
        wt = jnp.exp(w_end - m_new)
        c_row = jnp.maximum(m, bmax_row)
        inter_w = jnp.exp(m - c_row)
        floor = jnp.exp(-(a_row + c_row))
        k = k_scr[cs, :]
        q_t = qt_scr[:, cs]
        va_t = vat_scr[:, cs]
        qk_t = jnp.dot(k, q_t, preferred_element_type=F32)
        b_col = bb_scr[d, cs, :]
        expo = jnp.concatenate([b_col] * (L // LANES), axis=1) - c_row * LOG2E
        p_t = jnp.exp2(jnp.where(mask_t, expo, -jnp.inf)) * qk_t
        q_in = q_t * inter_w.astype(BF16)
        nd_t = (jnp.dot(va_t, p_t.astype(BF16), preferred_element_type=F32)
                + jnp.dot(st_t.astype(BF16), q_in, preferred_element_type=F32))
        h_scr[:, cs] = nd_t[:Dh] / jnp.maximum(jnp.abs(nd_t[Dh:]), floor)
        wt_b = wt.astype(BF16)
        vw = jnp.concatenate([va_t[:Dh] * wt_b, jnp.broadcast_to(wt_b, (Dh, L))], axis=0)
        return m_new, decay * st_t + jnp.dot(vw, k, preferred_element_type=F32)

    def finish(c):
        cs = slice(c * L, (c + 1) * L)
        h_t = hf_scr[:, cs] + hb_scr[:, cs]
        h_t = h_t * lax.rsqrt(jnp.mean(h_t * h_t, axis=0, keepdims=True) + EPS)
        y = h_t.T * ng_ref[...] * jax.nn.sigmoid(o_ref[0, cs, :].astype(F32))
        y_ref[0, cs, :] = y.astype(BF16)

    m_f = m_b = jnp.full((1, 1), NEG_INIT, F32)
    st_f = st_b = jnp.zeros((2 * Dh, Dh), F32)
    prep(0)
    prep(NC - 1)
    for c in range(NC):
        if c + 1 < NC - 1 - (c + 1):
            prep(c + 1)
            prep(NC - 2 - c)
        elif c + 1 == NC - 1 - (c + 1):
            prep(c + 1)
        m_f, st_f = chunk(c, m_f, st_f, hf_scr, True)
        m_b, st_b = chunk(NC - 1 - c, m_b, st_b, hb_scr, False)
        if c >= NC - 1 - c:
            finish(c)
            if c != NC - 1 - c:
                finish(NC - 1 - c)


def _mlstm(proj3, grow, brow, w_conv, b_conv, norm_g):
    B, S, _ = proj3.shape
    H, Dh = ML_HEADS, ML_HEAD_DIM
    blk = lambda off: pl.BlockSpec((1, S, Dh), lambda b, h: (b, 0, off + h))
    return pl.pallas_call(
        _mlstm_kernel,
        grid=(B, H),
        in_specs=[
            blk(0), blk(H), blk(2 * H), blk(3 * H),
            pl.BlockSpec((1, 4, S // ML_CHUNK, ML_CHUNK), lambda b, h: (h, 0, b, 0)),
            pl.BlockSpec((1, 4, 1), lambda b, h: (h, 0, 0)),
            pl.BlockSpec((3, Dh), lambda b, h: (0, h)),
            pl.BlockSpec((3, Dh), lambda b, h: (0, H + h)),
            pl.BlockSpec((1, Dh), lambda b, h: (0, h)),
            pl.BlockSpec((1, Dh), lambda b, h: (0, H + h)),
            pl.BlockSpec((1, Dh), lambda b, h: (0, h)),
        ],
        out_specs=pl.BlockSpec((1, S, Dh), lambda b, h: (b, 0, h)),
        out_shape=jax.ShapeDtypeStruct((B, S, ML_WIDTH), BF16),
        scratch_shapes=[
            pltpu.VMEM((Dh, S), BF16), pltpu.VMEM((S, Dh), BF16), pltpu.VMEM((2 * Dh, S), BF16),
            pltpu.VMEM((2, S, LANES), F32),
            pltpu.VMEM((Dh, S), F32), pltpu.VMEM((Dh, S), F32),
        ],
        compiler_params=_params(("parallel", "parallel"), VMEM_LIMIT),
        name="mlstm",
    )(proj3, proj3, proj3, proj3, grow, brow, w_conv, w_conv, b_conv, b_conv, norm_g)


def _na_kernel(q_ref, k_ref, v_ref, pat_ref, y_ref, t_scr):
    S = k_ref.shape[1]
    rows = S // GRID_W
    W = GRID_W
    n_pairs = NA_HEADS // 2
    n_dr = t_scr.shape[0]
    r0 = pl.program_id(1) * NA_ROWS_PER_STEP
    low = lax.broadcasted_iota(jnp.int32, (W, LANES), 1) < NA_HEAD_DIM

    @pl.when((pl.program_id(0) == 0) & (pl.program_id(1) == 0))
    def _():
        cq = lax.broadcasted_iota(jnp.int32, (W, LANES), 0)
        ck = lax.broadcasted_iota(jnp.int32, (W, LANES), 1) % W
        col_start = jnp.clip(cq - NA_WIN_COLS // 2, 0, W - NA_WIN_COLS)
        col_ok = (ck >= col_start) & (ck < col_start + NA_WIN_COLS)
        for d in range(n_dr):
            for h in range(NA_HEADS):
                pattern = jnp.broadcast_to(pat_ref[h, d:d + 1, :], (W, LANES))
                skewed = pltpu.roll(pattern, 0, 1, stride=1, stride_axis=0)
                t_scr[d, h // 2, (h % 2) * W:(h % 2 + 1) * W, :] = jnp.where(col_ok, skewed, -jnp.inf)

    ones = jnp.ones((NA_WIN_ROWS * W, LANES), BF16)

    def row_body(i, _):
        r = r0 + i
        row_start = jnp.clip(r - NA_WIN_ROWS // 2, 0, rows - NA_WIN_ROWS)
        dr0 = row_start - r + (NA_WIN_ROWS - 1)
        band = pl.ds(pl.multiple_of(row_start * W, W), NA_WIN_ROWS * W)
        qrow = pl.ds(pl.multiple_of(i * W, W), W)
        scores = []
        for hp in range(n_pairs):
            cols = slice(hp * LANES, (hp + 1) * LANES)
            q2 = q_ref[0, qrow, cols] * (NA_HEAD_DIM ** -0.5)
            zero = jnp.zeros_like(q2)
            qs = jnp.concatenate([jnp.where(low, q2, zero), jnp.where(low, zero, q2)], axis=0)
            s = lax.dot_general(qs, k_ref[0, band, cols], (((1,), (1,)), ((), ())),
                                preferred_element_type=F32)
            bias = jnp.concatenate([t_scr[dr0 + kk, hp] for kk in range(0, NA_WIN_ROWS, 2)], axis=-1)
            scores.append(s + bias)
        probs = [jnp.exp(s - jnp.max(s, axis=-1, keepdims=True)).astype(BF16) for s in scores]
        for hp in range(n_pairs):
            cols = slice(hp * LANES, (hp + 1) * LANES)
            v_aug = jnp.concatenate([v_ref[0, band, cols], ones], axis=1)
            o = jnp.dot(probs[hp], v_aug, preferred_element_type=F32)
            o = o[:, :LANES] / o[:, LANES:]
            y_ref[0, qrow, cols] = jnp.where(low, o[:W], o[W:]).astype(BF16)
        return 0

    lax.fori_loop(0, NA_ROWS_PER_STEP, row_body, 0, unroll=4)


def _na(proj3, patterns, q_blk, k_blk, v_blk):
    B, S, _ = proj3.shape
    rows = S // GRID_W
    rb = NA_ROWS_PER_STEP
    n_dr = patterns.shape[1]
    return pl.pallas_call(
        _na_kernel,
        grid=(B, rows // rb),
        in_specs=[
            pl.BlockSpec((1, rb * GRID_W, NA_WIDTH), lambda b, r: (b, r, q_blk)),
            pl.BlockSpec((1, S, NA_WIDTH), lambda b, r: (b, 0, k_blk)),
            pl.BlockSpec((1, S, NA_WIDTH), lambda b, r: (b, 0, v_blk)),
            pl.BlockSpec(patterns.shape, lambda b, r: (0, 0, 0)),
        ],
        out_specs=pl.BlockSpec((1, rb * GRID_W, NA_WIDTH), lambda b, r: (b, r, 0)),
        out_shape=jax.ShapeDtypeStruct((B, S, NA_WIDTH), BF16),
        scratch_shapes=[pltpu.VMEM((n_dr, NA_HEADS // 2, 2 * GRID_W, LANES), F32)],
        compiler_params=_params(("arbitrary", "arbitrary"), VMEM_LIMIT),
        name="natten",
    )(proj3, proj3, proj3, patterns)


def _na_bias_patterns(rpb):
    c0 = NA_WIN_COLS - 1
    r = rpb.astype(F32)
    gap = jnp.zeros(r.shape[:1] + (r.shape[1] - 1, LANES // 2 - NA_WIN_COLS - c0), F32)
    return jnp.concatenate([r[:, :-1, c0:], gap, r[:, 1:, :], gap, r[:, :-1, :c0]], axis=-1)


def _memkv_kernel(m_ref, g_ref, w_ref, o_ref):
    hb = _rms(m_ref[...], g_ref[...]).astype(BF16)
    o_ref[...] = jnp.dot(hb, w_ref[...], preferred_element_type=F32).astype(BF16)


def _memkv(mem2, g, w, tm):
    T, D = mem2.shape
    N = w.shape[1]
    return pl.pallas_call(
        _memkv_kernel,
        grid=(T // tm,),
        in_specs=[
            pl.BlockSpec((tm, D), lambda i: (i, 0)),
            pl.BlockSpec((1, D), lambda i: (0, 0)),
            pl.BlockSpec((D, N), lambda i: (0, 0)),
        ],
        out_specs=pl.BlockSpec((tm, N), lambda i: (i, 0)),
        out_shape=jax.ShapeDtypeStruct((T, N), BF16),
        compiler_params=_params(("parallel",), VMEM_LIMIT),
        name="memkv",
    )(mem2, g, w)


def _xa_kernel(q_ref, kv_ref, y_ref):
    Dh = XA_HEAD_DIM
    for h in range(XA_HEADS):
        q = q_ref[0, :, h * Dh:(h + 1) * Dh]
        k = kv_ref[0, :, h * Dh:(h + 1) * Dh]
        v = kv_ref[0, :, XA_WIDTH + h * Dh:XA_WIDTH + (h + 1) * Dh]
        s = lax.dot_general(q, k, (((1,), (1,)), ((), ())), preferred_element_type=F32) * (Dh ** -0.5)
        p = jnp.exp(s - jnp.max(s, axis=-1, keepdims=True))
        o = jnp.dot(p.astype(BF16), v, preferred_element_type=F32)
        y_ref[0, :, h * Dh:(h + 1) * Dh] = (o / jnp.sum(p, axis=-1, keepdims=True)).astype(BF16)


def _xa(proj3, kv3, q_blk, tq):
    B, S, _ = proj3.shape
    M = kv3.shape[1]
    return pl.pallas_call(
        _xa_kernel,
        grid=(B, S // tq),
        in_specs=[
            pl.BlockSpec((1, tq, XA_WIDTH), lambda b, i: (b, i, q_blk)),
            pl.BlockSpec((1, M, 2 * XA_WIDTH), lambda b, i: (b, 0, 0)),
        ],
        out_specs=pl.BlockSpec((1, tq, XA_WIDTH), lambda b, i: (b, i, 0)),
        out_shape=jax.ShapeDtypeStruct((B, S, XA_WIDTH), BF16),
        compiler_params=_params(("parallel", "parallel"), VMEM_LIMIT),
        name="memxattn",
    )(proj3, kv3)


def _merge_kernel(x_ref, yml_ref, yna_ref, yxa_ref, p0_ref, p1_ref, p2_ref, bg_ref,
                  wml_ref, wna_ref, wxa_ref, wo_ref, o_ref):
    D = x_ref.shape[1]
    merged = None
    for n, (y_ref, p_ref, w_ref) in enumerate(
            ((yml_ref, p0_ref, wml_ref), (yna_ref, p1_ref, wna_ref), (yxa_ref, p2_ref, wxa_ref))):
        gate = 0.5 + 0.5 * jnp.tanh(0.5 * (p_ref[...].astype(F32) + bg_ref[:, n * D:(n + 1) * D]))
        term = gate * jnp.dot(y_ref[...], w_ref[...], preferred_element_type=F32)
        merged = term if merged is None else merged + term
    o_ref[...] = x_ref[...] + jnp.dot(merged.astype(BF16), wo_ref[...], preferred_element_type=F32)


def _merge(x2, yml, yna, yxa, proj2, b_gate, wml, wna, wxa, wo, gate_blk, tm):
    T, D = x2.shape
    row = lambda w: pl.BlockSpec((tm, w), lambda i: (i, 0))
    full = lambda a: pl.BlockSpec(a.shape, lambda i: (0, 0))
    return pl.pallas_call(
        _merge_kernel,
        grid=(T // tm,),
        in_specs=[
            row(D), row(ML_WIDTH), row(NA_WIDTH), row(XA_WIDTH),
            pl.BlockSpec((tm, D), lambda i: (i, gate_blk)),
            pl.BlockSpec((tm, D), lambda i: (i, gate_blk + 1)),
            pl.BlockSpec((tm, D), lambda i: (i, gate_blk + 2)),
            full(b_gate), full(wml), full(wna), full(wxa), full(wo),
        ],
        out_specs=row(D),
        out_shape=jax.ShapeDtypeStruct((T, D), F32),
        compiler_params=_params(("parallel",), VMEM_LIMIT),
        name="merge",
    )(x2, yml, yna, yxa, proj2, proj2, proj2, b_gate, wml, wna, wxa, wo)


def _gelu_tanh(x):
    return 0.5 * x * (1.0 + jnp.tanh(0.7978845608028654 * (x + 0.044715 * (x * x * x))))


def _ffn_kernel(x_ref, g_ref, wa0_ref, wa1_ref, wu0_ref, wu1_ref, wc0_ref, wc1_ref, bc0_ref, bc1_ref,
                wd0_ref, wd1_ref, gf_ref, o_ref, h_scr, a0_scr, u0_scr, a1_scr, u1_scr):
    t = pl.program_id(1)
    last = pl.num_programs(1) - 1
    S = x_ref.shape[1]

    def produce(wa_ref, wu_ref, a_scr, u_scr):
        h = h_scr[...]
        a_scr[...] = jnp.dot(h, wa_ref[...], preferred_element_type=F32)
        u_scr[...] = jnp.dot(h, wu_ref[...], preferred_element_type=F32)

    def consume(a_scr, u_scr, wc_ref, bc_ref, wd_ref):
        a = a_scr[...]
        row = lax.broadcasted_iota(jnp.int32, a.shape, 0)
        a_prev = jnp.where(row == 0, 0.0, pltpu.roll(a, 1, 0))
        a_next = jnp.where(row == S - 1, 0.0, pltpu.roll(a, S - 1, 0))
        w = wc_ref[...]
        conv = a_prev * w[0:1] + a * w[1:2] + a_next * w[2:3] + bc_ref[...]
        act = (_gelu_tanh(conv) * u_scr[...]).astype(BF16)
        o_ref[0] += jnp.dot(act, wd_ref[...], preferred_element_type=F32)

    @pl.when(t == 0)
    def _():
        x = x_ref[0]
        h_scr[...] = _rms(x, g_ref[...]).astype(BF16)
        o_ref[0] = x
        produce(wa0_ref, wu0_ref, a0_scr, u0_scr)
        produce(wa1_ref, wu1_ref, a1_scr, u1_scr)
        consume(a0_scr, u0_scr, wc1_ref, bc1_ref, wd1_ref)

    @pl.when((t > 0) & (t < last))
    def _():
        produce(wa0_ref, wu0_ref, a0_scr, u0_scr)
        consume(a1_scr, u1_scr, wc0_ref, bc0_ref, wd0_ref)
        produce(wa1_ref, wu1_ref, a1_scr, u1_scr)
        consume(a0_scr, u0_scr, wc1_ref, bc1_ref, wd1_ref)

    @pl.when(t == last)
    def _():
        produce(wa0_ref, wu0_ref, a0_scr, u0_scr)
        consume(a1_scr, u1_scr, wc0_ref, bc0_ref, wd0_ref)
        consume(a0_scr, u0_scr, wc1_ref, bc1_ref, wd1_ref)
        o_ref[0] = _rms(o_ref[0], gf_ref[...])


def _ffn(x3, g, w_up, w_conv, b_conv, w_down, g_final, tf):
    B, S, D = x3.shape
    FF = w_down.shape[0]
    nf = FF // tf
    assert nf % 2 == 1, "the two-chunk pipeline ends on a produce-one / consume-two step"
    steps = (nf + 1) // 2
    prod0 = lambda t: 2 * t
    prod1 = lambda t: jnp.minimum(2 * t + 1, nf - 1)
    cons0 = lambda t: jnp.maximum(2 * t - 1, 0)
    cons1 = lambda t: 2 * t
    col = lambda f, off=0: pl.BlockSpec((D, tf), lambda b, t: (0, off + f(t)))
    vec = lambda rows, f: pl.BlockSpec((rows, tf), lambda b, t: (0, f(t)))
    dn = lambda f: pl.BlockSpec((tf, D), lambda b, t: (f(t), 0))
    buf = pltpu.VMEM((S, tf), F32)
    return pl.pallas_call(
        _ffn_kernel,
        grid=(B, steps),
        in_specs=[
            pl.BlockSpec((1, S, D), lambda b, t: (b, 0, 0)),
            pl.BlockSpec((1, D), lambda b, t: (0, 0)),
            col(prod0), col(prod1), col(prod0, nf), col(prod1, nf),
            vec(3, cons0), vec(3, cons1), vec(1, cons0), vec(1, cons1),
            dn(cons0), dn(cons1),
            pl.BlockSpec((1, D), lambda b, t: (0, 0)),
        ],
        out_specs=pl.BlockSpec((1, S, D), lambda b, t: (b, 0, 0)),
        out_shape=jax.ShapeDtypeStruct((B, S, D), F32),
        scratch_shapes=[pltpu.VMEM((S, D), BF16), buf, buf, buf, buf],
        compiler_params=_params(("parallel", "arbitrary"), VMEM_LIMIT),
        name="ffn",
    )(x3, g, w_up, w_up, w_up, w_up, w_conv, w_conv, b_conv, b_conv, w_down, w_down, g_final)


def kernel(x, mem, mix_norm_g, w_in, b_ml_igate, b_ml_fgate, w_ml_conv, b_ml_conv, ml_norm_g, na_rpb, mem_norm_g, w_mem_kv, b_merge_gate, w_br_ml, w_br_na, w_br_xa, w_out, ffn_norm_g, w_ffn_up, w_ffn_conv, b_ffn_conv, w_ffn_down, final_norm_g):
    B, S, D = x.shape
    H = ML_HEADS
    M = mem.shape[1]
    T = B * S
    assert w_in.shape[0] == 1, "single-layer block: the FFN kernel also applies the final norm"
    assert S % ML_CHUNK == 0 and S % (GRID_W * NA_ROWS_PER_STEP) == 0
    l = 0
    n_gate = 4 * H
    g0 = 4 * ML_WIDTH
    row2 = lambda v: v.reshape(1, -1).astype(F32)

    w_main, w_gate = _wprep(jnp.transpose(w_in[l]), g0, n_gate, tr=1024)
    w_gate = jnp.transpose(w_gate.reshape(4, H, D), (1, 0, 2)).reshape(n_gate, D).astype(BF16)

    proj2, grow = _inproj(x.reshape(T, D), row2(mix_norm_g[l]), w_main, w_gate, tm=1024, tn=3584)
    proj3 = proj2.reshape(B, S, -1)

    bias = jnp.stack([b_ml_igate[l][0], b_ml_fgate[l][0], b_ml_igate[l][1], b_ml_fgate[l][1]],
                     axis=-1).astype(F32)
    y_ml = _mlstm(proj3, grow.reshape(H, 4, T // ML_CHUNK, ML_CHUNK), bias[:, :, None],
                  w_ml_conv[l].astype(F32), row2(b_ml_conv[l]), row2(ml_norm_g[l]))

    nb = (4 * ML_WIDTH) // NA_WIDTH
    y_na = _na(proj3, _na_bias_patterns(na_rpb[l]), nb, nb + 1, nb + 2)

    kv = _memkv(mem.reshape(B * M, D), row2(mem_norm_g[l]), w_mem_kv[l].astype(BF16), tm=512)
    xb = (4 * ML_WIDTH + 3 * NA_WIDTH) // XA_WIDTH
    y_xa = _xa(proj3, kv.reshape(B, M, -1), xb, tq=S)

    gate_blk = (4 * ML_WIDTH + 3 * NA_WIDTH + XA_WIDTH) // D
    x1 = _merge(x.reshape(T, D), y_ml.reshape(T, -1), y_na.reshape(T, -1), y_xa.reshape(T, -1),
                proj2, row2(b_merge_gate[l]), w_br_ml[l].astype(BF16), w_br_na[l].astype(BF16),
                w_br_xa[l].astype(BF16), w_out[l].astype(BF16), gate_blk, tm=1024)

    return _ffn(x1.reshape(B, S, D), row2(ffn_norm_g[l]), w_ffn_up[l].astype(BF16),
                w_ffn_conv[l].astype(F32), row2(b_ffn_conv[l]), w_ffn_down[l].astype(BF16),
                row2(final_norm_g), tf=256)
```

```python
import jax
import jax.numpy as jnp
from jax import lax
from jax.experimental import pallas as pl
from jax.experimental.pallas import tpu as pltpu

F32 = jnp.float32
BF16 = jnp.bfloat16
EPS = 1e-6
NEG_INIT = -1e30
LOG2E = 1.4426950408889634

GRID_W = 64
ML_HEADS = 4
ML_HEAD_DIM = 128
ML_WIDTH = ML_HEADS * ML_HEAD_DIM
NA_HEADS = 8
NA_HEAD_DIM = 64
NA_WIDTH = NA_HEADS * NA_HEAD_DIM
NA_WIN_ROWS = 8
NA_WIN_COLS = 16
XA_HEADS = 4
XA_HEAD_DIM = 128
XA_WIDTH = XA_HEADS * XA_HEAD_DIM

LANES = 128
F32_ROWS = 8
BF16_ROWS = 16
ML_CHUNK = 256
NA_ROWS_PER_STEP = 16
VMEM_LIMIT = 60 * 1024 * 1024


def _params(dims, vmem=None):
    return pltpu.CompilerParams(dimension_semantics=dims, vmem_limit_bytes=vmem)


def _rms(x, g):
    return x * lax.rsqrt(jnp.mean(x * x, axis=-1, keepdims=True) + EPS) * g


def _wprep_kernel(w_ref, wg_ref, o_ref, og_ref):
    o_ref[...] = w_ref[...].astype(BF16)
    og_ref[...] = wg_ref[...]


def _wprep(w_t, g0, n_gate, tr):
    N, D = w_t.shape
    assert g0 % tr == 0 and g0 % n_gate == 0 and (N - n_gate) % tr == 0
    return pl.pallas_call(
        _wprep_kernel,
        grid=((N - n_gate) // tr,),
        in_specs=[
            pl.BlockSpec((pl.Element(tr), pl.Element(D)),
                         lambda i: (pl.multiple_of(i * tr + jnp.where(i * tr >= g0, n_gate, 0), n_gate), 0)),
            pl.BlockSpec((n_gate, D), lambda i: (g0 // n_gate, 0)),
        ],
        out_specs=[pl.BlockSpec((tr, D), lambda i: (i, 0)),
                   pl.BlockSpec((n_gate, D), lambda i: (0, 0))],
        out_shape=[jax.ShapeDtypeStruct((N - n_gate, D), BF16),
                   jax.ShapeDtypeStruct((n_gate, D), F32)],
        compiler_params=_params(("arbitrary",), VMEM_LIMIT),
        name="wprep",
    )(w_t, w_t)


def _inproj_kernel(x_ref, g_ref, w_ref, wg_ref, *rest):
    n_cast = (len(rest) - 3) // 2
    cast_in, (o_ref, gates_ref), cast_out, h_scr = (
        rest[:n_cast], rest[n_cast:n_cast + 2], rest[n_cast + 2:2 * n_cast + 2], rest[-1])
    for src, dst in zip(cast_in, cast_out):
        dst[...] = src[...].astype(BF16)

    @pl.when(pl.program_id(1) == 0)
    def _():
        hb = _rms(x_ref[...], g_ref[...]).astype(BF16)
        h_scr[...] = hb
        gt = lax.dot_general(wg_ref[...], hb, (((1,), (1,)), ((), ())), preferred_element_type=F32)
        for h in range(ML_HEADS):
            gates_ref[h] = gt[4 * h:4 * h + 4, :]

    o_ref[...] = lax.dot_general(h_scr[...], w_ref[...], (((1,), (1,)), ((), ())),
                                 preferred_element_type=F32).astype(BF16)


def _inproj(x2, g, w_main, w_gate, to_cast, tm, tn):
    T, D = x2.shape
    N = w_main.shape[0]
    ni, nj = T // tm, N // tn
    steps = ni * nj
    assert all(a.shape[0] % (steps * BF16_ROWS) == 0 for a in to_cast)
    slab = lambda a: pl.BlockSpec((a.shape[0] // steps, a.shape[1]), lambda i, j: (i * nj + j, 0))
    return pl.pallas_call(
        _inproj_kernel,
        grid=(ni, nj),
        in_specs=[
            pl.BlockSpec((tm, D), lambda i, j: (i, 0)),
            pl.BlockSpec((1, D), lambda i, j: (0, 0)),
            pl.BlockSpec((tn, D), lambda i, j: (j, 0)),
            pl.BlockSpec(w_gate.shape, lambda i, j: (0, 0)),
        ] + [slab(a) for a in to_cast],
        out_specs=[
            pl.BlockSpec((tm, tn), lambda i, j: (i, j)),
            pl.BlockSpec((ML_HEADS, 4, tm), lambda i, j: (0, 0, i)),
        ] + [slab(a) for a in to_cast],
        out_shape=[
            jax.ShapeDtypeStruct((T, N), BF16),
            jax.ShapeDtypeStruct((ML_HEADS, 4, T), F32),
        ] + [jax.ShapeDtypeStruct(a.shape, BF16) for a in to_cast],
        scratch_shapes=[pltpu.VMEM((tm, D), BF16)],
        compiler_params=_params(("parallel", "arbitrary"), VMEM_LIMIT),
        name="inproj",
    )(x2, g, w_main, w_gate, *to_cast)


def _log_sigmoid(x):
    return jnp.minimum(x, 0.0) - jnp.log(1.0 + jnp.exp(-jnp.abs(x)))


def _chunk_scan(x, pos, L, reverse, op, identity):
    n = x.shape[1]
    d = 1
    while d < L:
        if reverse:
            x = op(x, jnp.where(pos < L - d, pltpu.roll(x, n - d, 1), identity))
        else:
            x = op(x, jnp.where(pos >= d, pltpu.roll(x, d, 1), identity))
        d *= 2
    return x


def _mlstm_kernel(q_ref, k_ref, v_ref, o_ref, gr_ref, br_ref,
                  wq_ref, wk_ref, bq_ref, bk_ref, ng_ref, y_ref,
                  qt_scr, k_scr, vat_scr, bb_scr, hf_scr, hb_scr):
    S = q_ref.shape[1]
    L = ML_CHUNK
    NC = S // L
    Dh = ML_HEAD_DIM

    pos = lax.broadcasted_iota(jnp.int32, (NC, L), 1)
    gates = []
    for d, rev in ((0, False), (1, True)):
        i_pre = gr_ref[0, 2 * d] + br_ref[0, 2 * d:2 * d + 1, :]
        lf = _log_sigmoid(gr_ref[0, 2 * d + 1] + br_ref[0, 2 * d + 1:2 * d + 2, :])
        a = _chunk_scan(lf, pos, L, rev, jnp.add, 0.0)
        b = i_pre - a
        gates.append((b, a, _chunk_scan(b, pos, L, rev, jnp.maximum, -jnp.inf), lf))

    def conv_silu(x_ref, w_ref, b_ref, c):
        x = x_ref[0, c * L:(c + 1) * L, :].astype(F32)
        zero_row = jnp.zeros((1, Dh), F32)
        prev_row = x_ref[0, c * L - BF16_ROWS:c * L, :].astype(F32)[BF16_ROWS - 1:, :] if c > 0 else zero_row
        next_row = x_ref[0, (c + 1) * L:(c + 1) * L + BF16_ROWS, :].astype(F32)[0:1, :] if c < NC - 1 else zero_row
        tile = F32_ROWS
        row = lax.broadcasted_iota(jnp.int32, (tile, Dh), 0)
        x_prev = pltpu.roll(x, 1, 0)
        x_prev = jnp.concatenate([jnp.where(row == 0, prev_row, x_prev[:tile]), x_prev[tile:]], axis=0)
        x_next = pltpu.roll(x, L - 1, 0)
        x_next = jnp.concatenate([x_next[:L - tile], jnp.where(row == tile - 1, next_row, x_next[L - tile:])], axis=0)
        w = w_ref[...]
        y = x_prev * w[0:1] + x * w[1:2] + x_next * w[2:3] + b_ref[...]
        return y / (1.0 + jnp.exp2(y * (-LOG2E)))

    vat_scr[Dh:, :] = jnp.ones((Dh, S), BF16)

    def prep(c):
        cs = slice(c * L, (c + 1) * L)
        qt_scr[:, cs] = (conv_silu(q_ref, wq_ref, bq_ref, c) * (Dh ** -0.5)).astype(BF16).T
        k_scr[cs, :] = conv_silu(k_ref, wk_ref, bk_ref, c).astype(BF16)
        vat_scr[:Dh, cs] = v_ref[0, cs, :].T
        for d in range(2):
            bb_scr[d, cs, :] = jnp.broadcast_to(gates[d][0][c:c + 1, :] * LOG2E, (LANES, L)).T

    s_idx = lax.broadcasted_iota(jnp.int32, (L, L), 0)
    j_idx = lax.broadcasted_iota(jnp.int32, (L, L), 1)

    def chunk(c, m, st_t, h_scr, fwd):
        cs = slice(c * L, (c + 1) * L)
        d = 0 if fwd else 1
        b_row, a_row, bmax_row, lf_row = (t[c:c + 1, :] for t in gates[d])
        mask_t = (s_idx <= j_idx) if fwd else (s_idx >= j_idx)
        g = jnp.sum(lf_row, axis=1, keepdims=True)
        w_end = g + b_row
        m_new = jnp.maximum(g + m, jnp.max(w_end, axis=1, keepdims=True))
        decay = jnp.exp(g + m - m_new)
        wt = jnp.exp(w_end - m_new)
        c_row = jnp.maximum(m, bmax_row)
        inter_w = jnp.exp(m - c_row)
        floor = jnp.exp(-(a_row + c_row))
        k = k_scr[cs, :]
        q_t = qt_scr[:, cs]
        va_t = vat_scr[:, cs]
        qk_t = jnp.dot(k, q_t, preferred_element_type=F32)
        b_col = bb_scr[d, cs, :]
        expo = jnp.concatenate([b_col] * (L // LANES), axis=1) - c_row * LOG2E
        p_t = jnp.exp2(jnp.where(mask_t, expo, -jnp.inf)) * qk_t
        q_in = q_t * inter_w.astype(BF16)
        nd_t = (jnp.dot(va_t, p_t.astype(BF16), preferred_element_type=F32)
                + jnp.dot(st_t.astype(BF16), q_in, preferred_element_type=F32))
        h_scr[:, cs] = nd_t[:Dh] / jnp.maximum(jnp.abs(nd_t[Dh:]), floor)
        wt_b = wt.astype(BF16)
        vw = jnp.concatenate([va_t[:Dh] * wt_b, jnp.broadcast_to(wt_b, (Dh, L))], axis=0)
        return m_new, decay * st_t + jnp.dot(vw, k, preferred_element_type=F32)

    def finish(c):
        cs = slice(c * L, (c + 1) * L)
        h_t = hf_scr[:, cs] + hb_scr[:, cs]
        h_t = h_t * lax.rsqrt(jnp.mean(h_t * h_t, axis=0, keepdims=True) + EPS)
        y = h_t.T * ng_ref[...] * jax.nn.sigmoid(o_ref[0, cs, :].astype(F32))
        y_ref[0, cs, :] = y.astype(BF16)

    m_f = m_b = jnp.full((1, 1), NEG_INIT, F32)
    st_f = st_b = jnp.zeros((2 * Dh, Dh), F32)
    prep(0)
    prep(NC - 1)
    for c in range(NC):
        if c + 1 < NC - 1 - (c + 1):
            prep(c + 1)
            prep(NC - 2 - c)
        elif c + 1 == NC - 1 - (c + 1):
            prep(c + 1)
        m_f, st_f = chunk(c, m_f, st_f, hf_scr, True)
        m_b, st_b = chunk(NC - 1 - c, m_b, st_b, hb_scr, False)
        if c >= NC - 1 - c:
            finish(c)
            if c != NC - 1 - c:
                finish(NC - 1 - c)


def _mlstm(proj3, grow, brow, w_conv, b_conv, norm_g):
    B, S, _ = proj3.shape
    H, Dh = ML_HEADS, ML_HEAD_DIM
    blk = lambda off: pl.BlockSpec((1, S, Dh), lambda b, h: (b, 0, off + h))
    return pl.pallas_call(
        _mlstm_kernel,
        grid=(B, H),
        in_specs=[
            blk(0), blk(H), blk(2 * H), blk(3 * H),
            pl.BlockSpec((1, 4, S // ML_CHUNK, ML_CHUNK), lambda b, h: (h, 0, b, 0)),
            pl.BlockSpec((1, 4, 1), lambda b, h: (h, 0, 0)),
            pl.BlockSpec((3, Dh), lambda b, h: (0, h)),
            pl.BlockSpec((3, Dh), lambda b, h: (0, H + h)),
            pl.BlockSpec((1, Dh), lambda b, h: (0, h)),
            pl.BlockSpec((1, Dh), lambda b, h: (0, H + h)),
            pl.BlockSpec((1, Dh), lambda b, h: (0, h)),
        ],
        out_specs=pl.BlockSpec((1, S, Dh), lambda b, h: (b, 0, h)),
        out_shape=jax.ShapeDtypeStruct((B, S, ML_WIDTH), BF16),
        scratch_shapes=[
            pltpu.VMEM((Dh, S), BF16), pltpu.VMEM((S, Dh), BF16), pltpu.VMEM((2 * Dh, S), BF16),
            pltpu.VMEM((2, S, LANES), F32),
            pltpu.VMEM((Dh, S), F32), pltpu.VMEM((Dh, S), F32),
        ],
        compiler_params=_params(("parallel", "parallel"), VMEM_LIMIT),
        name="mlstm",
    )(proj3, proj3, proj3, proj3, grow, brow, w_conv, w_conv, b_conv, b_conv, norm_g)


def _na_kernel(q_ref, k_ref, v_ref, pat_ref, y_ref, t_scr):
    S = k_ref.shape[1]
    rows = S // GRID_W
    W = GRID_W
    n_pairs = NA_HEADS // 2
    n_dr = t_scr.shape[0]
    r0 = pl.program_id(1) * NA_ROWS_PER_STEP
    low = lax.broadcasted_iota(jnp.int32, (W, LANES), 1) < NA_HEAD_DIM

    @pl.when((pl.program_id(0) == 0) & (pl.program_id(1) == 0))
    def _():
        cq = lax.broadcasted_iota(jnp.int32, (W, LANES), 0)
        ck = lax.broadcasted_iota(jnp.int32, (W, LANES), 1) % W
        col_start = jnp.clip(cq - NA_WIN_COLS // 2, 0, W - NA_WIN_COLS)
        col_ok = (ck >= col_start) & (ck < col_start + NA_WIN_COLS)
        for d in range(n_dr):
            for h in range(NA_HEADS):
                pattern = jnp.broadcast_to(pat_ref[h, d:d + 1, :], (W, LANES))
                skewed = pltpu.roll(pattern, 0, 1, stride=1, stride_axis=0)
                t_scr[d, h // 2, (h % 2) * W:(h % 2 + 1) * W, :] = jnp.where(col_ok, skewed, -jnp.inf)

    ones = jnp.ones((NA_WIN_ROWS * W, LANES), BF16)

    def row_body(i, _):
        r = r0 + i
        row_start = jnp.clip(r - NA_WIN_ROWS // 2, 0, rows - NA_WIN_ROWS)
        dr0 = row_start - r + (NA_WIN_ROWS - 1)
        band = pl.ds(pl.multiple_of(row_start * W, W), NA_WIN_ROWS * W)
        qrow = pl.ds(pl.multiple_of(i * W, W), W)
        scores = []
        for hp in range(n_pairs):
            cols = slice(hp * LANES, (hp + 1) * LANES)
            q2 = q_ref[0, qrow, cols] * (NA_HEAD_DIM ** -0.5)
            zero = jnp.zeros_like(q2)
            qs = jnp.concatenate([jnp.where(low, q2, zero), jnp.where(low, zero, q2)], axis=0)
            s = lax.dot_general(qs, k_ref[0, band, cols], (((1,), (1,)), ((), ())),
                                preferred_element_type=F32)
            bias = jnp.concatenate([t_scr[dr0 + kk, hp] for kk in range(0, NA_WIN_ROWS, 2)], axis=-1)
            scores.append(s + bias)
        probs = [jnp.exp(s - jnp.max(s, axis=-1, keepdims=True)).astype(BF16) for s in scores]
        for hp in range(n_pairs):
            cols = slice(hp * LANES, (hp + 1) * LANES)
            v_aug = jnp.concatenate([v_ref[0, band, cols], ones], axis=1)
            o = jnp.dot(probs[hp], v_aug, preferred_element_type=F32)
            o = o[:, :LANES] / o[:, LANES:]
            y_ref[0, qrow, cols] = jnp.where(low, o[:W], o[W:]).astype(BF16)
        return 0

    lax.fori_loop(0, NA_ROWS_PER_STEP, row_body, 0, unroll=4)


def _na(proj3, patterns, q_blk, k_blk, v_blk):
    B, S, _ = proj3.shape
    rows = S // GRID_W
    rb = NA_ROWS_PER_STEP
    n_dr = patterns.shape[1]
    return pl.pallas_call(
        _na_kernel,
        grid=(B, rows // rb),
        in_specs=[
            pl.BlockSpec((1, rb * GRID_W, NA_WIDTH), lambda b, r: (b, r, q_blk)),
            pl.BlockSpec((1, S, NA_WIDTH), lambda b, r: (b, 0, k_blk)),
            pl.BlockSpec((1, S, NA_WIDTH), lambda b, r: (b, 0, v_blk)),
            pl.BlockSpec(patterns.shape, lambda b, r: (0, 0, 0)),
        ],
        out_specs=pl.BlockSpec((1, rb * GRID_W, NA_WIDTH), lambda b, r: (b, r, 0)),
        out_shape=jax.ShapeDtypeStruct((B, S, NA_WIDTH), BF16),
        scratch_shapes=[pltpu.VMEM((n_dr, NA_HEADS // 2, 2 * GRID_W, LANES), F32)],
        compiler_params=_params(("arbitrary", "arbitrary"), VMEM_LIMIT),
        name="natten",
    )(proj3, proj3, proj3, patterns)


def _na_bias_patterns(rpb):
    c0 = NA_WIN_COLS - 1
    r = rpb.astype(F32)
    gap = jnp.zeros(r.shape[:1] + (r.shape[1] - 1, LANES // 2 - NA_WIN_COLS - c0), F32)
    return jnp.concatenate([r[:, :-1, c0:], gap, r[:, 1:, :], gap, r[:, :-1, :c0]], axis=-1)


def _memkv_kernel(m_ref, g_ref, w_ref, o_ref):
    hb = _rms(m_ref[...], g_ref[...]).astype(BF16)
    o_ref[...] = jnp.dot(hb, w_ref[...], preferred_element_type=F32).astype(BF16)


def _memkv(mem2, g, w, tm):
    T, D = mem2.shape
    N = w.shape[1]
    return pl.pallas_call(
        _memkv_kernel,
        grid=(T // tm,),
        in_specs=[
            pl.BlockSpec((tm, D), lambda i: (i, 0)),
            pl.BlockSpec((1, D), lambda i: (0, 0)),
            pl.BlockSpec((D, N), lambda i: (0, 0)),
        ],
        out_specs=pl.BlockSpec((tm, N), lambda i: (i, 0)),
        out_shape=jax.ShapeDtypeStruct((T, N), BF16),
        compiler_params=_params(("parallel",), VMEM_LIMIT),
        name="memkv",
    )(mem2, g, w)


def _xa_kernel(q_ref, kv_ref, y_ref):
    Dh = XA_HEAD_DIM
    for h in range(XA_HEADS):
        q = q_ref[0, :, h * Dh:(h + 1) * Dh]
        k = kv_ref[0, :, h * Dh:(h + 1) * Dh]
        v = kv_ref[0, :, XA_WIDTH + h * Dh:XA_WIDTH + (h + 1) * Dh]
        s = lax.dot_general(q, k, (((1,), (1,)), ((), ())), preferred_element_type=F32) * (Dh ** -0.5)
        p = jnp.exp(s - jnp.max(s, axis=-1, keepdims=True))
        o = jnp.dot(p.astype(BF16), v, preferred_element_type=F32)
        y_ref[0, :, h * Dh:(h + 1) * Dh] = (o / jnp.sum(p, axis=-1, keepdims=True)).astype(BF16)


def _xa(proj3, kv3, q_blk, tq):
    B, S, _ = proj3.shape
    M = kv3.shape[1]
    return pl.pallas_call(
        _xa_kernel,
        grid=(B, S // tq),
        in_specs=[
            pl.BlockSpec((1, tq, XA_WIDTH), lambda b, i: (b, i, q_blk)),
            pl.BlockSpec((1, M, 2 * XA_WIDTH), lambda b, i: (b, 0, 0)),
        ],
        out_specs=pl.BlockSpec((1, tq, XA_WIDTH), lambda b, i: (b, i, 0)),
        out_shape=jax.ShapeDtypeStruct((B, S, XA_WIDTH), BF16),
        compiler_params=_params(("parallel", "parallel"), VMEM_LIMIT),
        name="memxattn",
    )(proj3, kv3)


def _merge_kernel(x_ref, yml_ref, yna_ref, yxa_ref, p0_ref, p1_ref, p2_ref, bg_ref,
                  wml_ref, wna_ref, wxa_ref, wo_ref, wup_ref, wdn_ref, o_ref, wup_o_ref, wdn_o_ref):
    D = x_ref.shape[1]
    merged = None
    for n, (y_ref, p_ref, w_ref) in enumerate(
            ((yml_ref, p0_ref, wml_ref), (yna_ref, p1_ref, wna_ref), (yxa_ref, p2_ref, wxa_ref))):
        gate = 0.5 + 0.5 * jnp.tanh(0.5 * (p_ref[...].astype(F32) + bg_ref[:, n * D:(n + 1) * D]))
        term = gate * jnp.dot(y_ref[...], w_ref[...], preferred_element_type=F32)
        merged = term if merged is None else merged + term
    o_ref[...] = x_ref[...] + jnp.dot(merged.astype(BF16), wo_ref[...], preferred_element_type=F32)
    wup_o_ref[...] = wup_ref[...].astype(BF16)
    wdn_o_ref[...] = wdn_ref[...].astype(BF16)


def _merge(x2, yml, yna, yxa, proj2, b_gate, wml, wna, wxa, wo, w_up, w_down, gate_blk, tm):
    T, D = x2.shape
    steps = T // tm
    row = lambda w: pl.BlockSpec((tm, w), lambda i: (i, 0))
    full = lambda a: pl.BlockSpec(a.shape, lambda i: (0, 0))
    slab = lambda a: pl.BlockSpec((a.shape[0] // steps, a.shape[1]), lambda i: (i, 0))
    assert w_up.shape[0] % (steps * BF16_ROWS) == 0 and w_down.shape[0] % (steps * BF16_ROWS) == 0
    return pl.pallas_call(
        _merge_kernel,
        grid=(steps,),
        in_specs=[
            row(D), row(ML_WIDTH), row(NA_WIDTH), row(XA_WIDTH),
            pl.BlockSpec((tm, D), lambda i: (i, gate_blk)),
            pl.BlockSpec((tm, D), lambda i: (i, gate_blk + 1)),
            pl.BlockSpec((tm, D), lambda i: (i, gate_blk + 2)),
            full(b_gate), full(wml), full(wna), full(wxa), full(wo), slab(w_up), slab(w_down),
        ],
        out_specs=[row(D), slab(w_up), slab(w_down)],
        out_shape=[jax.ShapeDtypeStruct((T, D), F32),
                   jax.ShapeDtypeStruct(w_up.shape, BF16), jax.ShapeDtypeStruct(w_down.shape, BF16)],
        compiler_params=_params(("parallel",), VMEM_LIMIT),
        name="merge",
    )(x2, yml, yna, yxa, proj2, proj2, proj2, b_gate, wml, wna, wxa, wo, w_up, w_down)


def _gelu_tanh(x):
    return 0.5 * x * (1.0 + jnp.tanh(0.7978845608028654 * (x + 0.044715 * (x * x * x))))


def _ffn_kernel(x_ref, g_ref, wa0_ref, wa1_ref, wu0_ref, wu1_ref, wc0_ref, wc1_ref, bc0_ref, bc1_ref,
                wd0_ref, wd1_ref, gf_ref, o_ref, h_scr, a0_scr, u0_scr, a1_scr, u1_scr):
    t = pl.program_id(1)
    last = pl.num_programs(1) - 1
    S = x_ref.shape[1]

    def produce(wa_ref, wu_ref, a_scr, u_scr):
        h = h_scr[...]
        a_scr[...] = jnp.dot(h, wa_ref[...], preferred_element_type=F32)
        u_scr[...] = jnp.dot(h, wu_ref[...], preferred_element_type=F32)

    def consume(a_scr, u_scr, wc_ref, bc_ref, wd_ref):
        a = a_scr[...]
        row = lax.broadcasted_iota(jnp.int32, a.shape, 0)
        a_prev = jnp.where(row == 0, 0.0, pltpu.roll(a, 1, 0))
        a_next = jnp.where(row == S - 1, 0.0, pltpu.roll(a, S - 1, 0))
        w = wc_ref[...]
        conv = a_prev * w[0:1] + a * w[1:2] + a_next * w[2:3] + bc_ref[...]
        act = (_gelu_tanh(conv) * u_scr[...]).astype(BF16)
        o_ref[0] += jnp.dot(act, wd_ref[...], preferred_element_type=F32)

    @pl.when(t == 0)
    def _():
        x = x_ref[0]
        h_scr[...] = _rms(x, g_ref[...]).astype(BF16)
        o_ref[0] = x
        produce(wa0_ref, wu0_ref, a0_scr, u0_scr)
        produce(wa1_ref, wu1_ref, a1_scr, u1_scr)
        consume(a0_scr, u0_scr, wc1_ref, bc1_ref, wd1_ref)

    @pl.when((t > 0) & (t < last))
    def _():
        produce(wa0_ref, wu0_ref, a0_scr, u0_scr)
        consume(a1_scr, u1_scr, wc0_ref, bc0_ref, wd0_ref)
        produce(wa1_ref, wu1_ref, a1_scr, u1_scr)
        consume(a0_scr, u0_scr, wc1_ref, bc1_ref, wd1_ref)

    @pl.when(t == last)
    def _():
        produce(wa0_ref, wu0_ref, a0_scr, u0_scr)
        consume(a1_scr, u1_scr, wc0_ref, bc0_ref, wd0_ref)
        consume(a0_scr, u0_scr, wc1_ref, bc1_ref, wd1_ref)
        o_ref[0] = _rms(o_ref[0], gf_ref[...])


def _ffn(x3, g, w_up, w_conv, b_conv, w_down, g_final, tf):
    B, S, D = x3.shape
    FF = w_down.shape[0]
    nf = FF // tf
    assert nf % 2 == 1, "the two-chunk pipeline ends on a produce-one / consume-two step"
    steps = (nf + 1) // 2
    prod0 = lambda t: 2 * t
    prod1 = lambda t: jnp.minimum(2 * t + 1, nf - 1)
    cons0 = lambda t: jnp.maximum(2 * t - 1, 0)
    cons1 = lambda t: 2 * t
    col = lambda f, off=0: pl.BlockSpec((D, tf), lambda b, t: (0, off + f(t)))
    vec = lambda rows, f: pl.BlockSpec((rows, tf), lambda b, t: (0, f(t)))
    dn = lambda f: pl.BlockSpec((tf, D), lambda b, t: (f(t), 0))
    buf = pltpu.VMEM((S, tf), F32)
    return pl.pallas_call(
        _ffn_kernel,
        grid=(B, steps),
        in_specs=[
            pl.BlockSpec((1, S, D), lambda b, t: (b, 0, 0)),
            pl.BlockSpec((1, D), lambda b, t: (0, 0)),
            col(prod0), col(prod1), col(prod0, nf), col(prod1, nf),
            vec(3, cons0), vec(3, cons1), vec(1, cons0), vec(1, cons1),
            dn(cons0), dn(cons1),
            pl.BlockSpec((1, D), lambda b, t: (0, 0)),
        ],
        out_specs=pl.BlockSpec((1, S, D), lambda b, t: (b, 0, 0)),
        out_shape=jax.ShapeDtypeStruct((B, S, D), F32),
        scratch_shapes=[pltpu.VMEM((S, D), BF16), buf, buf, buf, buf],
        compiler_params=_params(("parallel", "arbitrary"), VMEM_LIMIT),
        name="ffn",
    )(x3, g, w_up, w_up, w_up, w_up, w_conv, w_conv, b_conv, b_conv, w_down, w_down, g_final)


def kernel(x, mem, mix_norm_g, w_in, b_ml_igate, b_ml_fgate, w_ml_conv, b_ml_conv, ml_norm_g, na_rpb, mem_norm_g, w_mem_kv, b_merge_gate, w_br_ml, w_br_na, w_br_xa, w_out, ffn_norm_g, w_ffn_up, w_ffn_conv, b_ffn_conv, w_ffn_down, final_norm_g):
    B, S, D = x.shape
    H = ML_HEADS
    M = mem.shape[1]
    T = B * S
    assert w_in.shape[0] == 1, "single-layer block: the FFN kernel also applies the final norm"
    assert S % ML_CHUNK == 0 and S % (GRID_W * NA_ROWS_PER_STEP) == 0
    l = 0
    n_gate = 4 * H
    g0 = 4 * ML_WIDTH
    row2 = lambda v: v.reshape(1, -1).astype(F32)

    w_main, w_gate = _wprep(jnp.transpose(w_in[l]), g0, n_gate, tr=1024)
    w_gate = jnp.transpose(w_gate.reshape(4, H, D), (1, 0, 2)).reshape(n_gate, D).astype(BF16)

    proj2, grow, w_kv_b, w_ml_b, w_na_b, w_xa_b, w_out_b = _inproj(
        x.reshape(T, D), row2(mix_norm_g[l]), w_main, w_gate,
        [w_mem_kv[l], w_br_ml[l], w_br_na[l], w_br_xa[l], w_out[l]], tm=1024, tn=3584)
    proj3 = proj2.reshape(B, S, -1)

    bias = jnp.stack([b_ml_igate[l][0], b_ml_fgate[l][0], b_ml_igate[l][1], b_ml_fgate[l][1]],
                     axis=-1).astype(F32)
    y_ml = _mlstm(proj3, grow.reshape(H, 4, T // ML_CHUNK, ML_CHUNK), bias[:, :, None],
                  w_ml_conv[l].astype(F32), row2(b_ml_conv[l]), row2(ml_norm_g[l]))

    nb = (4 * ML_WIDTH) // NA_WIDTH
    y_na = _na(proj3, _na_bias_patterns(na_rpb[l]), nb, nb + 1, nb + 2)

    kv = _memkv(mem.reshape(B * M, D), row2(mem_norm_g[l]), w_kv_b, tm=512)
    xb = (4 * ML_WIDTH + 3 * NA_WIDTH) // XA_WIDTH
    y_xa = _xa(proj3, kv.reshape(B, M, -1), xb, tq=S)

    gate_blk = (4 * ML_WIDTH + 3 * NA_WIDTH + XA_WIDTH) // D
    x1, w_up_b, w_down_b = _merge(
        x.reshape(T, D), y_ml.reshape(T, -1), y_na.reshape(T, -1), y_xa.reshape(T, -1),
        proj2, row2(b_merge_gate[l]), w_ml_b, w_na_b, w_xa_b, w_out_b, w_ffn_up[l], w_ffn_down[l], gate_blk, tm=1024)

    return _ffn(x1.reshape(B, S, D), row2(ffn_norm_g[l]), w_up_b,
                w_ffn_conv[l].astype(F32), row2(b_ffn_conv[l]), w_down_b,
                row2(final_norm_g), tf=256)
```

```python
import jax
import jax.numpy as jnp
from jax import lax
from jax.experimental import pallas as pl
from jax.experimental.pallas import tpu as pltpu

F32 = jnp.float32
BF16 = jnp.bfloat16
EPS = 1e-6
NEG_INIT = -1e30
LOG2E = 1.4426950408889634

GRID_W = 64
ML_HEADS = 4
ML_HEAD_DIM = 128
ML_WIDTH = ML_HEADS * ML_HEAD_DIM
NA_HEADS = 8
NA_HEAD_DIM = 64
NA_WIDTH = NA_HEADS * NA_HEAD_DIM
NA_WIN_ROWS = 8
NA_WIN_COLS = 16
XA_HEADS = 4
XA_HEAD_DIM = 128
XA_WIDTH = XA_HEADS * XA_HEAD_DIM

LANES = 128
F32_ROWS = 8
BF16_ROWS = 16
ML_CHUNK = 256
NA_ROWS_PER_STEP = 16
VMEM_LIMIT = 60 * 1024 * 1024


def _params(dims, vmem=None):
    return pltpu.CompilerParams(dimension_semantics=dims, vmem_limit_bytes=vmem)


def _rms(x, g):
    return x * lax.rsqrt(jnp.mean(x * x, axis=-1, keepdims=True) + EPS) * g


def _wprep_kernel(w_ref, wg_ref, o_ref, og_ref):
    o_ref[...] = w_ref[...].astype(BF16)
    og_ref[...] = wg_ref[...]


def _wprep(w_t, g0, n_gate, tr):
    N, D = w_t.shape
    assert g0 % tr == 0 and g0 % n_gate == 0 and (N - n_gate) % tr == 0
    return pl.pallas_call(
        _wprep_kernel,
        grid=((N - n_gate) // tr,),
        in_specs=[
            pl.BlockSpec((pl.Element(tr), pl.Element(D)),
                         lambda i: (pl.multiple_of(i * tr + jnp.where(i * tr >= g0, n_gate, 0), n_gate), 0)),
            pl.BlockSpec((n_gate, D), lambda i: (g0 // n_gate, 0)),
        ],
        out_specs=[pl.BlockSpec((tr, D), lambda i: (i, 0)),
                   pl.BlockSpec((n_gate, D), lambda i: (0, 0))],
        out_shape=[jax.ShapeDtypeStruct((N - n_gate, D), BF16),
                   jax.ShapeDtypeStruct((n_gate, D), F32)],
        compiler_params=_params(("arbitrary",), VMEM_LIMIT),
        name="wprep",
    )(w_t, w_t)


def _inproj_kernel(x_ref, g_ref, w_ref, wg_ref, *rest):
    n_cast = (len(rest) - 3) // 2
    cast_in, (o_ref, gates_ref), cast_out, h_scr = (
        rest[:n_cast], rest[n_cast:n_cast + 2], rest[n_cast + 2:2 * n_cast + 2], rest[-1])
    for src, dst in zip(cast_in, cast_out):
        dst[...] = src[...].astype(BF16)

    @pl.when(pl.program_id(1) == 0)
    def _():
        hb = _rms(x_ref[...], g_ref[...]).astype(BF16)
        h_scr[...] = hb
        gt = lax.dot_general(wg_ref[...], hb, (((1,), (1,)), ((), ())), preferred_element_type=F32)
        for h in range(ML_HEADS):
            gates_ref[h] = gt[4 * h:4 * h + 4, :]

    o_ref[...] = lax.dot_general(h_scr[...], w_ref[...], (((1,), (1,)), ((), ())),
                                 preferred_element_type=F32).astype(BF16)


def _inproj(x2, g, w_main, w_gate, to_cast, tm, tn):
    T, D = x2.shape
    N = w_main.shape[0]
    ni, nj = T // tm, N // tn
    steps = ni * nj
    assert all(a.shape[0] % (steps * BF16_ROWS) == 0 for a in to_cast)
    slab = lambda a: pl.BlockSpec((a.shape[0] // steps, a.shape[1]), lambda i, j: (i * nj + j, 0))
    return pl.pallas_call(
        _inproj_kernel,
        grid=(ni, nj),
        in_specs=[
            pl.BlockSpec((tm, D), lambda i, j: (i, 0)),
            pl.BlockSpec((1, D), lambda i, j: (0, 0)),
            pl.BlockSpec((tn, D), lambda i, j: (j, 0)),
            pl.BlockSpec(w_gate.shape, lambda i, j: (0, 0)),
        ] + [slab(a) for a in to_cast],
        out_specs=[
            pl.BlockSpec((tm, tn), lambda i, j: (i, j)),
            pl.BlockSpec((ML_HEADS, 4, tm), lambda i, j: (0, 0, i)),
        ] + [slab(a) for a in to_cast],
        out_shape=[
            jax.ShapeDtypeStruct((T, N), BF16),
            jax.ShapeDtypeStruct((ML_HEADS, 4, T), F32),
        ] + [jax.ShapeDtypeStruct(a.shape, BF16) for a in to_cast],
        scratch_shapes=[pltpu.VMEM((tm, D), BF16)],
        compiler_params=_params(("parallel", "arbitrary"), VMEM_LIMIT),
        name="inproj",
    )(x2, g, w_main, w_gate, *to_cast)


def _log_sigmoid(x):
    return jnp.minimum(x, 0.0) - jnp.log(1.0 + jnp.exp(-jnp.abs(x)))


def _chunk_scan(x, pos, L, reverse, op, identity):
    n = x.shape[1]
    d = 1
    while d < L:
        if reverse:
            x = op(x, jnp.where(pos < L - d, pltpu.roll(x, n - d, 1), identity))
        else:
            x = op(x, jnp.where(pos >= d, pltpu.roll(x, d, 1), identity))
        d *= 2
    return x


def _mlstm_kernel(q_ref, k_ref, v_ref, o_ref, gr_ref, br_ref,
                  wq_ref, wk_ref, bq_ref, bk_ref, ng_ref, y_ref,
                  qt_scr, k_scr, vat_scr, bb_scr, hf_scr, hb_scr):
    S = q_ref.shape[1]
    L = ML_CHUNK
    NC = S // L
    Dh = ML_HEAD_DIM

    pos = lax.broadcasted_iota(jnp.int32, (NC, L), 1)
    gates = []
    for d, rev in ((0, False), (1, True)):
        i_pre = gr_ref[0, 2 * d] + br_ref[0, 2 * d:2 * d + 1, :]
        lf = _log_sigmoid(gr_ref[0, 2 * d + 1] + br_ref[0, 2 * d + 1:2 * d + 2, :])
        a = _chunk_scan(lf, pos, L, rev, jnp.add, 0.0)
        b = i_pre - a
        gates.append((b, a, _chunk_scan(b, pos, L, rev, jnp.maximum, -jnp.inf), lf))

    def conv_silu(x_ref, w_ref, b_ref, c):
        x = x_ref[0, c * L:(c + 1) * L, :].astype(F32)
        zero_row = jnp.zeros((1, Dh), F32)
        prev_row = x_ref[0, c * L - BF16_ROWS:c * L, :].astype(F32)[BF16_ROWS - 1:, :] if c > 0 else zero_row
        next_row = x_ref[0, (c + 1) * L:(c + 1) * L + BF16_ROWS, :].astype(F32)[0:1, :] if c < NC - 1 else zero_row
        tile = F32_ROWS
        row = lax.broadcasted_iota(jnp.int32, (tile, Dh), 0)
        x_prev = pltpu.roll(x, 1, 0)
        x_prev = jnp.concatenate([jnp.where(row == 0, prev_row, x_prev[:tile]), x_prev[tile:]], axis=0)
        x_next = pltpu.roll(x, L - 1, 0)
        x_next = jnp.concatenate([x_next[:L - tile], jnp.where(row == tile - 1, next_row, x_next[L - tile:])], axis=0)
        w = w_ref[...]
        y = x_prev * w[0:1] + x * w[1:2] + x_next * w[2:3] + b_ref[...]
        return y / (1.0 + jnp.exp2(y * (-LOG2E)))

    vat_scr[Dh:, :] = jnp.ones((Dh, S), BF16)

    def prep(c):
        cs = slice(c * L, (c + 1) * L)
        qt_scr[:, cs] = (conv_silu(q_ref, wq_ref, bq_ref, c) * (Dh ** -0.5)).astype(BF16).T
        k_scr[cs, :] = conv_silu(k_ref, wk_ref, bk_ref, c).astype(BF16)
        vat_scr[:Dh, cs] = v_ref[0, cs, :].T
        for d in range(2):
            bb_scr[d, cs, :] = jnp.broadcast_to(gates[d][0][c:c + 1, :] * LOG2E, (LANES, L)).T

    s_idx = lax.broadcasted_iota(jnp.int32, (L, L), 0)
    j_idx = lax.broadcasted_iota(jnp.int32, (L, L), 1)

    def chunk(c, m, st_t, h_scr, fwd):
        cs = slice(c * L, (c + 1) * L)
        d = 0 if fwd else 1
        b_row, a_row, bmax_row, lf_row = (t[c:c + 1, :] for t in gates[d])
        mask_t = (s_idx <= j_idx) if fwd else (s_idx >= j_idx)
        g = jnp.sum(lf_row, axis=1, keepdims=True)
        w_end = g + b_row
        m_new = jnp.maximum(g + m, jnp.max(w_end, axis=1, keepdims=True))
        decay = jnp.exp(g + m - m_new)
        wt = jnp.exp(w_end - m_new)
        c_row = jnp.maximum(m, bmax_row)
        inter_w = jnp.exp(m - c_row)
        floor = jnp.exp(-(a_row + c_row))
        k = k_scr[cs, :]
        q_t = qt_scr[:, cs]
        va_t = vat_scr[:, cs]
        qk_t = jnp.dot(k, q_t, preferred_element_type=F32)
        b_col = bb_scr[d, cs, :]
        expo = jnp.concatenate([b_col] * (L // LANES), axis=1) - c_row * LOG2E
        p_t = jnp.exp2(jnp.where(mask_t, expo, -jnp.inf)) * qk_t
        q_in = q_t * inter_w.astype(BF16)
        nd_t = (jnp.dot(va_t, p_t.astype(BF16), preferred_element_type=F32)
                + jnp.dot(st_t.astype(BF16), q_in, preferred_element_type=F32))
        h_scr[:, cs] = nd_t[:Dh] / jnp.maximum(jnp.abs(nd_t[Dh:]), floor)
        wt_b = wt.astype(BF16)
        vw = jnp.concatenate([va_t[:Dh] * wt_b, jnp.broadcast_to(wt_b, (Dh, L))], axis=0)
        return m_new, decay * st_t + jnp.dot(vw, k, preferred_element_type=F32)

    def finish(c):
        cs = slice(c * L, (c + 1) * L)
        h_t = hf_scr[:, cs] + hb_scr[:, cs]
        h_t = h_t * lax.rsqrt(jnp.mean(h_t * h_t, axis=0, keepdims=True) + EPS)
        y = h_t.T * ng_ref[...] * jax.nn.sigmoid(o_ref[0, cs, :].astype(F32))
        y_ref[0, cs, :] = y.astype(BF16)

    m_f = m_b = jnp.full((1, 1), NEG_INIT, F32)
    st_f = st_b = jnp.zeros((2 * Dh, Dh), F32)
    prep(0)
    prep(NC - 1)
    for c in range(NC):
        if c + 1 < NC - 1 - (c + 1):
            prep(c + 1)
            prep(NC - 2 - c)
        elif c + 1 == NC - 1 - (c + 1):
            prep(c + 1)
        m_f, st_f = chunk(c, m_f, st_f, hf_scr, True)
        m_b, st_b = chunk(NC - 1 - c, m_b, st_b, hb_scr, False)
        if c >= NC - 1 - c:
            finish(c)
            if c != NC - 1 - c:
                finish(NC - 1 - c)


def _mlstm(proj3, grow, brow, w_conv, b_conv, norm_g):
    B, S, _ = proj3.shape
    H, Dh = ML_HEADS, ML_HEAD_DIM
    blk = lambda off: pl.BlockSpec((1, S, Dh), lambda b, h: (b, 0, off + h))
    return pl.pallas_call(
        _mlstm_kernel,
        grid=(B, H),
        in_specs=[
            blk(0), blk(H), blk(2 * H), blk(3 * H),
            pl.BlockSpec((1, 4, S // ML_CHUNK, ML_CHUNK), lambda b, h: (h, 0, b, 0)),
            pl.BlockSpec((1, 4, 1), lambda b, h: (h, 0, 0)),
            pl.BlockSpec((3, Dh), lambda b, h: (0, h)),
            pl.BlockSpec((3, Dh), lambda b, h: (0, H + h)),
            pl.BlockSpec((1, Dh), lambda b, h: (0, h)),
            pl.BlockSpec((1, Dh), lambda b, h: (0, H + h)),
            pl.BlockSpec((1, Dh), lambda b, h: (0, h)),
        ],
        out_specs=pl.BlockSpec((1, S, Dh), lambda b, h: (b, 0, h)),
        out_shape=jax.ShapeDtypeStruct((B, S, ML_WIDTH), BF16),
        scratch_shapes=[
            pltpu.VMEM((Dh, S), BF16), pltpu.VMEM((S, Dh), BF16), pltpu.VMEM((2 * Dh, S), BF16),
            pltpu.VMEM((2, S, LANES), F32),
            pltpu.VMEM((Dh, S), F32), pltpu.VMEM((Dh, S), F32),
        ],
        compiler_params=_params(("parallel", "parallel"), VMEM_LIMIT),
        name="mlstm",
    )(proj3, proj3, proj3, proj3, grow, brow, w_conv, w_conv, b_conv, b_conv, norm_g)


def _na_kernel(q_ref, k_ref, v_ref, pat_ref, wup_ref, wdn_ref, y_ref, wup_o_ref, wdn_o_ref, t_scr):
    wup_o_ref[...] = wup_ref[...].astype(BF16)
    wdn_o_ref[...] = wdn_ref[...].astype(BF16)

    S = k_ref.shape[1]
    rows = S // GRID_W
    W = GRID_W
    n_pairs = NA_HEADS // 2
    n_dr = t_scr.shape[0]
    r0 = pl.program_id(1) * NA_ROWS_PER_STEP
    low = lax.broadcasted_iota(jnp.int32, (W, LANES), 1) < NA_HEAD_DIM

    @pl.when((pl.program_id(0) == 0) & (pl.program_id(1) == 0))
    def _():
        cq = lax.broadcasted_iota(jnp.int32, (W, LANES), 0)
        ck = lax.broadcasted_iota(jnp.int32, (W, LANES), 1) % W
        col_start = jnp.clip(cq - NA_WIN_COLS // 2, 0, W - NA_WIN_COLS)
        col_ok = (ck >= col_start) & (ck < col_start + NA_WIN_COLS)
        for d in range(n_dr):
            for h in range(NA_HEADS):
                pattern = jnp.broadcast_to(pat_ref[h, d:d + 1, :], (W, LANES))
                skewed = pltpu.roll(pattern, 0, 1, stride=1, stride_axis=0)
                t_scr[d, h // 2, (h % 2) * W:(h % 2 + 1) * W, :] = jnp.where(col_ok, skewed, -jnp.inf)

    ones = jnp.ones((NA_WIN_ROWS * W, LANES), BF16)

    def row_body(i, _):
        r = r0 + i
        row_start = jnp.clip(r - NA_WIN_ROWS // 2, 0, rows - NA_WIN_ROWS)
        dr0 = row_start - r + (NA_WIN_ROWS - 1)
        band = pl.ds(pl.multiple_of(row_start * W, W), NA_WIN_ROWS * W)
        qrow = pl.ds(pl.multiple_of(i * W, W), W)
        scores = []
        for hp in range(n_pairs):
            cols = slice(hp * LANES, (hp + 1) * LANES)
            q2 = q_ref[0, qrow, cols] * (NA_HEAD_DIM ** -0.5)
            zero = jnp.zeros_like(q2)
            qs = jnp.concatenate([jnp.where(low, q2, zero), jnp.where(low, zero, q2)], axis=0)
            s = lax.dot_general(qs, k_ref[0, band, cols], (((1,), (1,)), ((), ())),
                                preferred_element_type=F32)
            bias = jnp.concatenate([t_scr[dr0 + kk, hp] for kk in range(0, NA_WIN_ROWS, 2)], axis=-1)
            scores.append(s + bias)
        probs = [jnp.exp(s - jnp.max(s, axis=-1, keepdims=True)).astype(BF16) for s in scores]
        for hp in range(n_pairs):
            cols = slice(hp * LANES, (hp + 1) * LANES)
            v_aug = jnp.concatenate([v_ref[0, band, cols], ones], axis=1)
            o = jnp.dot(probs[hp], v_aug, preferred_element_type=F32)
            o = o[:, :LANES] / o[:, LANES:]
            y_ref[0, qrow, cols] = jnp.where(low, o[:W], o[W:]).astype(BF16)
        return 0

    lax.fori_loop(0, NA_ROWS_PER_STEP, row_body, 0, unroll=4)


def _na(proj3, patterns, w_up, w_down, q_blk, k_blk, v_blk):
    B, S, _ = proj3.shape
    rows = S // GRID_W
    rb = NA_ROWS_PER_STEP
    n_dr = patterns.shape[1]
    nr = rows // rb
    steps = B * nr
    assert w_up.shape[0] % (steps * BF16_ROWS) == 0 and w_down.shape[0] % (steps * BF16_ROWS) == 0
    slab = lambda a: pl.BlockSpec((a.shape[0] // steps, a.shape[1]), lambda b, r: (b * nr + r, 0))
    return pl.pallas_call(
        _na_kernel,
        grid=(B, nr),
        in_specs=[
            pl.BlockSpec((1, rb * GRID_W, NA_WIDTH), lambda b, r: (b, r, q_blk)),
            pl.BlockSpec((1, S, NA_WIDTH), lambda b, r: (b, 0, k_blk)),
            pl.BlockSpec((1, S, NA_WIDTH), lambda b, r: (b, 0, v_blk)),
            pl.BlockSpec(patterns.shape, lambda b, r: (0, 0, 0)),
            slab(w_up), slab(w_down),
        ],
        out_specs=[pl.BlockSpec((1, rb * GRID_W, NA_WIDTH), lambda b, r: (b, r, 0)), slab(w_up), slab(w_down)],
        out_shape=[jax.ShapeDtypeStruct((B, S, NA_WIDTH), BF16),
                   jax.ShapeDtypeStruct(w_up.shape, BF16), jax.ShapeDtypeStruct(w_down.shape, BF16)],
        scratch_shapes=[pltpu.VMEM((n_dr, NA_HEADS // 2, 2 * GRID_W, LANES), F32)],
        compiler_params=_params(("arbitrary", "arbitrary"), VMEM_LIMIT),
        name="natten",
    )(proj3, proj3, proj3, patterns, w_up, w_down)


def _na_bias_patterns(rpb):
    c0 = NA_WIN_COLS - 1
    r = rpb.astype(F32)
    gap = jnp.zeros(r.shape[:1] + (r.shape[1] - 1, LANES // 2 - NA_WIN_COLS - c0), F32)
    return jnp.concatenate([r[:, :-1, c0:], gap, r[:, 1:, :], gap, r[:, :-1, :c0]], axis=-1)


def _memkv_kernel(m_ref, g_ref, w_ref, o_ref):
    hb = _rms(m_ref[...], g_ref[...]).astype(BF16)
    o_ref[...] = jnp.dot(hb, w_ref[...], preferred_element_type=F32).astype(BF16)


def _memkv(mem2, g, w, tm):
    T, D = mem2.shape
    N = w.shape[1]
    return pl.pallas_call(
        _memkv_kernel,
        grid=(T // tm,),
        in_specs=[
            pl.BlockSpec((tm, D), lambda i: (i, 0)),
            pl.BlockSpec((1, D), lambda i: (0, 0)),
            pl.BlockSpec((D, N), lambda i: (0, 0)),
        ],
        out_specs=pl.BlockSpec((tm, N), lambda i: (i, 0)),
        out_shape=jax.ShapeDtypeStruct((T, N), BF16),
        compiler_params=_params(("parallel",), VMEM_LIMIT),
        name="memkv",
    )(mem2, g, w)


def _xa_kernel(q_ref, kv_ref, y_ref):
    Dh = XA_HEAD_DIM
    for h in range(XA_HEADS):
        q = q_ref[0, :, h * Dh:(h + 1) * Dh]
        k = kv_ref[0, :, h * Dh:(h + 1) * Dh]
        v = kv_ref[0, :, XA_WIDTH + h * Dh:XA_WIDTH + (h + 1) * Dh]
        s = lax.dot_general(q, k, (((1,), (1,)), ((), ())), preferred_element_type=F32) * (Dh ** -0.5)
        p = jnp.exp(s - jnp.max(s, axis=-1, keepdims=True))
        o = jnp.dot(p.astype(BF16), v, preferred_element_type=F32)
        y_ref[0, :, h * Dh:(h + 1) * Dh] = (o / jnp.sum(p, axis=-1, keepdims=True)).astype(BF16)


def _xa(proj3, kv3, q_blk, tq):
    B, S, _ = proj3.shape
    M = kv3.shape[1]
    return pl.pallas_call(
        _xa_kernel,
        grid=(B, S // tq),
        in_specs=[
            pl.BlockSpec((1, tq, XA_WIDTH), lambda b, i: (b, i, q_blk)),
            pl.BlockSpec((1, M, 2 * XA_WIDTH), lambda b, i: (b, 0, 0)),
        ],
        out_specs=pl.BlockSpec((1, tq, XA_WIDTH), lambda b, i: (b, i, 0)),
        out_shape=jax.ShapeDtypeStruct((B, S, XA_WIDTH), BF16),
        compiler_params=_params(("parallel", "parallel"), VMEM_LIMIT),
        name="memxattn",
    )(proj3, kv3)


def _merge_kernel(x_ref, yml_ref, yna_ref, yxa_ref, p0_ref, p1_ref, p2_ref, bg_ref,
                  wml_ref, wna_ref, wxa_ref, wo_ref, o_ref):
    D = x_ref.shape[1]
    merged = None
    for n, (y_ref, p_ref, w_ref) in enumerate(
            ((yml_ref, p0_ref, wml_ref), (yna_ref, p1_ref, wna_ref), (yxa_ref, p2_ref, wxa_ref))):
        gate = 0.5 + 0.5 * jnp.tanh(0.5 * (p_ref[...].astype(F32) + bg_ref[:, n * D:(n + 1) * D]))
        term = gate * jnp.dot(y_ref[...], w_ref[...], preferred_element_type=F32)
        merged = term if merged is None else merged + term
    o_ref[...] = x_ref[...] + jnp.dot(merged.astype(BF16), wo_ref[...], preferred_element_type=F32)


def _merge(x2, yml, yna, yxa, proj2, b_gate, wml, wna, wxa, wo, gate_blk, tm):
    T, D = x2.shape
    row = lambda w: pl.BlockSpec((tm, w), lambda i: (i, 0))
    full = lambda a: pl.BlockSpec(a.shape, lambda i: (0, 0))
    return pl.pallas_call(
        _merge_kernel,
        grid=(T // tm,),
        in_specs=[
            row(D), row(ML_WIDTH), row(NA_WIDTH), row(XA_WIDTH),
            pl.BlockSpec((tm, D), lambda i: (i, gate_blk)),
            pl.BlockSpec((tm, D), lambda i: (i, gate_blk + 1)),
            pl.BlockSpec((tm, D), lambda i: (i, gate_blk + 2)),
            full(b_gate), full(wml), full(wna), full(wxa), full(wo),
        ],
        out_specs=row(D),
        out_shape=jax.ShapeDtypeStruct((T, D), F32),
        compiler_params=_params(("parallel",), VMEM_LIMIT),
        name="merge",
    )(x2, yml, yna, yxa, proj2, proj2, proj2, b_gate, wml, wna, wxa, wo)


def _gelu_tanh(x):
    return 0.5 * x * (1.0 + jnp.tanh(0.7978845608028654 * (x + 0.044715 * (x * x * x))))


def _ffn_kernel(x_ref, g_ref, wa0_ref, wa1_ref, wu0_ref, wu1_ref, wc0_ref, wc1_ref, bc0_ref, bc1_ref,
                wd0_ref, wd1_ref, gf_ref, o_ref, h_scr, a0_scr, u0_scr, a1_scr, u1_scr):
    t = pl.program_id(1)
    last = pl.num_programs(1) - 1
    S = x_ref.shape[1]

    def produce(wa_ref, wu_ref, a_scr, u_scr):
        h = h_scr[...]
        a_scr[...] = jnp.dot(h, wa_ref[...], preferred_element_type=F32)
        u_scr[...] = jnp.dot(h, wu_ref[...], preferred_element_type=F32)

    def consume(a_scr, u_scr, wc_ref, bc_ref, wd_ref):
        a = a_scr[...]
        row = lax.broadcasted_iota(jnp.int32, a.shape, 0)
        a_prev = jnp.where(row == 0, 0.0, pltpu.roll(a, 1, 0))
        a_next = jnp.where(row == S - 1, 0.0, pltpu.roll(a, S - 1, 0))
        w = wc_ref[...]
        conv = a_prev * w[0:1] + a * w[1:2] + a_next * w[2:3] + bc_ref[...]
        act = (_gelu_tanh(conv) * u_scr[...]).astype(BF16)
        o_ref[0] += jnp.dot(act, wd_ref[...], preferred_element_type=F32)

    @pl.when(t == 0)
    def _():
        x = x_ref[0]
        h_scr[...] = _rms(x, g_ref[...]).astype(BF16)
        o_ref[0] = x
        produce(wa0_ref, wu0_ref, a0_scr, u0_scr)
        produce(wa1_ref, wu1_ref, a1_scr, u1_scr)
        consume(a0_scr, u0_scr, wc1_ref, bc1_ref, wd1_ref)

    @pl.when((t > 0) & (t < last))
    def _():
        produce(wa0_ref, wu0_ref, a0_scr, u0_scr)
        consume(a1_scr, u1_scr, wc0_ref, bc0_ref, wd0_ref)
        produce(wa1_ref, wu1_ref, a1_scr, u1_scr)
        consume(a0_scr, u0_scr, wc1_ref, bc1_ref, wd1_ref)

    @pl.when(t == last)
    def _():
        produce(wa0_ref, wu0_ref, a0_scr, u0_scr)
        consume(a1_scr, u1_scr, wc0_ref, bc0_ref, wd0_ref)
        consume(a0_scr, u0_scr, wc1_ref, bc1_ref, wd1_ref)
        o_ref[0] = _rms(o_ref[0], gf_ref[...])


def _ffn(x3, g, w_up, w_conv, b_conv, w_down, g_final, tf):
    B, S, D = x3.shape
    FF = w_down.shape[0]
    nf = FF // tf
    assert nf % 2 == 1, "the two-chunk pipeline ends on a produce-one / consume-two step"
    steps = (nf + 1) // 2
    prod0 = lambda t: 2 * t
    prod1 = lambda t: jnp.minimum(2 * t + 1, nf - 1)
    cons0 = lambda t: jnp.maximum(2 * t - 1, 0)
    cons1 = lambda t: 2 * t
    col = lambda f, off=0: pl.BlockSpec((D, tf), lambda b, t: (0, off + f(t)))
    vec = lambda rows, f: pl.BlockSpec((rows, tf), lambda b, t: (0, f(t)))
    dn = lambda f: pl.BlockSpec((tf, D), lambda b, t: (f(t), 0))
    buf = pltpu.VMEM((S, tf), F32)
    return pl.pallas_call(
        _ffn_kernel,
        grid=(B, steps),
        in_specs=[
            pl.BlockSpec((1, S, D), lambda b, t: (b, 0, 0)),
            pl.BlockSpec((1, D), lambda b, t: (0, 0)),
            col(prod0), col(prod1), col(prod0, nf), col(prod1, nf),
            vec(3, cons0), vec(3, cons1), vec(1, cons0), vec(1, cons1),
            dn(cons0), dn(cons1),
            pl.BlockSpec((1, D), lambda b, t: (0, 0)),
        ],
        out_specs=pl.BlockSpec((1, S, D), lambda b, t: (b, 0, 0)),
        out_shape=jax.ShapeDtypeStruct((B, S, D), F32),
        scratch_shapes=[pltpu.VMEM((S, D), BF16), buf, buf, buf, buf],
        compiler_params=_params(("parallel", "arbitrary"), VMEM_LIMIT),
        name="ffn",
    )(x3, g, w_up, w_up, w_up, w_up, w_conv, w_conv, b_conv, b_conv, w_down, w_down, g_final)


def kernel(x, mem, mix_norm_g, w_in, b_ml_igate, b_ml_fgate, w_ml_conv, b_ml_conv, ml_norm_g, na_rpb, mem_norm_g, w_mem_kv, b_merge_gate, w_br_ml, w_br_na, w_br_xa, w_out, ffn_norm_g, w_ffn_up, w_ffn_conv, b_ffn_conv, w_ffn_down, final_norm_g):
    B, S, D = x.shape
    H = ML_HEADS
    M = mem.shape[1]
    T = B * S
    assert w_in.shape[0] == 1, "single-layer block: the FFN kernel also applies the final norm"
    assert S % ML_CHUNK == 0 and S % (GRID_W * NA_ROWS_PER_STEP) == 0
    l = 0
    n_gate = 4 * H
    g0 = 4 * ML_WIDTH
    row2 = lambda v: v.reshape(1, -1).astype(F32)

    w_main, w_gate = _wprep(jnp.transpose(w_in[l]), g0, n_gate, tr=1024)
    w_gate = jnp.transpose(w_gate.reshape(4, H, D), (1, 0, 2)).reshape(n_gate, D).astype(BF16)

    proj2, grow, w_kv_b, w_ml_b, w_na_b, w_xa_b, w_out_b = _inproj(
        x.reshape(T, D), row2(mix_norm_g[l]), w_main, w_gate,
        [w_mem_kv[l], w_br_ml[l], w_br_na[l], w_br_xa[l], w_out[l]], tm=1024, tn=3584)
    proj3 = proj2.reshape(B, S, -1)

    bias = jnp.stack([b_ml_igate[l][0], b_ml_fgate[l][0], b_ml_igate[l][1], b_ml_fgate[l][1]],
                     axis=-1).astype(F32)
    y_ml = _mlstm(proj3, grow.reshape(H, 4, T // ML_CHUNK, ML_CHUNK), bias[:, :, None],
                  w_ml_conv[l].astype(F32), row2(b_ml_conv[l]), row2(ml_norm_g[l]))

    nb = (4 * ML_WIDTH) // NA_WIDTH
    y_na, w_up_b, w_down_b = _na(proj3, _na_bias_patterns(na_rpb[l]), w_ffn_up[l], w_ffn_down[l],
                                 nb, nb + 1, nb + 2)

    kv = _memkv(mem.reshape(B * M, D), row2(mem_norm_g[l]), w_kv_b, tm=512)
    xb = (4 * ML_WIDTH + 3 * NA_WIDTH) // XA_WIDTH
    y_xa = _xa(proj3, kv.reshape(B, M, -1), xb, tq=S)

    gate_blk = (4 * ML_WIDTH + 3 * NA_WIDTH + XA_WIDTH) // D
    x1 = _merge(x.reshape(T, D), y_ml.reshape(T, -1), y_na.reshape(T, -1), y_xa.reshape(T, -1),
                proj2, row2(b_merge_gate[l]), w_ml_b, w_na_b, w_xa_b, w_out_b, gate_blk, tm=1024)

    return _ffn(x1.reshape(B, S, D), row2(ffn_norm_g[l]), w_up_b,
                w_ffn_conv[l].astype(F32), row2(b_ffn_conv[l]), w_down_b,
                row2(final_norm_g), tf=256)
```

```python
import jax
import jax.numpy as jnp
from jax import lax
from jax.experimental import pallas as pl
from jax.experimental.pallas import tpu as pltpu

F32 = jnp.float32
BF16 = jnp.bfloat16
EPS = 1e-6
NEG_INIT = -1e30
LOG2E = 1.4426950408889634

GRID_W = 64
ML_HEADS = 4
ML_HEAD_DIM = 128
ML_WIDTH = ML_HEADS * ML_HEAD_DIM
NA_HEADS = 8
NA_HEAD_DIM = 64
NA_WIDTH = NA_HEADS * NA_HEAD_DIM
NA_WIN_ROWS = 8
NA_WIN_COLS = 16
XA_HEADS = 4
XA_HEAD_DIM = 128
XA_WIDTH = XA_HEADS * XA_HEAD_DIM

LANES = 128
F32_ROWS = 8
BF16_ROWS = 16
ML_CHUNK = 256
NA_ROWS_PER_STEP = 16
FFN_UP_ROWS = 1024
FFN_DOWN_ROWS = 512
VMEM_LIMIT = 60 * 1024 * 1024


def _params(dims, vmem=None):
    return pltpu.CompilerParams(dimension_semantics=dims, vmem_limit_bytes=vmem)


def _rms(x, g):
    return x * lax.rsqrt(jnp.mean(x * x, axis=-1, keepdims=True) + EPS) * g


def _wprep_kernel(w_ref, wg_ref, o_ref, og_ref):
    o_ref[...] = w_ref[...].astype(BF16)
    og_ref[...] = wg_ref[...]


def _wprep(w_t, g0, n_gate, tr):
    N, D = w_t.shape
    assert g0 % tr == 0 and g0 % n_gate == 0 and (N - n_gate) % tr == 0
    return pl.pallas_call(
        _wprep_kernel,
        grid=((N - n_gate) // tr,),
        in_specs=[
            pl.BlockSpec((pl.Element(tr), pl.Element(D)),
                         lambda i: (pl.multiple_of(i * tr + jnp.where(i * tr >= g0, n_gate, 0), n_gate), 0)),
            pl.BlockSpec((n_gate, D), lambda i: (g0 // n_gate, 0)),
        ],
        out_specs=[pl.BlockSpec((tr, D), lambda i: (i, 0)),
                   pl.BlockSpec((n_gate, D), lambda i: (0, 0))],
        out_shape=[jax.ShapeDtypeStruct((N - n_gate, D), BF16),
                   jax.ShapeDtypeStruct((n_gate, D), F32)],
        compiler_params=_params(("arbitrary",), VMEM_LIMIT),
        name="wprep",
    )(w_t, w_t)


def _inproj_kernel(x_ref, g_ref, w_ref, wg_ref, *rest):
    n_cast = (len(rest) - 3) // 2
    cast_in, (o_ref, gates_ref), cast_out, h_scr = (
        rest[:n_cast], rest[n_cast:n_cast + 2], rest[n_cast + 2:2 * n_cast + 2], rest[-1])
    for src, dst in zip(cast_in, cast_out):
        dst[...] = src[...].astype(BF16)

    @pl.when(pl.program_id(1) == 0)
    def _():
        hb = _rms(x_ref[...], g_ref[...]).astype(BF16)
        h_scr[...] = hb
        gt = lax.dot_general(wg_ref[...], hb, (((1,), (1,)), ((), ())), preferred_element_type=F32)
        for h in range(ML_HEADS):
            gates_ref[h] = gt[4 * h:4 * h + 4, :]

    o_ref[...] = lax.dot_general(h_scr[...], w_ref[...], (((1,), (1,)), ((), ())),
                                 preferred_element_type=F32).astype(BF16)


def _inproj(x2, g, w_main, w_gate, to_cast, tm, tn):
    T, D = x2.shape
    N = w_main.shape[0]
    ni, nj = T // tm, N // tn
    steps = ni * nj
    assert all(a.shape[0] % (steps * BF16_ROWS) == 0 for a in to_cast)
    slab = lambda a: pl.BlockSpec((a.shape[0] // steps, a.shape[1]), lambda i, j: (i * nj + j, 0))
    return pl.pallas_call(
        _inproj_kernel,
        grid=(ni, nj),
        in_specs=[
            pl.BlockSpec((tm, D), lambda i, j: (i, 0)),
            pl.BlockSpec((1, D), lambda i, j: (0, 0)),
            pl.BlockSpec((tn, D), lambda i, j: (j, 0)),
            pl.BlockSpec(w_gate.shape, lambda i, j: (0, 0)),
        ] + [slab(a) for a in to_cast],
        out_specs=[
            pl.BlockSpec((tm, tn), lambda i, j: (i, j)),
            pl.BlockSpec((ML_HEADS, 4, tm), lambda i, j: (0, 0, i)),
        ] + [slab(a) for a in to_cast],
        out_shape=[
            jax.ShapeDtypeStruct((T, N), BF16),
            jax.ShapeDtypeStruct((ML_HEADS, 4, T), F32),
        ] + [jax.ShapeDtypeStruct(a.shape, BF16) for a in to_cast],
        scratch_shapes=[pltpu.VMEM((tm, D), BF16)],
        compiler_params=_params(("parallel", "arbitrary"), VMEM_LIMIT),
        name="inproj",
    )(x2, g, w_main, w_gate, *to_cast)


def _log_sigmoid(x):
    return jnp.minimum(x, 0.0) - jnp.log(1.0 + jnp.exp(-jnp.abs(x)))


def _chunk_scan(x, pos, L, reverse, op, identity):
    n = x.shape[1]
    d = 1
    while d < L:
        if reverse:
            x = op(x, jnp.where(pos < L - d, pltpu.roll(x, n - d, 1), identity))
        else:
            x = op(x, jnp.where(pos >= d, pltpu.roll(x, d, 1), identity))
        d *= 2
    return x


def _mlstm_kernel(q_ref, k_ref, v_ref, o_ref, gr_ref, br_ref,
                  wq_ref, wk_ref, bq_ref, bk_ref, ng_ref, y_ref,
                  qt_scr, k_scr, vat_scr, bb_scr, hf_scr, hb_scr):
    S = q_ref.shape[1]
    L = ML_CHUNK
    NC = S // L
    Dh = ML_HEAD_DIM

    pos = lax.broadcasted_iota(jnp.int32, (NC, L), 1)
    gates = []
    for d, rev in ((0, False), (1, True)):
        i_pre = gr_ref[0, 2 * d] + br_ref[0, 2 * d:2 * d + 1, :]
        lf = _log_sigmoid(gr_ref[0, 2 * d + 1] + br_ref[0, 2 * d + 1:2 * d + 2, :])
        a = _chunk_scan(lf, pos, L, rev, jnp.add, 0.0)
        b = i_pre - a
        gates.append((b, a, _chunk_scan(b, pos, L, rev, jnp.maximum, -jnp.inf), lf))

    def conv_silu(x_ref, w_ref, b_ref, c):
        x = x_ref[0, c * L:(c + 1) * L, :].astype(F32)
        zero_row = jnp.zeros((1, Dh), F32)
        prev_row = x_ref[0, c * L - BF16_ROWS:c * L, :].astype(F32)[BF16_ROWS - 1:, :] if c > 0 else zero_row
        next_row = x_ref[0, (c + 1) * L:(c + 1) * L + BF16_ROWS, :].astype(F32)[0:1, :] if c < NC - 1 else zero_row
        tile = F32_ROWS
        row = lax.broadcasted_iota(jnp.int32, (tile, Dh), 0)
        x_prev = pltpu.roll(x, 1, 0)
        x_prev = jnp.concatenate([jnp.where(row == 0, prev_row, x_prev[:tile]), x_prev[tile:]], axis=0)
        x_next = pltpu.roll(x, L - 1, 0)
        x_next = jnp.concatenate([x_next[:L - tile], jnp.where(row == tile - 1, next_row, x_next[L - tile:])], axis=0)
        w = w_ref[...]
        y = x_prev * w[0:1] + x * w[1:2] + x_next * w[2:3] + b_ref[...]
        return y / (1.0 + jnp.exp2(y * (-LOG2E)))

    vat_scr[Dh:, :] = jnp.ones((Dh, S), BF16)

    def prep(c):
        cs = slice(c * L, (c + 1) * L)
        qt_scr[:, cs] = (conv_silu(q_ref, wq_ref, bq_ref, c) * (Dh ** -0.5)).astype(BF16).T
        k_scr[cs, :] = conv_silu(k_ref, wk_ref, bk_ref, c).astype(BF16)
        vat_scr[:Dh, cs] = v_ref[0, cs, :].T
        for d in range(2):
            bb_scr[d, cs, :] = jnp.broadcast_to(gates[d][0][c:c + 1, :] * LOG2E, (LANES, L)).T

    s_idx = lax.broadcasted_iota(jnp.int32, (L, L), 0)
    j_idx = lax.broadcasted_iota(jnp.int32, (L, L), 1)

    def chunk(c, m, st_t, h_scr, fwd):
        cs = slice(c * L, (c + 1) * L)
        d = 0 if fwd else 1
        b_row, a_row, bmax_row, lf_row = (t[c:c + 1, :] for t in gates[d])
        mask_t = (s_idx <= j_idx) if fwd else (s_idx >= j_idx)
        g = jnp.sum(lf_row, axis=1, keepdims=True)
        w_end = g + b_row
        m_new = jnp.maximum(g + m, jnp.max(w_end, axis=1, keepdims=True))
        decay = jnp.exp(g + m - m_new)
        wt = jnp.exp(w_end - m_new)
        c_row = jnp.maximum(m, bmax_row)
        inter_w = jnp.exp(m - c_row)
        floor = jnp.exp(-(a_row + c_row))
        k = k_scr[cs, :]
        q_t = qt_scr[:, cs]
        va_t = vat_scr[:, cs]
        qk_t = jnp.dot(k, q_t, preferred_element_type=F32)
        b_col = bb_scr[d, cs, :]
        expo = jnp.concatenate([b_col] * (L // LANES), axis=1) - c_row * LOG2E
        p_t = jnp.exp2(jnp.where(mask_t, expo, -jnp.inf)) * qk_t
        q_in = q_t * inter_w.astype(BF16)
        nd_t = (jnp.dot(va_t, p_t.astype(BF16), preferred_element_type=F32)
                + jnp.dot(st_t.astype(BF16), q_in, preferred_element_type=F32))
        h_scr[:, cs] = nd_t[:Dh] / jnp.maximum(jnp.abs(nd_t[Dh:]), floor)
        wt_b = wt.astype(BF16)
        vw = jnp.concatenate([va_t[:Dh] * wt_b, jnp.broadcast_to(wt_b, (Dh, L))], axis=0)
        return m_new, decay * st_t + jnp.dot(vw, k, preferred_element_type=F32)

    def finish(c):
        cs = slice(c * L, (c + 1) * L)
        h_t = hf_scr[:, cs] + hb_scr[:, cs]
        h_t = h_t * lax.rsqrt(jnp.mean(h_t * h_t, axis=0, keepdims=True) + EPS)
        y = h_t.T * ng_ref[...] * jax.nn.sigmoid(o_ref[0, cs, :].astype(F32))
        y_ref[0, cs, :] = y.astype(BF16)

    m_f = m_b = jnp.full((1, 1), NEG_INIT, F32)
    st_f = st_b = jnp.zeros((2 * Dh, Dh), F32)
    prep(0)
    prep(NC - 1)
    for c in range(NC):
        if c + 1 < NC - 1 - (c + 1):
            prep(c + 1)
            prep(NC - 2 - c)
        elif c + 1 == NC - 1 - (c + 1):
            prep(c + 1)
        m_f, st_f = chunk(c, m_f, st_f, hf_scr, True)
        m_b, st_b = chunk(NC - 1 - c, m_b, st_b, hb_scr, False)
        if c >= NC - 1 - c:
            finish(c)
            if c != NC - 1 - c:
                finish(NC - 1 - c)


def _mlstm(proj3, grow, brow, w_conv, b_conv, norm_g):
    B, S, _ = proj3.shape
    H, Dh = ML_HEADS, ML_HEAD_DIM
    blk = lambda off: pl.BlockSpec((1, S, Dh), lambda b, h: (b, 0, off + h))
    return pl.pallas_call(
        _mlstm_kernel,
        grid=(B, H),
        in_specs=[
            blk(0), blk(H), blk(2 * H), blk(3 * H),
            pl.BlockSpec((1, 4, S // ML_CHUNK, ML_CHUNK), lambda b, h: (h, 0, b, 0)),
            pl.BlockSpec((1, 4, 1), lambda b, h: (h, 0, 0)),
            pl.BlockSpec((3, Dh), lambda b, h: (0, h)),
            pl.BlockSpec((3, Dh), lambda b, h: (0, H + h)),
            pl.BlockSpec((1, Dh), lambda b, h: (0, h)),
            pl.BlockSpec((1, Dh), lambda b, h: (0, H + h)),
            pl.BlockSpec((1, Dh), lambda b, h: (0, h)),
        ],
        out_specs=pl.BlockSpec((1, S, Dh), lambda b, h: (b, 0, h)),
        out_shape=jax.ShapeDtypeStruct((B, S, ML_WIDTH), BF16),
        scratch_shapes=[
            pltpu.VMEM((Dh, S), BF16), pltpu.VMEM((S, Dh), BF16), pltpu.VMEM((2 * Dh, S), BF16),
            pltpu.VMEM((2, S, LANES), F32),
            pltpu.VMEM((Dh, S), F32), pltpu.VMEM((Dh, S), F32),
        ],
        compiler_params=_params(("parallel", "parallel"), VMEM_LIMIT),
        name="mlstm",
    )(proj3, proj3, proj3, proj3, grow, brow, w_conv, w_conv, b_conv, b_conv, norm_g)


def _na_kernel(q_ref, k_ref, v_ref, pat_ref, wup_ref, wdn_ref, y_ref, wup_o_ref, wdn_o_ref, t_scr):
    wup_o_ref[...] = wup_ref[...].astype(BF16)
    wdn_o_ref[...] = wdn_ref[...].astype(BF16)

    S = k_ref.shape[1]
    rows = S // GRID_W
    W = GRID_W
    n_pairs = NA_HEADS // 2
    n_dr = t_scr.shape[0]
    r0 = pl.program_id(1) * NA_ROWS_PER_STEP
    low = lax.broadcasted_iota(jnp.int32, (W, LANES), 1) < NA_HEAD_DIM

    @pl.when((pl.program_id(0) == 0) & (pl.program_id(1) == 0))
    def _():
        cq = lax.broadcasted_iota(jnp.int32, (W, LANES), 0)
        ck = lax.broadcasted_iota(jnp.int32, (W, LANES), 1) % W
        col_start = jnp.clip(cq - NA_WIN_COLS // 2, 0, W - NA_WIN_COLS)
        col_ok = (ck >= col_start) & (ck < col_start + NA_WIN_COLS)
        for d in range(n_dr):
            for h in range(NA_HEADS):
                pattern = jnp.broadcast_to(pat_ref[h, d:d + 1, :], (W, LANES))
                skewed = pltpu.roll(pattern, 0, 1, stride=1, stride_axis=0)
                t_scr[d, h // 2, (h % 2) * W:(h % 2 + 1) * W, :] = jnp.where(col_ok, skewed, -jnp.inf)

    ones = jnp.ones((NA_WIN_ROWS * W, LANES), BF16)

    def row_body(i, _):
        r = r0 + i
        row_start = jnp.clip(r - NA_WIN_ROWS // 2, 0, rows - NA_WIN_ROWS)
        dr0 = row_start - r + (NA_WIN_ROWS - 1)
        band = pl.ds(pl.multiple_of(row_start * W, W), NA_WIN_ROWS * W)
        qrow = pl.ds(pl.multiple_of(i * W, W), W)
        scores = []
        for hp in range(n_pairs):
            cols = slice(hp * LANES, (hp + 1) * LANES)
            q2 = q_ref[0, qrow, cols] * (NA_HEAD_DIM ** -0.5)
            zero = jnp.zeros_like(q2)
            qs = jnp.concatenate([jnp.where(low, q2, zero), jnp.where(low, zero, q2)], axis=0)
            s = lax.dot_general(qs, k_ref[0, band, cols], (((1,), (1,)), ((), ())),
                                preferred_element_type=F32)
            bias = jnp.concatenate([t_scr[dr0 + kk, hp] for kk in range(0, NA_WIN_ROWS, 2)], axis=-1)
            scores.append(s + bias)
        probs = [jnp.exp(s - jnp.max(s, axis=-1, keepdims=True)).astype(BF16) for s in scores]
        for hp in range(n_pairs):
            cols = slice(hp * LANES, (hp + 1) * LANES)
            v_aug = jnp.concatenate([v_ref[0, band, cols], ones], axis=1)
            o = jnp.dot(probs[hp], v_aug, preferred_element_type=F32)
            o = o[:, :LANES] / o[:, LANES:]
            y_ref[0, qrow, cols] = jnp.where(low, o[:W], o[W:]).astype(BF16)
        return 0

    lax.fori_loop(0, NA_ROWS_PER_STEP, row_body, 0, unroll=8)


def _na(proj3, patterns, w_up, w_down, q_blk, k_blk, v_blk):
    B, S, _ = proj3.shape
    rows = S // GRID_W
    rb = NA_ROWS_PER_STEP
    n_dr = patterns.shape[1]
    nr = rows // rb
    steps = B * nr
    assert w_up.shape[0] % (steps * BF16_ROWS) == 0 and w_down.shape[0] % (steps * BF16_ROWS) == 0
    slab = lambda a: pl.BlockSpec((a.shape[0] // steps, a.shape[1]), lambda b, r: (b * nr + r, 0))
    return pl.pallas_call(
        _na_kernel,
        grid=(B, nr),
        in_specs=[
            pl.BlockSpec((1, rb * GRID_W, NA_WIDTH), lambda b, r: (b, r, q_blk)),
            pl.BlockSpec((1, S, NA_WIDTH), lambda b, r: (b, 0, k_blk)),
            pl.BlockSpec((1, S, NA_WIDTH), lambda b, r: (b, 0, v_blk)),
            pl.BlockSpec(patterns.shape, lambda b, r: (0, 0, 0)),
            slab(w_up), slab(w_down),
        ],
        out_specs=[pl.BlockSpec((1, rb * GRID_W, NA_WIDTH), lambda b, r: (b, r, 0)), slab(w_up), slab(w_down)],
        out_shape=[jax.ShapeDtypeStruct((B, S, NA_WIDTH), BF16),
                   jax.ShapeDtypeStruct(w_up.shape, BF16), jax.ShapeDtypeStruct(w_down.shape, BF16)],
        scratch_shapes=[pltpu.VMEM((n_dr, NA_HEADS // 2, 2 * GRID_W, LANES), F32)],
        compiler_params=_params(("arbitrary", "arbitrary"), VMEM_LIMIT),
        name="natten",
    )(proj3, proj3, proj3, patterns, w_up, w_down)


def _na_bias_patterns(rpb):
    c0 = NA_WIN_COLS - 1
    r = rpb.astype(F32)
    gap = jnp.zeros(r.shape[:1] + (r.shape[1] - 1, LANES // 2 - NA_WIN_COLS - c0), F32)
    return jnp.concatenate([r[:, :-1, c0:], gap, r[:, 1:, :], gap, r[:, :-1, :c0]], axis=-1)


def _memkv_kernel(m_ref, g_ref, w_ref, o_ref):
    hb = _rms(m_ref[...], g_ref[...]).astype(BF16)
    o_ref[...] = jnp.dot(hb, w_ref[...], preferred_element_type=F32).astype(BF16)


def _memkv(mem2, g, w, tm):
    T, D = mem2.shape
    N = w.shape[1]
    return pl.pallas_call(
        _memkv_kernel,
        grid=(T // tm,),
        in_specs=[
            pl.BlockSpec((tm, D), lambda i: (i, 0)),
            pl.BlockSpec((1, D), lambda i: (0, 0)),
            pl.BlockSpec((D, N), lambda i: (0, 0)),
        ],
        out_specs=pl.BlockSpec((tm, N), lambda i: (i, 0)),
        out_shape=jax.ShapeDtypeStruct((T, N), BF16),
        compiler_params=_params(("parallel",), VMEM_LIMIT),
        name="memkv",
    )(mem2, g, w)


def _xa_kernel(q_ref, kv_ref, y_ref):
    Dh = XA_HEAD_DIM
    for h in range(XA_HEADS):
        q = q_ref[0, :, h * Dh:(h + 1) * Dh]
        k = kv_ref[0, :, h * Dh:(h + 1) * Dh]
        v = kv_ref[0, :, XA_WIDTH + h * Dh:XA_WIDTH + (h + 1) * Dh]
        s = lax.dot_general(q, k, (((1,), (1,)), ((), ())), preferred_element_type=F32) * (Dh ** -0.5)
        p = jnp.exp(s - jnp.max(s, axis=-1, keepdims=True))
        o = jnp.dot(p.astype(BF16), v, preferred_element_type=F32)
        y_ref[0, :, h * Dh:(h + 1) * Dh] = (o / jnp.sum(p, axis=-1, keepdims=True)).astype(BF16)


def _xa(proj3, kv3, q_blk, tq):
    B, S, _ = proj3.shape
    M = kv3.shape[1]
    return pl.pallas_call(
        _xa_kernel,
        grid=(B, S // tq),
        in_specs=[
            pl.BlockSpec((1, tq, XA_WIDTH), lambda b, i: (b, i, q_blk)),
            pl.BlockSpec((1, M, 2 * XA_WIDTH), lambda b, i: (b, 0, 0)),
        ],
        out_specs=pl.BlockSpec((1, tq, XA_WIDTH), lambda b, i: (b, i, 0)),
        out_shape=jax.ShapeDtypeStruct((B, S, XA_WIDTH), BF16),
        compiler_params=_params(("parallel", "parallel"), VMEM_LIMIT),
        name="memxattn",
    )(proj3, kv3)


def _merge_kernel(x_ref, yml_ref, yna_ref, yxa_ref, p0_ref, p1_ref, p2_ref, bg_ref,
                  wml_ref, wna_ref, wxa_ref, wo_ref, o_ref):
    D = x_ref.shape[1]
    merged = None
    for n, (y_ref, p_ref, w_ref) in enumerate(
            ((yml_ref, p0_ref, wml_ref), (yna_ref, p1_ref, wna_ref), (yxa_ref, p2_ref, wxa_ref))):
        gate = 0.5 + 0.5 * jnp.tanh(0.5 * (p_ref[...].astype(F32) + bg_ref[:, n * D:(n + 1) * D]))
        term = gate * jnp.dot(y_ref[...], w_ref[...], preferred_element_type=F32)
        merged = term if merged is None else merged + term
    o_ref[...] = x_ref[...] + jnp.dot(merged.astype(BF16), wo_ref[...], preferred_element_type=F32)


def _merge(x2, yml, yna, yxa, proj2, b_gate, wml, wna, wxa, wo, gate_blk, tm):
    T, D = x2.shape
    row = lambda w: pl.BlockSpec((tm, w), lambda i: (i, 0))
    full = lambda a: pl.BlockSpec(a.shape, lambda i: (0, 0))
    return pl.pallas_call(
        _merge_kernel,
        grid=(T // tm,),
        in_specs=[
            row(D), row(ML_WIDTH), row(NA_WIDTH), row(XA_WIDTH),
            pl.BlockSpec((tm, D), lambda i: (i, gate_blk)),
            pl.BlockSpec((tm, D), lambda i: (i, gate_blk + 1)),
            pl.BlockSpec((tm, D), lambda i: (i, gate_blk + 2)),
            full(b_gate), full(wml), full(wna), full(wxa), full(wo),
        ],
        out_specs=row(D),
        out_shape=jax.ShapeDtypeStruct((T, D), F32),
        compiler_params=_params(("parallel",), VMEM_LIMIT),
        name="merge",
    )(x2, yml, yna, yxa, proj2, proj2, proj2, b_gate, wml, wna, wxa, wo)


def _gelu_tanh(x):
    return 0.5 * x * (1.0 + jnp.tanh(0.7978845608028654 * (x + 0.044715 * (x * x * x))))


def _ffn_kernel(x_ref, g_ref, wa0_ref, wa1_ref, wu0_ref, wu1_ref, wc0_ref, wc1_ref, bc0_ref, bc1_ref,
                wd0_ref, wd1_ref, gf_ref, o_ref, h_scr, a0_scr, u0_scr, a1_scr, u1_scr):
    t = pl.program_id(1)
    last = pl.num_programs(1) - 1
    S = x_ref.shape[1]

    def produce(wa_ref, wu_ref, a_scr, u_scr):
        for r in range(0, S, FFN_UP_ROWS):
            h = h_scr[r:r + FFN_UP_ROWS, :]
            a_scr[r:r + FFN_UP_ROWS, :] = jnp.dot(h, wa_ref[...], preferred_element_type=F32)
            u_scr[r:r + FFN_UP_ROWS, :] = jnp.dot(h, wu_ref[...], preferred_element_type=F32)

    def consume(a_scr, u_scr, wc_ref, bc_ref, wd_ref):
        a = a_scr[...]
        row = lax.broadcasted_iota(jnp.int32, a.shape, 0)
        a_prev = jnp.where(row == 0, 0.0, pltpu.roll(a, 1, 0))
        a_next = jnp.where(row == S - 1, 0.0, pltpu.roll(a, S - 1, 0))
        w = wc_ref[...]
        bc = bc_ref[...]
        for r in range(0, S, FFN_DOWN_ROWS):
            rows = slice(r, r + FFN_DOWN_ROWS)
            conv = a_prev[rows] * w[0:1] + a[rows] * w[1:2] + a_next[rows] * w[2:3] + bc
            act = (_gelu_tanh(conv) * u_scr[rows, :]).astype(BF16)
            o_ref[0, rows, :] += jnp.dot(act, wd_ref[...], preferred_element_type=F32)

    @pl.when(t == 0)
    def _():
        x = x_ref[0]
        h_scr[...] = _rms(x, g_ref[...]).astype(BF16)
        o_ref[0] = x
        produce(wa0_ref, wu0_ref, a0_scr, u0_scr)
        produce(wa1_ref, wu1_ref, a1_scr, u1_scr)
        consume(a0_scr, u0_scr, wc1_ref, bc1_ref, wd1_ref)

    @pl.when((t > 0) & (t < last))
    def _():
        produce(wa0_ref, wu0_ref, a0_scr, u0_scr)
        consume(a1_scr, u1_scr, wc0_ref, bc0_ref, wd0_ref)
        produce(wa1_ref, wu1_ref, a1_scr, u1_scr)
        consume(a0_scr, u0_scr, wc1_ref, bc1_ref, wd1_ref)

    @pl.when(t == last)
    def _():
        produce(wa0_ref, wu0_ref, a0_scr, u0_scr)
        consume(a1_scr, u1_scr, wc0_ref, bc0_ref, wd0_ref)
        consume(a0_scr, u0_scr, wc1_ref, bc1_ref, wd1_ref)
        o_ref[0] = _rms(o_ref[0], gf_ref[...])


def _ffn(x3, g, w_up, w_conv, b_conv, w_down, g_final, tf):
    B, S, D = x3.shape
    FF = w_down.shape[0]
    nf = FF // tf
    assert nf % 2 == 1, "the two-chunk pipeline ends on a produce-one / consume-two step"
    steps = (nf + 1) // 2
    prod0 = lambda t: 2 * t
    prod1 = lambda t: jnp.minimum(2 * t + 1, nf - 1)
    cons0 = lambda t: jnp.maximum(2 * t - 1, 0)
    cons1 = lambda t: 2 * t
    col = lambda f, off=0: pl.BlockSpec((D, tf), lambda b, t: (0, off + f(t)))
    vec = lambda rows, f: pl.BlockSpec((rows, tf), lambda b, t: (0, f(t)))
    dn = lambda f: pl.BlockSpec((tf, D), lambda b, t: (f(t), 0))
    buf = pltpu.VMEM((S, tf), F32)
    return pl.pallas_call(
        _ffn_kernel,
        grid=(B, steps),
        in_specs=[
            pl.BlockSpec((1, S, D), lambda b, t: (b, 0, 0)),
            pl.BlockSpec((1, D), lambda b, t: (0, 0)),
            col(prod0), col(prod1), col(prod0, nf), col(prod1, nf),
            vec(3, cons0), vec(3, cons1), vec(1, cons0), vec(1, cons1),
            dn(cons0), dn(cons1),
            pl.BlockSpec((1, D), lambda b, t: (0, 0)),
        ],
        out_specs=pl.BlockSpec((1, S, D), lambda b, t: (b, 0, 0)),
        out_shape=jax.ShapeDtypeStruct((B, S, D), F32),
        scratch_shapes=[pltpu.VMEM((S, D), BF16), buf, buf, buf, buf],
        compiler_params=_params(("parallel", "arbitrary"), VMEM_LIMIT),
        name="ffn",
    )(x3, g, w_up, w_up, w_up, w_up, w_conv, w_conv, b_conv, b_conv, w_down, w_down, g_final)


def kernel(x, mem, mix_norm_g, w_in, b_ml_igate, b_ml_fgate, w_ml_conv, b_ml_conv, ml_norm_g, na_rpb, mem_norm_g, w_mem_kv, b_merge_gate, w_br_ml, w_br_na, w_br_xa, w_out, ffn_norm_g, w_ffn_up, w_ffn_conv, b_ffn_conv, w_ffn_down, final_norm_g):
    B, S, D = x.shape
    H = ML_HEADS
    M = mem.shape[1]
    T = B * S
    assert w_in.shape[0] == 1, "single-layer block: the FFN kernel also applies the final norm"
    assert S % ML_CHUNK == 0 and S % (GRID_W * NA_ROWS_PER_STEP) == 0
    l = 0
    n_gate = 4 * H
    g0 = 4 * ML_WIDTH
    row2 = lambda v: v.reshape(1, -1).astype(F32)

    w_main, w_gate = _wprep(jnp.transpose(w_in[l]), g0, n_gate, tr=1024)
    w_gate = jnp.transpose(w_gate.reshape(4, H, D), (1, 0, 2)).reshape(n_gate, D).astype(BF16)

    proj2, grow, w_kv_b, w_ml_b, w_na_b, w_xa_b, w_out_b = _inproj(
        x.reshape(T, D), row2(mix_norm_g[l]), w_main, w_gate,
        [w_mem_kv[l], w_br_ml[l], w_br_na[l], w_br_xa[l], w_out[l]], tm=1024, tn=3584)
    proj3 = proj2.reshape(B, S, -1)

    bias = jnp.stack([b_ml_igate[l][0], b_ml_fgate[l][0], b_ml_igate[l][1], b_ml_fgate[l][1]],
                     axis=-1).astype(F32)
    y_ml = _mlstm(proj3, grow.reshape(H, 4, T // ML_CHUNK, ML_CHUNK), bias[:, :, None],
                  w_ml_conv[l].astype(F32), row2(b_ml_conv[l]), row2(ml_norm_g[l]))

    nb = (4 * ML_WIDTH) // NA_WIDTH
    y_na, w_up_b, w_down_b = _na(proj3, _na_bias_patterns(na_rpb[l]), w_ffn_up[l], w_ffn_down[l],
                                 nb, nb + 1, nb + 2)

    kv = _memkv(mem.reshape(B * M, D), row2(mem_norm_g[l]), w_kv_b, tm=512)
    xb = (4 * ML_WIDTH + 3 * NA_WIDTH) // XA_WIDTH
    y_xa = _xa(proj3, kv.reshape(B, M, -1), xb, tq=S)

    gate_blk = (4 * ML_WIDTH + 3 * NA_WIDTH + XA_WIDTH) // D
    x1 = _merge(x.reshape(T, D), y_ml.reshape(T, -1), y_na.reshape(T, -1), y_xa.reshape(T, -1),
                proj2, row2(b_merge_gate[l]), w_ml_b, w_na_b, w_xa_b, w_out_b, gate_blk, tm=1024)

    return _ffn(x1.reshape(B, S, D), row2(ffn_norm_g[l]), w_up_b,
                w_ffn_conv[l].astype(F32), row2(b_ffn_conv[l]), w_down_b,
                row2(final_norm_g), tf=256)
```

```python
import jax
import jax.numpy as jnp
from jax import lax
from jax.experimental import pallas as pl
from jax.experimental.pallas import tpu as pltpu

F32 = jnp.float32
BF16 = jnp.bfloat16
EPS = 1e-6
NEG_INIT = -1e30
LOG2E = 1.4426950408889634

GRID_W = 64
ML_HEADS = 4
ML_HEAD_DIM = 128
ML_WIDTH = ML_HEADS * ML_HEAD_DIM
NA_HEADS = 8
NA_HEAD_DIM = 64
NA_WIDTH = NA_HEADS * NA_HEAD_DIM
NA_WIN_ROWS = 8
NA_WIN_COLS = 16
XA_HEADS = 4
XA_HEAD_DIM = 128
XA_WIDTH = XA_HEADS * XA_HEAD_DIM

LANES = 128
F32_ROWS = 8
BF16_ROWS = 16
ML_CHUNK = 256
ML_HEADS_PER_STEP = 4
NA_ROWS_PER_STEP = 16
FFN_UP_ROWS = 1024
FFN_DOWN_ROWS = 512
VMEM_LIMIT = 60 * 1024 * 1024


def _params(dims, vmem=None):
    return pltpu.CompilerParams(dimension_semantics=dims, vmem_limit_bytes=vmem)


def _rms(x, g):
    return x * lax.rsqrt(jnp.mean(x * x, axis=-1, keepdims=True) + EPS) * g


def _wprep_kernel(w_ref, wg_ref, o_ref, og_ref):
    o_ref[...] = w_ref[...].astype(BF16)
    og_ref[...] = wg_ref[...]


def _wprep(w_t, g0, n_gate, tr):
    N, D = w_t.shape
    assert g0 % tr == 0 and g0 % n_gate == 0 and (N - n_gate) % tr == 0
    return pl.pallas_call(
        _wprep_kernel,
        grid=((N - n_gate) // tr,),
        in_specs=[
            pl.BlockSpec((pl.Element(tr), pl.Element(D)),
                         lambda i: (pl.multiple_of(i * tr + jnp.where(i * tr >= g0, n_gate, 0), n_gate), 0)),
            pl.BlockSpec((n_gate, D), lambda i: (g0 // n_gate, 0)),
        ],
        out_specs=[pl.BlockSpec((tr, D), lambda i: (i, 0)),
                   pl.BlockSpec((n_gate, D), lambda i: (0, 0))],
        out_shape=[jax.ShapeDtypeStruct((N - n_gate, D), BF16),
                   jax.ShapeDtypeStruct((n_gate, D), F32)],
        compiler_params=_params(("arbitrary",), VMEM_LIMIT),
        name="wprep",
    )(w_t, w_t)


def _inproj_kernel(x_ref, g_ref, w_ref, wg_ref, *rest):
    n_cast = (len(rest) - 3) // 2
    cast_in, (o_ref, gates_ref), cast_out, h_scr = (
        rest[:n_cast], rest[n_cast:n_cast + 2], rest[n_cast + 2:2 * n_cast + 2], rest[-1])
    for src, dst in zip(cast_in, cast_out):
        dst[...] = src[...].astype(BF16)

    @pl.when(pl.program_id(1) == 0)
    def _():
        hb = _rms(x_ref[...], g_ref[...]).astype(BF16)
        h_scr[...] = hb
        gt = lax.dot_general(wg_ref[...], hb, (((1,), (1,)), ((), ())), preferred_element_type=F32)
        for h in range(ML_HEADS):
            gates_ref[h] = gt[4 * h:4 * h + 4, :]

    o_ref[...] = lax.dot_general(h_scr[...], w_ref[...], (((1,), (1,)), ((), ())),
                                 preferred_element_type=F32).astype(BF16)


def _inproj(x2, g, w_main, w_gate, to_cast, tm, tn):
    T, D = x2.shape
    N = w_main.shape[0]
    ni, nj = T // tm, N // tn
    steps = ni * nj
    assert all(a.shape[0] % (steps * BF16_ROWS) == 0 for a in to_cast)
    slab = lambda a: pl.BlockSpec((a.shape[0] // steps, a.shape[1]), lambda i, j: (i * nj + j, 0))
    return pl.pallas_call(
        _inproj_kernel,
        grid=(ni, nj),
        in_specs=[
            pl.BlockSpec((tm, D), lambda i, j: (i, 0)),
            pl.BlockSpec((1, D), lambda i, j: (0, 0)),
            pl.BlockSpec((tn, D), lambda i, j: (j, 0)),
            pl.BlockSpec(w_gate.shape, lambda i, j: (0, 0)),
        ] + [slab(a) for a in to_cast],
        out_specs=[
            pl.BlockSpec((tm, tn), lambda i, j: (i, j)),
            pl.BlockSpec((ML_HEADS, 4, tm), lambda i, j: (0, 0, i)),
        ] + [slab(a) for a in to_cast],
        out_shape=[
            jax.ShapeDtypeStruct((T, N), BF16),
            jax.ShapeDtypeStruct((ML_HEADS, 4, T), F32),
        ] + [jax.ShapeDtypeStruct(a.shape, BF16) for a in to_cast],
        scratch_shapes=[pltpu.VMEM((tm, D), BF16)],
        compiler_params=_params(("parallel", "arbitrary"), VMEM_LIMIT),
        name="inproj",
    )(x2, g, w_main, w_gate, *to_cast)


def _log_sigmoid(x):
    return jnp.minimum(x, 0.0) - jnp.log(1.0 + jnp.exp(-jnp.abs(x)))


def _chunk_scan(x, pos, L, reverse, op, identity):
    n = x.shape[1]
    d = 1
    while d < L:
        if reverse:
            x = op(x, jnp.where(pos < L - d, pltpu.roll(x, n - d, 1), identity))
        else:
            x = op(x, jnp.where(pos >= d, pltpu.roll(x, d, 1), identity))
        d *= 2
    return x


def _mlstm_kernel(q_ref, k_ref, v_ref, o_ref, gr_ref, br_ref,
                  wq_ref, wk_ref, bq_ref, bk_ref, ng_ref, y_ref,
                  qt_scr, k_scr, vat_scr, bb_scr, hf_scr, hb_scr):
    S = q_ref.shape[1]
    L = ML_CHUNK
    NC = S // L
    Dh = ML_HEAD_DIM
    s_idx = lax.broadcasted_iota(jnp.int32, (L, L), 0)
    j_idx = lax.broadcasted_iota(jnp.int32, (L, L), 1)
    pos = lax.broadcasted_iota(jnp.int32, (NC, L), 1)

    def head(hh):
        hs = slice(hh * Dh, (hh + 1) * Dh)
        gates = []
        for d, rev in ((0, False), (1, True)):
            i_pre = gr_ref[hh, 2 * d] + br_ref[hh, 2 * d:2 * d + 1, :]
            lf = _log_sigmoid(gr_ref[hh, 2 * d + 1] + br_ref[hh, 2 * d + 1:2 * d + 2, :])
            a = _chunk_scan(lf, pos, L, rev, jnp.add, 0.0)
            b = i_pre - a
            gates.append((b, a, _chunk_scan(b, pos, L, rev, jnp.maximum, -jnp.inf), lf))

        def conv_silu(x_ref, w_ref, b_ref, c):
            x = x_ref[0, c * L:(c + 1) * L, hs].astype(F32)
            zero_row = jnp.zeros((1, Dh), F32)
            prev_row = (x_ref[0, c * L - BF16_ROWS:c * L, hs].astype(F32)[BF16_ROWS - 1:, :]
                        if c > 0 else zero_row)
            next_row = (x_ref[0, (c + 1) * L:(c + 1) * L + BF16_ROWS, hs].astype(F32)[0:1, :]
                        if c < NC - 1 else zero_row)
            tile = F32_ROWS
            row = lax.broadcasted_iota(jnp.int32, (tile, Dh), 0)
            x_prev = pltpu.roll(x, 1, 0)
            x_prev = jnp.concatenate([jnp.where(row == 0, prev_row, x_prev[:tile]), x_prev[tile:]], axis=0)
            x_next = pltpu.roll(x, L - 1, 0)
            x_next = jnp.concatenate(
                [x_next[:L - tile], jnp.where(row == tile - 1, next_row, x_next[L - tile:])], axis=0)
            w = w_ref[:, hs]
            y = x_prev * w[0:1] + x * w[1:2] + x_next * w[2:3] + b_ref[:, hs]
            return y / (1.0 + jnp.exp2(y * (-LOG2E)))

        vat_scr[hh, Dh:, :] = jnp.ones((Dh, S), BF16)

        def prep(c):
            cs = slice(c * L, (c + 1) * L)
            qt_scr[hh, :, cs] = (conv_silu(q_ref, wq_ref, bq_ref, c) * (Dh ** -0.5)).astype(BF16).T
            k_scr[hh, cs, :] = conv_silu(k_ref, wk_ref, bk_ref, c).astype(BF16)
            vat_scr[hh, :Dh, cs] = v_ref[0, cs, hs].T
            for d in range(2):
                bb_scr[hh, d, cs, :] = jnp.broadcast_to(gates[d][0][c:c + 1, :] * LOG2E, (LANES, L)).T

        def chunk(c, m, st_t, fwd):
            cs = slice(c * L, (c + 1) * L)
            d = 0 if fwd else 1
            h_scr = hf_scr if fwd else hb_scr
            b_row, a_row, bmax_row, lf_row = (t[c:c + 1, :] for t in gates[d])
            mask_t = (s_idx <= j_idx) if fwd else (s_idx >= j_idx)
            g = jnp.sum(lf_row, axis=1, keepdims=True)
            w_end = g + b_row
            m_new = jnp.maximum(g + m, jnp.max(w_end, axis=1, keepdims=True))
            decay = jnp.exp(g + m - m_new)
            wt = jnp.exp(w_end - m_new)
            c_row = jnp.maximum(m, bmax_row)
            inter_w = jnp.exp(m - c_row)
            floor = jnp.exp(-(a_row + c_row))
            k = k_scr[hh, cs, :]
            q_t = qt_scr[hh, :, cs]
            va_t = vat_scr[hh, :, cs]
            qk_t = jnp.dot(k, q_t, preferred_element_type=F32)
            b_col = bb_scr[hh, d, cs, :]
            expo = jnp.concatenate([b_col] * (L // LANES), axis=1) - c_row * LOG2E
            p_t = jnp.exp2(jnp.where(mask_t, expo, -jnp.inf)) * qk_t
            q_in = q_t * inter_w.astype(BF16)
            nd_t = (jnp.dot(va_t, p_t.astype(BF16), preferred_element_type=F32)
                    + jnp.dot(st_t.astype(BF16), q_in, preferred_element_type=F32))
            h_scr[hh, :, cs] = nd_t[:Dh] / jnp.maximum(jnp.abs(nd_t[Dh:]), floor)
            wt_b = wt.astype(BF16)
            vw = jnp.concatenate([va_t[:Dh] * wt_b, jnp.broadcast_to(wt_b, (Dh, L))], axis=0)
            return m_new, decay * st_t + jnp.dot(vw, k, preferred_element_type=F32)

        def finish(c):
            cs = slice(c * L, (c + 1) * L)
            h_t = hf_scr[hh, :, cs] + hb_scr[hh, :, cs]
            h_t = h_t * lax.rsqrt(jnp.mean(h_t * h_t, axis=0, keepdims=True) + EPS)
            y = h_t.T * ng_ref[:, hs] * jax.nn.sigmoid(o_ref[0, cs, hs].astype(F32))
            y_ref[0, cs, hs] = y.astype(BF16)

        return prep, chunk, finish

    heads = [head(hh) for hh in range(ML_HEADS_PER_STEP)]
    m0 = jnp.full((1, 1), NEG_INIT, F32)
    st0 = jnp.zeros((2 * Dh, Dh), F32)
    state = [(m0, st0, m0, st0) for _ in heads]
    for prep, _, _ in heads:
        prep(0)
        prep(NC - 1)
    for c in range(NC):
        for prep, _, _ in heads:
            if c + 1 < NC - 1 - (c + 1):
                prep(c + 1)
                prep(NC - 2 - c)
            elif c + 1 == NC - 1 - (c + 1):
                prep(c + 1)
        for n, (_, chunk, _) in enumerate(heads):
            m_f, st_f, m_b, st_b = state[n]
            m_f, st_f = chunk(c, m_f, st_f, True)
            m_b, st_b = chunk(NC - 1 - c, m_b, st_b, False)
            state[n] = (m_f, st_f, m_b, st_b)
        for _, _, finish in heads:
            if c >= NC - 1 - c:
                finish(c)
                if c != NC - 1 - c:
                    finish(NC - 1 - c)


def _mlstm(proj3, grow, brow, w_conv, b_conv, norm_g):
    B, S, _ = proj3.shape
    H, Dh, G = ML_HEADS, ML_HEAD_DIM, ML_HEADS_PER_STEP
    assert H % G == 0
    ng = H // G
    blk = lambda part: pl.BlockSpec((1, S, G * Dh), lambda b, h: (b, 0, part * ng + h))
    par = lambda rows, part: pl.BlockSpec((rows, G * Dh), lambda b, h: (0, part * ng + h))
    head_buf = lambda shape, dt: pltpu.VMEM((G,) + shape, dt)
    return pl.pallas_call(
        _mlstm_kernel,
        grid=(B, ng),
        in_specs=[
            blk(0), blk(1), blk(2), blk(3),
            pl.BlockSpec((G, 4, S // ML_CHUNK, ML_CHUNK), lambda b, h: (h, 0, b, 0)),
            pl.BlockSpec((G, 4, 1), lambda b, h: (h, 0, 0)),
            par(3, 0), par(3, 1), par(1, 0), par(1, 1), par(1, 0),
        ],
        out_specs=pl.BlockSpec((1, S, G * Dh), lambda b, h: (b, 0, h)),
        out_shape=jax.ShapeDtypeStruct((B, S, ML_WIDTH), BF16),
        scratch_shapes=[
            head_buf((Dh, S), BF16), head_buf((S, Dh), BF16), head_buf((2 * Dh, S), BF16),
            head_buf((2, S, LANES), F32),
            head_buf((Dh, S), F32), head_buf((Dh, S), F32),
        ],
        compiler_params=_params(("parallel", "parallel"), VMEM_LIMIT),
        name="mlstm",
    )(proj3, proj3, proj3, proj3, grow, brow, w_conv, w_conv, b_conv, b_conv, norm_g)


def _na_kernel(q_ref, k_ref, v_ref, pat_ref, wup_ref, wdn_ref, y_ref, wup_o_ref, wdn_o_ref, t_scr):
    wup_o_ref[...] = wup_ref[...].astype(BF16)
    wdn_o_ref[...] = wdn_ref[...].astype(BF16)

    S = k_ref.shape[1]
    rows = S // GRID_W
    W = GRID_W
    n_pairs = NA_HEADS // 2
    n_dr = t_scr.shape[0]
    r0 = pl.program_id(1) * NA_ROWS_PER_STEP
    low = lax.broadcasted_iota(jnp.int32, (W, LANES), 1) < NA_HEAD_DIM

    @pl.when((pl.program_id(0) == 0) & (pl.program_id(1) == 0))
    def _():
        cq = lax.broadcasted_iota(jnp.int32, (W, LANES), 0)
        ck = lax.broadcasted_iota(jnp.int32, (W, LANES), 1) % W
        col_start = jnp.clip(cq - NA_WIN_COLS // 2, 0, W - NA_WIN_COLS)
        col_ok = (ck >= col_start) & (ck < col_start + NA_WIN_COLS)
        for d in range(n_dr):
            for h in range(NA_HEADS):
                pattern = jnp.broadcast_to(pat_ref[h, d:d + 1, :], (W, LANES))
                skewed = pltpu.roll(pattern, 0, 1, stride=1, stride_axis=0)
                t_scr[d, h // 2, (h % 2) * W:(h % 2 + 1) * W, :] = jnp.where(col_ok, skewed, -jnp.inf)

    ones = jnp.ones((NA_WIN_ROWS * W, LANES), BF16)

    def row_body(i, _):
        r = r0 + i
        row_start = jnp.clip(r - NA_WIN_ROWS // 2, 0, rows - NA_WIN_ROWS)
        dr0 = row_start - r + (NA_WIN_ROWS - 1)
        band = pl.ds(pl.multiple_of(row_start * W, W), NA_WIN_ROWS * W)
        qrow = pl.ds(pl.multiple_of(i * W, W), W)
        scores = []
        for hp in range(n_pairs):
            cols = slice(hp * LANES, (hp + 1) * LANES)
            q2 = q_ref[0, qrow, cols] * (NA_HEAD_DIM ** -0.5)
            zero = jnp.zeros_like(q2)
            qs = jnp.concatenate([jnp.where(low, q2, zero), jnp.where(low, zero, q2)], axis=0)
            s = lax.dot_general(qs, k_ref[0, band, cols], (((1,), (1,)), ((), ())),
                                preferred_element_type=F32)
            bias = jnp.concatenate([t_scr[dr0 + kk, hp] for kk in range(0, NA_WIN_ROWS, 2)], axis=-1)
            scores.append(s + bias)
        probs = [jnp.exp(s - jnp.max(s, axis=-1, keepdims=True)).astype(BF16) for s in scores]
        for hp in range(n_pairs):
            cols = slice(hp * LANES, (hp + 1) * LANES)
            v_aug = jnp.concatenate([v_ref[0, band, cols], ones], axis=1)
            o = jnp.dot(probs[hp], v_aug, preferred_element_type=F32)
            o = o[:, :LANES] / o[:, LANES:]
            y_ref[0, qrow, cols] = jnp.where(low, o[:W], o[W:]).astype(BF16)
        return 0

    lax.fori_loop(0, NA_ROWS_PER_STEP, row_body, 0, unroll=8)


def _na(proj3, patterns, w_up, w_down, q_blk, k_blk, v_blk):
    B, S, _ = proj3.shape
    rows = S // GRID_W
    rb = NA_ROWS_PER_STEP
    n_dr = patterns.shape[1]
    nr = rows // rb
    steps = B * nr
    assert w_up.shape[0] % (steps * BF16_ROWS) == 0 and w_down.shape[0] % (steps * BF16_ROWS) == 0
    slab = lambda a: pl.BlockSpec((a.shape[0] // steps, a.shape[1]), lambda b, r: (b * nr + r, 0))
    return pl.pallas_call(
        _na_kernel,
        grid=(B, nr),
        in_specs=[
            pl.BlockSpec((1, rb * GRID_W, NA_WIDTH), lambda b, r: (b, r, q_blk)),
            pl.BlockSpec((1, S, NA_WIDTH), lambda b, r: (b, 0, k_blk)),
            pl.BlockSpec((1, S, NA_WIDTH), lambda b, r: (b, 0, v_blk)),
            pl.BlockSpec(patterns.shape, lambda b, r: (0, 0, 0)),
            slab(w_up), slab(w_down),
        ],
        out_specs=[pl.BlockSpec((1, rb * GRID_W, NA_WIDTH), lambda b, r: (b, r, 0)), slab(w_up), slab(w_down)],
        out_shape=[jax.ShapeDtypeStruct((B, S, NA_WIDTH), BF16),
                   jax.ShapeDtypeStruct(w_up.shape, BF16), jax.ShapeDtypeStruct(w_down.shape, BF16)],
        scratch_shapes=[pltpu.VMEM((n_dr, NA_HEADS // 2, 2 * GRID_W, LANES), F32)],
        compiler_params=_params(("arbitrary", "arbitrary"), VMEM_LIMIT),
        name="natten",
    )(proj3, proj3, proj3, patterns, w_up, w_down)


def _na_bias_patterns(rpb):
    c0 = NA_WIN_COLS - 1
    r = rpb.astype(F32)
    gap = jnp.zeros(r.shape[:1] + (r.shape[1] - 1, LANES // 2 - NA_WIN_COLS - c0), F32)
    return jnp.concatenate([r[:, :-1, c0:], gap, r[:, 1:, :], gap, r[:, :-1, :c0]], axis=-1)


def _memkv_kernel(m_ref, g_ref, w_ref, o_ref):
    hb = _rms(m_ref[...], g_ref[...]).astype(BF16)
    o_ref[...] = jnp.dot(hb, w_ref[...], preferred_element_type=F32).astype(BF16)


def _memkv(mem2, g, w, tm):
    T, D = mem2.shape
    N = w.shape[1]
    return pl.pallas_call(
        _memkv_kernel,
        grid=(T // tm,),
        in_specs=[
            pl.BlockSpec((tm, D), lambda i: (i, 0)),
            pl.BlockSpec((1, D), lambda i: (0, 0)),
            pl.BlockSpec((D, N), lambda i: (0, 0)),
        ],
        out_specs=pl.BlockSpec((tm, N), lambda i: (i, 0)),
        out_shape=jax.ShapeDtypeStruct((T, N), BF16),
        compiler_params=_params(("parallel",), VMEM_LIMIT),
        name="memkv",
    )(mem2, g, w)


def _xa_kernel(q_ref, kv_ref, y_ref):
    Dh = XA_HEAD_DIM
    for h in range(XA_HEADS):
        q = q_ref[0, :, h * Dh:(h + 1) * Dh]
        k = kv_ref[0, :, h * Dh:(h + 1) * Dh]
        v = kv_ref[0, :, XA_WIDTH + h * Dh:XA_WIDTH + (h + 1) * Dh]
        s = lax.dot_general(q, k, (((1,), (1,)), ((), ())), preferred_element_type=F32) * (Dh ** -0.5)
        p = jnp.exp(s - jnp.max(s, axis=-1, keepdims=True))
        o = jnp.dot(p.astype(BF16), v, preferred_element_type=F32)
        y_ref[0, :, h * Dh:(h + 1) * Dh] = (o / jnp.sum(p, axis=-1, keepdims=True)).astype(BF16)


def _xa(proj3, kv3, q_blk, tq):
    B, S, _ = proj3.shape
    M = kv3.shape[1]
    return pl.pallas_call(
        _xa_kernel,
        grid=(B, S // tq),
        in_specs=[
            pl.BlockSpec((1, tq, XA_WIDTH), lambda b, i: (b, i, q_blk)),
            pl.BlockSpec((1, M, 2 * XA_WIDTH), lambda b, i: (b, 0, 0)),
        ],
        out_specs=pl.BlockSpec((1, tq, XA_WIDTH), lambda b, i: (b, i, 0)),
        out_shape=jax.ShapeDtypeStruct((B, S, XA_WIDTH), BF16),
        compiler_params=_params(("parallel", "parallel"), VMEM_LIMIT),
        name="memxattn",
    )(proj3, kv3)


def _merge_kernel(x_ref, yml_ref, yna_ref, yxa_ref, p0_ref, p1_ref, p2_ref, bg_ref,
                  wml_ref, wna_ref, wxa_ref, wo_ref, o_ref):
    D = x_ref.shape[1]
    merged = None
    for n, (y_ref, p_ref, w_ref) in enumerate(
            ((yml_ref, p0_ref, wml_ref), (yna_ref, p1_ref, wna_ref), (yxa_ref, p2_ref, wxa_ref))):
        gate = 0.5 + 0.5 * jnp.tanh(0.5 * (p_ref[...].astype(F32) + bg_ref[:, n * D:(n + 1) * D]))
        term = gate * jnp.dot(y_ref[...], w_ref[...], preferred_element_type=F32)
        merged = term if merged is None else merged + term
    o_ref[...] = x_ref[...] + jnp.dot(merged.astype(BF16), wo_ref[...], preferred_element_type=F32)


def _merge(x2, yml, yna, yxa, proj2, b_gate, wml, wna, wxa, wo, gate_blk, tm):
    T, D = x2.shape
    row = lambda w: pl.BlockSpec((tm, w), lambda i: (i, 0))
    full = lambda a: pl.BlockSpec(a.shape, lambda i: (0, 0))
    return pl.pallas_call(
        _merge_kernel,
        grid=(T // tm,),
        in_specs=[
            row(D), row(ML_WIDTH), row(NA_WIDTH), row(XA_WIDTH),
            pl.BlockSpec((tm, D), lambda i: (i, gate_blk)),
            pl.BlockSpec((tm, D), lambda i: (i, gate_blk + 1)),
            pl.BlockSpec((tm, D), lambda i: (i, gate_blk + 2)),
            full(b_gate), full(wml), full(wna), full(wxa), full(wo),
        ],
        out_specs=row(D),
        out_shape=jax.ShapeDtypeStruct((T, D), F32),
        compiler_params=_params(("parallel",), VMEM_LIMIT),
        name="merge",
    )(x2, yml, yna, yxa, proj2, proj2, proj2, b_gate, wml, wna, wxa, wo)


def _gelu_tanh(x):
    return 0.5 * x * (1.0 + jnp.tanh(0.7978845608028654 * (x + 0.044715 * (x * x * x))))


def _ffn_kernel(x_ref, g_ref, wa0_ref, wa1_ref, wu0_ref, wu1_ref, wc0_ref, wc1_ref, bc0_ref, bc1_ref,
                wd0_ref, wd1_ref, gf_ref, o_ref, h_scr, a0_scr, u0_scr, a1_scr, u1_scr):
    t = pl.program_id(1)
    last = pl.num_programs(1) - 1
    S = x_ref.shape[1]

    def produce(wa_ref, wu_ref, a_scr, u_scr):
        for r in range(0, S, FFN_UP_ROWS):
            h = h_scr[r:r + FFN_UP_ROWS, :]
            a_scr[r:r + FFN_UP_ROWS, :] = jnp.dot(h, wa_ref[...], preferred_element_type=F32)
            u_scr[r:r + FFN_UP_ROWS, :] = jnp.dot(h, wu_ref[...], preferred_element_type=F32)

    def consume(a_scr, u_scr, wc_ref, bc_ref, wd_ref):
        a = a_scr[...]
        row = lax.broadcasted_iota(jnp.int32, a.shape, 0)
        a_prev = jnp.where(row == 0, 0.0, pltpu.roll(a, 1, 0))
        a_next = jnp.where(row == S - 1, 0.0, pltpu.roll(a, S - 1, 0))
        w = wc_ref[...]
        bc = bc_ref[...]
        for r in range(0, S, FFN_DOWN_ROWS):
            rows = slice(r, r + FFN_DOWN_ROWS)
            conv = a_prev[rows] * w[0:1] + a[rows] * w[1:2] + a_next[rows] * w[2:3] + bc
            act = (_gelu_tanh(conv) * u_scr[rows, :]).astype(BF16)
            o_ref[0, rows, :] += jnp.dot(act, wd_ref[...], preferred_element_type=F32)

    @pl.when(t == 0)
    def _():
        x = x_ref[0]
        h_scr[...] = _rms(x, g_ref[...]).astype(BF16)
        o_ref[0] = x
        produce(wa0_ref, wu0_ref, a0_scr, u0_scr)
        produce(wa1_ref, wu1_ref, a1_scr, u1_scr)
        consume(a0_scr, u0_scr, wc1_ref, bc1_ref, wd1_ref)

    @pl.when((t > 0) & (t < last))
    def _():
        produce(wa0_ref, wu0_ref, a0_scr, u0_scr)
        consume(a1_scr, u1_scr, wc0_ref, bc0_ref, wd0_ref)
        produce(wa1_ref, wu1_ref, a1_scr, u1_scr)
        consume(a0_scr, u0_scr, wc1_ref, bc1_ref, wd1_ref)

    @pl.when(t == last)
    def _():
        produce(wa0_ref, wu0_ref, a0_scr, u0_scr)
        consume(a1_scr, u1_scr, wc0_ref, bc0_ref, wd0_ref)
        consume(a0_scr, u0_scr, wc1_ref, bc1_ref, wd1_ref)
        o_ref[0] = _rms(o_ref[0], gf_ref[...])


def _ffn(x3, g, w_up, w_conv, b_conv, w_down, g_final, tf):
    B, S, D = x3.shape
    FF = w_down.shape[0]
    nf = FF // tf
    assert nf % 2 == 1, "the two-chunk pipeline ends on a produce-one / consume-two step"
    steps = (nf + 1) // 2
    prod0 = lambda t: 2 * t
    prod1 = lambda t: jnp.minimum(2 * t + 1, nf - 1)
    cons0 = lambda t: jnp.maximum(2 * t - 1, 0)
    cons1 = lambda t: 2 * t
    col = lambda f, off=0: pl.BlockSpec((D, tf), lambda b, t: (0, off + f(t)))
    vec = lambda rows, f: pl.BlockSpec((rows, tf), lambda b, t: (0, f(t)))
    dn = lambda f: pl.BlockSpec((tf, D), lambda b, t: (f(t), 0))
    buf = pltpu.VMEM((S, tf), F32)
    return pl.pallas_call(
        _ffn_kernel,
        grid=(B, steps),
        in_specs=[
            pl.BlockSpec((1, S, D), lambda b, t: (b, 0, 0)),
            pl.BlockSpec((1, D), lambda b, t: (0, 0)),
            col(prod0), col(prod1), col(prod0, nf), col(prod1, nf),
            vec(3, cons0), vec(3, cons1), vec(1, cons0), vec(1, cons1),
            dn(cons0), dn(cons1),
            pl.BlockSpec((1, D), lambda b, t: (0, 0)),
        ],
        out_specs=pl.BlockSpec((1, S, D), lambda b, t: (b, 0, 0)),
        out_shape=jax.ShapeDtypeStruct((B, S, D), F32),
        scratch_shapes=[pltpu.VMEM((S, D), BF16), buf, buf, buf, buf],
        compiler_params=_params(("parallel", "arbitrary"), VMEM_LIMIT),
        name="ffn",
    )(x3, g, w_up, w_up, w_up, w_up, w_conv, w_conv, b_conv, b_conv, w_down, w_down, g_final)


def kernel(x, mem, mix_norm_g, w_in, b_ml_igate, b_ml_fgate, w_ml_conv, b_ml_conv, ml_norm_g, na_rpb, mem_norm_g, w_mem_kv, b_merge_gate, w_br_ml, w_br_na, w_br_xa, w_out, ffn_norm_g, w_ffn_up, w_ffn_conv, b_ffn_conv, w_ffn_down, final_norm_g):
    B, S, D = x.shape
    H = ML_HEADS
    M = mem.shape[1]
    T = B * S
    assert w_in.shape[0] == 1, "single-layer block: the FFN kernel also applies the final norm"
    assert S % ML_CHUNK == 0 and S % (GRID_W * NA_ROWS_PER_STEP) == 0
    l = 0
    n_gate = 4 * H
    g0 = 4 * ML_WIDTH
    row2 = lambda v: v.reshape(1, -1).astype(F32)

    w_main, w_gate = _wprep(jnp.transpose(w_in[l]), g0, n_gate, tr=1024)
    w_gate = jnp.transpose(w_gate.reshape(4, H, D), (1, 0, 2)).reshape(n_gate, D).astype(BF16)

    proj2, grow, w_kv_b, w_ml_b, w_na_b, w_xa_b, w_out_b = _inproj(
        x.reshape(T, D), row2(mix_norm_g[l]), w_main, w_gate,
        [w_mem_kv[l], w_br_ml[l], w_br_na[l], w_br_xa[l], w_out[l]], tm=1024, tn=3584)
    proj3 = proj2.reshape(B, S, -1)

    bias = jnp.stack([b_ml_igate[l][0], b_ml_fgate[l][0], b_ml_igate[l][1], b_ml_fgate[l][1]],
                     axis=-1).astype(F32)
    y_ml = _mlstm(proj3, grow.reshape(H, 4, T // ML_CHUNK, ML_CHUNK), bias[:, :, None],
                  w_ml_conv[l].astype(F32), row2(b_ml_conv[l]), row2(ml_norm_g[l]))

    nb = (4 * ML_WIDTH) // NA_WIDTH
    y_na, w_up_b, w_down_b = _na(proj3, _na_bias_patterns(na_rpb[l]), w_ffn_up[l], w_ffn_down[l],
                                 nb, nb + 1, nb + 2)

    kv = _memkv(mem.reshape(B * M, D), row2(mem_norm_g[l]), w_kv_b, tm=512)
    xb = (4 * ML_WIDTH + 3 * NA_WIDTH) // XA_WIDTH
    y_xa = _xa(proj3, kv.reshape(B, M, -1), xb, tq=S)

    gate_blk = (4 * ML_WIDTH + 3 * NA_WIDTH + XA_WIDTH) // D
    x1 = _merge(x.reshape(T, D), y_ml.reshape(T, -1), y_na.reshape(T, -1), y_xa.reshape(T, -1),
                proj2, row2(b_merge_gate[l]), w_ml_b, w_na_b, w_xa_b, w_out_b, gate_blk, tm=1024)

    return _ffn(x1.reshape(B, S, D), row2(ffn_norm_g[l]), w_up_b,
                w_ffn_conv[l].astype(F32), row2(b_ffn_conv[l]), w_down_b,
                row2(final_norm_g), tf=256)
```

```python
import math

import jax
import jax.numpy as jnp
from jax import lax
from jax.experimental import pallas as pl
from jax.experimental.pallas import tpu as pltpu

F32 = jnp.float32
BF16 = jnp.bfloat16
EPS = 1e-6
NEG_INIT = -1e30
LOG2E = 1.4426950408889634

GRID_W = 64
ML_HEADS = 4
ML_HEAD_DIM = 128
ML_WIDTH = ML_HEADS * ML_HEAD_DIM
LOG2_Q_SCALE = -0.5 * math.log2(ML_HEAD_DIM)
NA_HEADS = 8
NA_HEAD_DIM = 64
NA_WIDTH = NA_HEADS * NA_HEAD_DIM
NA_WIN_ROWS = 8
NA_WIN_COLS = 16
XA_HEADS = 4
XA_HEAD_DIM = 128
XA_WIDTH = XA_HEADS * XA_HEAD_DIM

LANES = 128
F32_ROWS = 8
BF16_ROWS = 16
ML_CHUNK = 256
ML_HEADS_PER_STEP = 2
NA_ROWS_PER_STEP = 16
FFN_UP_ROWS = 512
FFN_DOWN_ROWS = 512
VMEM_LIMIT = 60 * 1024 * 1024


def _params(dims, vmem=None):
    return pltpu.CompilerParams(dimension_semantics=dims, vmem_limit_bytes=vmem)


def _rms(x, g):
    return x * lax.rsqrt(jnp.mean(x * x, axis=-1, keepdims=True) + EPS) * g


def _wprep_kernel(w_ref, wg_ref, o_ref, og_ref):
    o_ref[...] = w_ref[...].astype(BF16)
    og_ref[...] = wg_ref[...]


def _wprep(w_t, g0, n_gate, tr):
    N, D = w_t.shape
    assert g0 % tr == 0 and g0 % n_gate == 0 and (N - n_gate) % tr == 0
    return pl.pallas_call(
        _wprep_kernel,
        grid=((N - n_gate) // tr,),
        in_specs=[
            pl.BlockSpec((pl.Element(tr), pl.Element(D)),
                         lambda i: (pl.multiple_of(i * tr + jnp.where(i * tr >= g0, n_gate, 0), n_gate), 0)),
            pl.BlockSpec((n_gate, D), lambda i: (g0 // n_gate, 0)),
        ],
        out_specs=[pl.BlockSpec((tr, D), lambda i: (i, 0)),
                   pl.BlockSpec((n_gate, D), lambda i: (0, 0))],
        out_shape=[jax.ShapeDtypeStruct((N - n_gate, D), BF16),
                   jax.ShapeDtypeStruct((n_gate, D), F32)],
        compiler_params=_params(("arbitrary",), VMEM_LIMIT),
        name="wprep",
    )(w_t, w_t)


def _inproj_kernel(x_ref, g_ref, w_ref, wg_ref, *rest):
    n_cast = (len(rest) - 3) // 2
    cast_in, (o_ref, gates_ref), cast_out, h_scr = (
        rest[:n_cast], rest[n_cast:n_cast + 2], rest[n_cast + 2:2 * n_cast + 2], rest[-1])
    for src, dst in zip(cast_in, cast_out):
        dst[...] = src[...].astype(BF16)

    @pl.when(pl.program_id(1) == 0)
    def _():
        hb = _rms(x_ref[...], g_ref[...]).astype(BF16)
        h_scr[...] = hb
        gt = lax.dot_general(wg_ref[...], hb, (((1,), (1,)), ((), ())), preferred_element_type=F32)
        for h in range(ML_HEADS):
            gates_ref[h] = gt[4 * h:4 * h + 4, :]

    o_ref[...] = lax.dot_general(h_scr[...], w_ref[...], (((1,), (1,)), ((), ())),
                                 preferred_element_type=F32).astype(BF16)


def _inproj(x2, g, w_main, w_gate, to_cast, tm, tn):
    T, D = x2.shape
    N = w_main.shape[0]
    ni, nj = T // tm, N // tn
    steps = ni * nj
    assert all(a.shape[0] % (steps * BF16_ROWS) == 0 for a in to_cast)
    slab = lambda a: pl.BlockSpec((a.shape[0] // steps, a.shape[1]), lambda i, j: (i * nj + j, 0))
    return pl.pallas_call(
        _inproj_kernel,
        grid=(ni, nj),
        in_specs=[
            pl.BlockSpec((tm, D), lambda i, j: (i, 0)),
            pl.BlockSpec((1, D), lambda i, j: (0, 0)),
            pl.BlockSpec((tn, D), lambda i, j: (j, 0)),
            pl.BlockSpec(w_gate.shape, lambda i, j: (0, 0)),
        ] + [slab(a) for a in to_cast],
        out_specs=[
            pl.BlockSpec((tm, tn), lambda i, j: (i, j)),
            pl.BlockSpec((ML_HEADS, 4, tm), lambda i, j: (0, 0, i)),
        ] + [slab(a) for a in to_cast],
        out_shape=[
            jax.ShapeDtypeStruct((T, N), BF16),
            jax.ShapeDtypeStruct((ML_HEADS, 4, T), F32),
        ] + [jax.ShapeDtypeStruct(a.shape, BF16) for a in to_cast],
        scratch_shapes=[pltpu.VMEM((tm, D), BF16)],
        compiler_params=_params(("parallel", "arbitrary"), VMEM_LIMIT),
        name="inproj",
    )(x2, g, w_main, w_gate, *to_cast)


def _log_sigmoid(x):
    return jnp.minimum(x, 0.0) - jnp.log(1.0 + jnp.exp(-jnp.abs(x)))


def _chunk_scan(x, pos, L, reverse, op, identity):
    n = x.shape[1]
    d = 1
    while d < L:
        if reverse:
            x = op(x, jnp.where(pos < L - d, pltpu.roll(x, n - d, 1), identity))
        else:
            x = op(x, jnp.where(pos >= d, pltpu.roll(x, d, 1), identity))
        d *= 2
    return x


def _mlstm_kernel(q_ref, k_ref, v_ref, o_ref, gr_ref, br_ref,
                  wq_ref, wk_ref, bq_ref, bk_ref, ng_ref, y_ref,
                  qt_scr, k_scr, vat_scr, bb_scr, hf_scr, hb_scr):
    S = q_ref.shape[1]
    L = ML_CHUNK
    NC = S // L
    Dh = ML_HEAD_DIM
    s_idx = lax.broadcasted_iota(jnp.int32, (L, L), 0)
    j_idx = lax.broadcasted_iota(jnp.int32, (L, L), 1)
    pos = lax.broadcasted_iota(jnp.int32, (NC, L), 1)

    def head(hh):
        hs = slice(hh * Dh, (hh + 1) * Dh)
        gates = []
        for d, rev in ((0, False), (1, True)):
            i_pre = gr_ref[hh, 2 * d] + br_ref[hh, 2 * d:2 * d + 1, :]
            lf = _log_sigmoid(gr_ref[hh, 2 * d + 1] + br_ref[hh, 2 * d + 1:2 * d + 2, :])
            a = _chunk_scan(lf, pos, L, rev, jnp.add, 0.0)
            b = i_pre - a
            gates.append((b, a, _chunk_scan(b, pos, L, rev, jnp.maximum, -jnp.inf), lf))

        def conv_silu(x_ref, w_ref, b_ref, c):
            x = x_ref[0, c * L:(c + 1) * L, hs].astype(F32)
            zero_row = jnp.zeros((1, Dh), F32)
            prev_row = (x_ref[0, c * L - BF16_ROWS:c * L, hs].astype(F32)[BF16_ROWS - 1:, :]
                        if c > 0 else zero_row)
            next_row = (x_ref[0, (c + 1) * L:(c + 1) * L + BF16_ROWS, hs].astype(F32)[0:1, :]
                        if c < NC - 1 else zero_row)
            tile = F32_ROWS
            row = lax.broadcasted_iota(jnp.int32, (tile, Dh), 0)
            x_prev = pltpu.roll(x, 1, 0)
            x_prev = jnp.concatenate([jnp.where(row == 0, prev_row, x_prev[:tile]), x_prev[tile:]], axis=0)
            x_next = pltpu.roll(x, L - 1, 0)
            x_next = jnp.concatenate(
                [x_next[:L - tile], jnp.where(row == tile - 1, next_row, x_next[L - tile:])], axis=0)
            w = w_ref[:, hs]
            y = x_prev * w[0:1] + x * w[1:2] + x_next * w[2:3] + b_ref[:, hs]
            return y / (1.0 + jnp.exp2(y * (-LOG2E)))

        vat_scr[hh, Dh:, :] = jnp.ones((Dh, S), BF16)

        def prep(c):
            cs = slice(c * L, (c + 1) * L)
            qt_scr[hh, :, cs] = conv_silu(q_ref, wq_ref, bq_ref, c).astype(BF16).T
            k_scr[hh, cs, :] = conv_silu(k_ref, wk_ref, bk_ref, c).astype(BF16)
            vat_scr[hh, :Dh, cs] = v_ref[0, cs, hs].T
            for d in range(2):
                bb_scr[hh, d, cs, :] = jnp.broadcast_to(gates[d][0][c:c + 1, :] * LOG2E, (LANES, L)).T

        def chunk(c, m, st_t, fwd):
            cs = slice(c * L, (c + 1) * L)
            d = 0 if fwd else 1
            h_scr = hf_scr if fwd else hb_scr
            b_row, a_row, bmax_row, lf_row = (t[c:c + 1, :] for t in gates[d])
            mask_t = (s_idx <= j_idx) if fwd else (s_idx >= j_idx)
            g = jnp.sum(lf_row, axis=1, keepdims=True)
            w_end = g + b_row
            m_new = jnp.maximum(g + m, jnp.max(w_end, axis=1, keepdims=True))
            decay = jnp.exp(g + m - m_new)
            wt = jnp.exp(w_end - m_new)
            c_row = jnp.maximum(m, bmax_row)
            inter_w = jnp.exp(m - c_row)
            floor = jnp.exp(-(a_row + c_row))
            k = k_scr[hh, cs, :]
            q_t = qt_scr[hh, :, cs]
            va_t = vat_scr[hh, :, cs]
            qk_t = jnp.dot(k, q_t, preferred_element_type=F32)
            b_col = bb_scr[hh, d, cs, :]
            expo = jnp.concatenate([b_col] * (L // LANES), axis=1) - (c_row * LOG2E - LOG2_Q_SCALE)
            p_t = jnp.exp2(jnp.where(mask_t, expo, -jnp.inf)) * qk_t
            q_in = q_t * (inter_w * (Dh ** -0.5)).astype(BF16)
            nd_t = (jnp.dot(va_t, p_t.astype(BF16), preferred_element_type=F32)
                    + jnp.dot(st_t.astype(BF16), q_in, preferred_element_type=F32))
            h_scr[hh, :, cs] = nd_t[:Dh] / jnp.maximum(jnp.abs(nd_t[Dh:]), floor)
            wt_b = wt.astype(BF16)
            vw = jnp.concatenate([va_t[:Dh] * wt_b, jnp.broadcast_to(wt_b, (Dh, L))], axis=0)
            return m_new, decay * st_t + jnp.dot(vw, k, preferred_element_type=F32)

        def finish(c):
            cs = slice(c * L, (c + 1) * L)
            h_t = hf_scr[hh, :, cs] + hb_scr[hh, :, cs]
            h_t = h_t * lax.rsqrt(jnp.mean(h_t * h_t, axis=0, keepdims=True) + EPS)
            o_gate = 1.0 / (1.0 + jnp.exp2(o_ref[0, cs, hs].astype(F32) * (-LOG2E)))
            y = h_t.T * ng_ref[:, hs] * o_gate
            y_ref[0, cs, hs] = y.astype(BF16)

        return prep, chunk, finish

    heads = [head(hh) for hh in range(ML_HEADS_PER_STEP)]
    m0 = jnp.full((1, 1), NEG_INIT, F32)
    st0 = jnp.zeros((2 * Dh, Dh), F32)
    state = [(m0, st0, m0, st0) for _ in heads]
    for prep, _, _ in heads:
        prep(0)
        prep(NC - 1)
    for c in range(NC):
        for prep, _, _ in heads:
            if c + 1 < NC - 1 - (c + 1):
                prep(c + 1)
                prep(NC - 2 - c)
            elif c + 1 == NC - 1 - (c + 1):
                prep(c + 1)
        for n, (_, chunk, _) in enumerate(heads):
            m_f, st_f, m_b, st_b = state[n]
            m_f, st_f = chunk(c, m_f, st_f, True)
            m_b, st_b = chunk(NC - 1 - c, m_b, st_b, False)
            state[n] = (m_f, st_f, m_b, st_b)
        for _, _, finish in heads:
            if c >= NC - 1 - c:
                finish(c)
                if c != NC - 1 - c:
                    finish(NC - 1 - c)


def _mlstm(proj3, grow, brow, w_conv, b_conv, norm_g):
    B, S, _ = proj3.shape
    H, Dh, G = ML_HEADS, ML_HEAD_DIM, ML_HEADS_PER_STEP
    assert H % G == 0
    ng = H // G
    blk = lambda part: pl.BlockSpec((1, S, G * Dh), lambda b, h: (b, 0, part * ng + h))
    par = lambda rows, part: pl.BlockSpec((rows, G * Dh), lambda b, h: (0, part * ng + h))
    head_buf = lambda shape, dt: pltpu.VMEM((G,) + shape, dt)
    return pl.pallas_call(
        _mlstm_kernel,
        grid=(B, ng),
        in_specs=[
            blk(0), blk(1), blk(2), blk(3),
            pl.BlockSpec((G, 4, S // ML_CHUNK, ML_CHUNK), lambda b, h: (h, 0, b, 0)),
            pl.BlockSpec((G, 4, 1), lambda b, h: (h, 0, 0)),
            par(3, 0), par(3, 1), par(1, 0), par(1, 1), par(1, 0),
        ],
        out_specs=pl.BlockSpec((1, S, G * Dh), lambda b, h: (b, 0, h)),
        out_shape=jax.ShapeDtypeStruct((B, S, ML_WIDTH), BF16),
        scratch_shapes=[
            head_buf((Dh, S), BF16), head_buf((S, Dh), BF16), head_buf((2 * Dh, S), BF16),
            head_buf((2, S, LANES), F32),
            head_buf((Dh, S), F32), head_buf((Dh, S), F32),
        ],
        compiler_params=_params(("parallel", "parallel"), VMEM_LIMIT),
        name="mlstm",
    )(proj3, proj3, proj3, proj3, grow, brow, w_conv, w_conv, b_conv, b_conv, norm_g)


def _na_kernel(q_ref, k_ref, v_ref, pat_ref, wup_ref, wdn_ref, y_ref, wup_o_ref, wdn_o_ref, t_scr):
    wup_o_ref[...] = wup_ref[...].astype(BF16)
    wdn_o_ref[...] = wdn_ref[...].astype(BF16)

    S = k_ref.shape[1]
    rows = S // GRID_W
    W = GRID_W
    n_pairs = NA_HEADS // 2
    n_dr = t_scr.shape[0]
    r0 = pl.program_id(1) * NA_ROWS_PER_STEP
    low = lax.broadcasted_iota(jnp.int32, (W, LANES), 1) < NA_HEAD_DIM

    @pl.when((pl.program_id(0) == 0) & (pl.program_id(1) == 0))
    def _():
        cq = lax.broadcasted_iota(jnp.int32, (W, LANES), 0)
        ck = lax.broadcasted_iota(jnp.int32, (W, LANES), 1) % W
        col_start = jnp.clip(cq - NA_WIN_COLS // 2, 0, W - NA_WIN_COLS)
        col_ok = (ck >= col_start) & (ck < col_start + NA_WIN_COLS)
        for d in range(n_dr):
            for h in range(NA_HEADS):
                pattern = jnp.broadcast_to(pat_ref[h, d:d + 1, :], (W, LANES))
                skewed = pltpu.roll(pattern, 0, 1, stride=1, stride_axis=0)
                t_scr[d, h // 2, (h % 2) * W:(h % 2 + 1) * W, :] = jnp.where(col_ok, skewed, -jnp.inf)

    ones = jnp.ones((NA_WIN_ROWS * W, LANES), BF16)

    def row_body(i, _):
        r = r0 + i
        row_start = jnp.clip(r - NA_WIN_ROWS // 2, 0, rows - NA_WIN_ROWS)
        dr0 = row_start - r + (NA_WIN_ROWS - 1)
        band = pl.ds(pl.multiple_of(row_start * W, W), NA_WIN_ROWS * W)
        qrow = pl.ds(pl.multiple_of(i * W, W), W)
        scores = []
        for hp in range(n_pairs):
            cols = slice(hp * LANES, (hp + 1) * LANES)
            q2 = q_ref[0, qrow, cols] * (NA_HEAD_DIM ** -0.5)
            zero = jnp.zeros_like(q2)
            qs = jnp.concatenate([jnp.where(low, q2, zero), jnp.where(low, zero, q2)], axis=0)
            s = lax.dot_general(qs, k_ref[0, band, cols], (((1,), (1,)), ((), ())),
                                preferred_element_type=F32)
            bias = jnp.concatenate([t_scr[dr0 + kk, hp] for kk in range(0, NA_WIN_ROWS, 2)], axis=-1)
            scores.append(s + bias)
        probs = [jnp.exp(s - jnp.max(s, axis=-1, keepdims=True)).astype(BF16) for s in scores]
        for hp in range(n_pairs):
            cols = slice(hp * LANES, (hp + 1) * LANES)
            v_aug = jnp.concatenate([v_ref[0, band, cols], ones], axis=1)
            o = jnp.dot(probs[hp], v_aug, preferred_element_type=F32)
            o = o[:, :LANES] / o[:, LANES:]
            y_ref[0, qrow, cols] = jnp.where(low, o[:W], o[W:]).astype(BF16)
        return 0

    lax.fori_loop(0, NA_ROWS_PER_STEP, row_body, 0, unroll=8)


def _na(proj3, patterns, w_up, w_down, q_blk, k_blk, v_blk):
    B, S, _ = proj3.shape
    rows = S // GRID_W
    rb = NA_ROWS_PER_STEP
    n_dr = patterns.shape[1]
    nr = rows // rb
    steps = B * nr
    assert w_up.shape[0] % (steps * BF16_ROWS) == 0 and w_down.shape[0] % (steps * BF16_ROWS) == 0
    slab = lambda a: pl.BlockSpec((a.shape[0] // steps, a.shape[1]), lambda b, r: (b * nr + r, 0))
    return pl.pallas_call(
        _na_kernel,
        grid=(B, nr),
        in_specs=[
            pl.BlockSpec((1, rb * GRID_W, NA_WIDTH), lambda b, r: (b, r, q_blk)),
            pl.BlockSpec((1, S, NA_WIDTH), lambda b, r: (b, 0, k_blk)),
            pl.BlockSpec((1, S, NA_WIDTH), lambda b, r: (b, 0, v_blk)),
            pl.BlockSpec(patterns.shape, lambda b, r: (0, 0, 0)),
            slab(w_up), slab(w_down),
        ],
        out_specs=[pl.BlockSpec((1, rb * GRID_W, NA_WIDTH), lambda b, r: (b, r, 0)), slab(w_up), slab(w_down)],
        out_shape=[jax.ShapeDtypeStruct((B, S, NA_WIDTH), BF16),
                   jax.ShapeDtypeStruct(w_up.shape, BF16), jax.ShapeDtypeStruct(w_down.shape, BF16)],
        scratch_shapes=[pltpu.VMEM((n_dr, NA_HEADS // 2, 2 * GRID_W, LANES), F32)],
        compiler_params=_params(("arbitrary", "arbitrary"), VMEM_LIMIT),
        name="natten",
    )(proj3, proj3, proj3, patterns, w_up, w_down)


def _na_bias_patterns(rpb):
    c0 = NA_WIN_COLS - 1
    r = rpb.astype(F32)
    gap = jnp.zeros(r.shape[:1] + (r.shape[1] - 1, LANES // 2 - NA_WIN_COLS - c0), F32)
    return jnp.concatenate([r[:, :-1, c0:], gap, r[:, 1:, :], gap, r[:, :-1, :c0]], axis=-1)


def _memkv_kernel(m_ref, g_ref, w_ref, o_ref):
    hb = _rms(m_ref[...], g_ref[...]).astype(BF16)
    o_ref[...] = jnp.dot(hb, w_ref[...], preferred_element_type=F32).astype(BF16)


def _memkv(mem2, g, w, tm):
    T, D = mem2.shape
    N = w.shape[1]
    return pl.pallas_call(
        _memkv_kernel,
        grid=(T // tm,),
        in_specs=[
            pl.BlockSpec((tm, D), lambda i: (i, 0)),
            pl.BlockSpec((1, D), lambda i: (0, 0)),
            pl.BlockSpec((D, N), lambda i: (0, 0)),
        ],
        out_specs=pl.BlockSpec((tm, N), lambda i: (i, 0)),
        out_shape=jax.ShapeDtypeStruct((T, N), BF16),
        compiler_params=_params(("parallel",), VMEM_LIMIT),
        name="memkv",
    )(mem2, g, w)


def _xa_kernel(q_ref, kv_ref, y_ref):
    Dh = XA_HEAD_DIM
    for h in range(XA_HEADS):
        q = q_ref[0, :, h * Dh:(h + 1) * Dh]
        k = kv_ref[0, :, h * Dh:(h + 1) * Dh]
        v = kv_ref[0, :, XA_WIDTH + h * Dh:XA_WIDTH + (h + 1) * Dh]
        s = lax.dot_general(q, k, (((1,), (1,)), ((), ())), preferred_element_type=F32) * (Dh ** -0.5)
        p = jnp.exp(s - jnp.max(s, axis=-1, keepdims=True))
        o = jnp.dot(p.astype(BF16), v, preferred_element_type=F32)
        y_ref[0, :, h * Dh:(h + 1) * Dh] = (o / jnp.sum(p, axis=-1, keepdims=True)).astype(BF16)


def _xa(proj3, kv3, q_blk, tq):
    B, S, _ = proj3.shape
    M = kv3.shape[1]
    return pl.pallas_call(
        _xa_kernel,
        grid=(B, S // tq),
        in_specs=[
            pl.BlockSpec((1, tq, XA_WIDTH), lambda b, i: (b, i, q_blk)),
            pl.BlockSpec((1, M, 2 * XA_WIDTH), lambda b, i: (b, 0, 0)),
        ],
        out_specs=pl.BlockSpec((1, tq, XA_WIDTH), lambda b, i: (b, i, 0)),
        out_shape=jax.ShapeDtypeStruct((B, S, XA_WIDTH), BF16),
        compiler_params=_params(("parallel", "parallel"), VMEM_LIMIT),
        name="memxattn",
    )(proj3, kv3)


def _merge_kernel(x_ref, yml_ref, yna_ref, yxa_ref, p0_ref, p1_ref, p2_ref, bg_ref,
                  wml_ref, wna_ref, wxa_ref, wo_ref, o_ref):
    D = x_ref.shape[1]
    merged = None
    for n, (y_ref, p_ref, w_ref) in enumerate(
            ((yml_ref, p0_ref, wml_ref), (yna_ref, p1_ref, wna_ref), (yxa_ref, p2_ref, wxa_ref))):
        gate = 0.5 + 0.5 * jnp.tanh(0.5 * (p_ref[...].astype(F32) + bg_ref[:, n * D:(n + 1) * D]))
        term = gate * jnp.dot(y_ref[...], w_ref[...], preferred_element_type=F32)
        merged = term if merged is None else merged + term
    o_ref[...] = x_ref[...] + jnp.dot(merged.astype(BF16), wo_ref[...], preferred_element_type=F32)


def _merge(x2, yml, yna, yxa, proj2, b_gate, wml, wna, wxa, wo, gate_blk, tm):
    T, D = x2.shape
    row = lambda w: pl.BlockSpec((tm, w), lambda i: (i, 0))
    full = lambda a: pl.BlockSpec(a.shape, lambda i: (0, 0))
    return pl.pallas_call(
        _merge_kernel,
        grid=(T // tm,),
        in_specs=[
            row(D), row(ML_WIDTH), row(NA_WIDTH), row(XA_WIDTH),
            pl.BlockSpec((tm, D), lambda i: (i, gate_blk)),
            pl.BlockSpec((tm, D), lambda i: (i, gate_blk + 1)),
            pl.BlockSpec((tm, D), lambda i: (i, gate_blk + 2)),
            full(b_gate), full(wml), full(wna), full(wxa), full(wo),
        ],
        out_specs=row(D),
        out_shape=jax.ShapeDtypeStruct((T, D), F32),
        compiler_params=_params(("parallel",), VMEM_LIMIT),
        name="merge",
    )(x2, yml, yna, yxa, proj2, proj2, proj2, b_gate, wml, wna, wxa, wo)


def _gelu_tanh(x):
    return 0.5 * x * (1.0 + jnp.tanh(0.7978845608028654 * (x + 0.044715 * (x * x * x))))


def _ffn_kernel(x_ref, g_ref, wa0_ref, wa1_ref, wu0_ref, wu1_ref, wc0_ref, wc1_ref, bc0_ref, bc1_ref,
                wd0_ref, wd1_ref, gf_ref, o_ref, h_scr, a0_scr, u0_scr, a1_scr, u1_scr):
    t = pl.program_id(1)
    last = pl.num_programs(1) - 1
    S = x_ref.shape[1]

    def produce(wa_ref, wu_ref, a_scr, u_scr):
        for r in range(0, S, FFN_UP_ROWS):
            h = h_scr[r:r + FFN_UP_ROWS, :]
            a_scr[r:r + FFN_UP_ROWS, :] = jnp.dot(h, wa_ref[...], preferred_element_type=F32)
            u_scr[r:r + FFN_UP_ROWS, :] = jnp.dot(h, wu_ref[...], preferred_element_type=F32)

    def consume(a_scr, u_scr, wc_ref, bc_ref, wd_ref, first=False, final=False):
        a = a_scr[...]
        row = lax.broadcasted_iota(jnp.int32, a.shape, 0)
        a_prev = jnp.where(row == 0, 0.0, pltpu.roll(a, 1, 0))
        a_next = jnp.where(row == S - 1, 0.0, pltpu.roll(a, S - 1, 0))
        w = wc_ref[...]
        bc = bc_ref[...]
        for r in range(0, S, FFN_DOWN_ROWS):
            rows = slice(r, r + FFN_DOWN_ROWS)
            conv = a_prev[rows] * w[0:1] + a[rows] * w[1:2] + a_next[rows] * w[2:3] + bc
            act = (_gelu_tanh(conv) * u_scr[rows, :]).astype(BF16)
            base = x_ref[0, rows, :] if first else o_ref[0, rows, :]
            total = base + jnp.dot(act, wd_ref[...], preferred_element_type=F32)
            o_ref[0, rows, :] = _rms(total, gf_ref[...]) if final else total

    @pl.when(t == 0)
    def _():
        h_scr[...] = _rms(x_ref[0], g_ref[...]).astype(BF16)
        produce(wa0_ref, wu0_ref, a0_scr, u0_scr)
        produce(wa1_ref, wu1_ref, a1_scr, u1_scr)
        consume(a0_scr, u0_scr, wc1_ref, bc1_ref, wd1_ref, first=True)

    @pl.when((t > 0) & (t < last))
    def _():
        produce(wa0_ref, wu0_ref, a0_scr, u0_scr)
        consume(a1_scr, u1_scr, wc0_ref, bc0_ref, wd0_ref)
        produce(wa1_ref, wu1_ref, a1_scr, u1_scr)
        consume(a0_scr, u0_scr, wc1_ref, bc1_ref, wd1_ref)

    @pl.when(t == last)
    def _():
        produce(wa0_ref, wu0_ref, a0_scr, u0_scr)
        consume(a1_scr, u1_scr, wc0_ref, bc0_ref, wd0_ref)
        consume(a0_scr, u0_scr, wc1_ref, bc1_ref, wd1_ref, final=True)


def _ffn(x3, g, w_up, w_conv, b_conv, w_down, g_final, tf):
    B, S, D = x3.shape
    FF = w_down.shape[0]
    nf = FF // tf
    assert nf % 2 == 1 and nf >= 3, "the two-chunk pipeline ends on a produce-one / consume-two step"
    steps = (nf + 1) // 2
    prod0 = lambda t: 2 * t
    prod1 = lambda t: jnp.minimum(2 * t + 1, nf - 1)
    cons0 = lambda t: jnp.maximum(2 * t - 1, 0)
    cons1 = lambda t: 2 * t
    col = lambda f, off=0: pl.BlockSpec((D, tf), lambda b, t: (0, off + f(t)))
    vec = lambda rows, f: pl.BlockSpec((rows, tf), lambda b, t: (0, f(t)))
    dn = lambda f: pl.BlockSpec((tf, D), lambda b, t: (f(t), 0))
    buf = pltpu.VMEM((S, tf), F32)
    return pl.pallas_call(
        _ffn_kernel,
        grid=(B, steps),
        in_specs=[
            pl.BlockSpec((1, S, D), lambda b, t: (b, 0, 0)),
            pl.BlockSpec((1, D), lambda b, t: (0, 0)),
            col(prod0), col(prod1), col(prod0, nf), col(prod1, nf),
            vec(3, cons0), vec(3, cons1), vec(1, cons0), vec(1, cons1),
            dn(cons0), dn(cons1),
            pl.BlockSpec((1, D), lambda b, t: (0, 0)),
        ],
        out_specs=pl.BlockSpec((1, S, D), lambda b, t: (b, 0, 0)),
        out_shape=jax.ShapeDtypeStruct((B, S, D), F32),
        scratch_shapes=[pltpu.VMEM((S, D), BF16), buf, buf, buf, buf],
        compiler_params=_params(("parallel", "arbitrary"), VMEM_LIMIT),
        name="ffn",
    )(x3, g, w_up, w_up, w_up, w_up, w_conv, w_conv, b_conv, b_conv, w_down, w_down, g_final)


def kernel(x, mem, mix_norm_g, w_in, b_ml_igate, b_ml_fgate, w_ml_conv, b_ml_conv, ml_norm_g, na_rpb, mem_norm_g, w_mem_kv, b_merge_gate, w_br_ml, w_br_na, w_br_xa, w_out, ffn_norm_g, w_ffn_up, w_ffn_conv, b_ffn_conv, w_ffn_down, final_norm_g):
    B, S, D = x.shape
    H = ML_HEADS
    M = mem.shape[1]
    T = B * S
    assert w_in.shape[0] == 1, "single-layer block: the FFN kernel also applies the final norm"
    assert S % ML_CHUNK == 0 and S % (GRID_W * NA_ROWS_PER_STEP) == 0
    l = 0
    n_gate = 4 * H
    g0 = 4 * ML_WIDTH
    row2 = lambda v: v.reshape(1, -1).astype(F32)

    w_main, w_gate = _wprep(jnp.transpose(w_in[l]), g0, n_gate, tr=1024)
    w_gate = jnp.transpose(w_gate.reshape(4, H, D), (1, 0, 2)).reshape(n_gate, D).astype(BF16)

    proj2, grow, w_kv_b, w_ml_b, w_na_b, w_xa_b, w_out_b = _inproj(
        x.reshape(T, D), row2(mix_norm_g[l]), w_main, w_gate,
        [w_mem_kv[l], w_br_ml[l], w_br_na[l], w_br_xa[l], w_out[l]], tm=1024, tn=3584)
    proj3 = proj2.reshape(B, S, -1)

    bias = jnp.stack([b_ml_igate[l][0], b_ml_fgate[l][0], b_ml_igate[l][1], b_ml_fgate[l][1]],
                     axis=-1).astype(F32)
    y_ml = _mlstm(proj3, grow.reshape(H, 4, T // ML_CHUNK, ML_CHUNK), bias[:, :, None],
                  w_ml_conv[l].astype(F32), row2(b_ml_conv[l]), row2(ml_norm_g[l]))

    nb = (4 * ML_WIDTH) // NA_WIDTH
    y_na, w_up_b, w_down_b = _na(proj3, _na_bias_patterns(na_rpb[l]), w_ffn_up[l], w_ffn_down[l],
                                 nb, nb + 1, nb + 2)

    kv = _memkv(mem.reshape(B * M, D), row2(mem_norm_g[l]), w_kv_b, tm=512)
    xb = (4 * ML_WIDTH + 3 * NA_WIDTH) // XA_WIDTH
    y_xa = _xa(proj3, kv.reshape(B, M, -1), xb, tq=S)

    gate_blk = (4 * ML_WIDTH + 3 * NA_WIDTH + XA_WIDTH) // D
    x1 = _merge(x.reshape(T, D), y_ml.reshape(T, -1), y_na.reshape(T, -1), y_xa.reshape(T, -1),
                proj2, row2(b_merge_gate[l]), w_ml_b, w_na_b, w_xa_b, w_out_b, gate_blk, tm=1024)

    return _ffn(x1.reshape(B, S, D), row2(ffn_norm_g[l]), w_up_b,
                w_ffn_conv[l].astype(F32), row2(b_ffn_conv[l]), w_down_b,
                row2(final_norm_g), tf=256)
```

```python
import math

import jax
import jax.numpy as jnp
from jax import lax
from jax.experimental import pallas as pl
from jax.experimental.pallas import tpu as pltpu

F32 = jnp.float32
BF16 = jnp.bfloat16
EPS = 1e-6
NEG_INIT = -1e30
LOG2E = 1.4426950408889634

GRID_W = 64
ML_HEADS = 4
ML_HEAD_DIM = 128
ML_WIDTH = ML_HEADS * ML_HEAD_DIM
LOG2_Q_SCALE = -0.5 * math.log2(ML_HEAD_DIM)
NA_HEADS = 8
NA_HEAD_DIM = 64
NA_WIDTH = NA_HEADS * NA_HEAD_DIM
NA_WIN_ROWS = 8
NA_WIN_COLS = 16
XA_HEADS = 4
XA_HEAD_DIM = 128
XA_WIDTH = XA_HEADS * XA_HEAD_DIM

LANES = 128
F32_ROWS = 8
BF16_ROWS = 16
ML_CHUNK = 256
ML_HEADS_PER_STEP = 2
NA_ROWS_PER_STEP = 32
FFN_UP_ROWS = 512
FFN_DOWN_ROWS = 512
VMEM_LIMIT = 60 * 1024 * 1024


def _params(dims, vmem=None):
    return pltpu.CompilerParams(dimension_semantics=dims, vmem_limit_bytes=vmem)


def _rms(x, g):
    return x * lax.rsqrt(jnp.mean(x * x, axis=-1, keepdims=True) + EPS) * g


def _wprep_kernel(w_ref, wg_ref, o_ref, og_ref):
    o_ref[...] = w_ref[...].astype(BF16)
    og_ref[...] = wg_ref[...]


def _wprep(w_t, g0, n_gate, tr):
    N, D = w_t.shape
    assert g0 % tr == 0 and g0 % n_gate == 0 and (N - n_gate) % tr == 0
    return pl.pallas_call(
        _wprep_kernel,
        grid=((N - n_gate) // tr,),
        in_specs=[
            pl.BlockSpec((pl.Element(tr), pl.Element(D)),
                         lambda i: (pl.multiple_of(i * tr + jnp.where(i * tr >= g0, n_gate, 0), n_gate), 0)),
            pl.BlockSpec((n_gate, D), lambda i: (g0 // n_gate, 0)),
        ],
        out_specs=[pl.BlockSpec((tr, D), lambda i: (i, 0)),
                   pl.BlockSpec((n_gate, D), lambda i: (0, 0))],
        out_shape=[jax.ShapeDtypeStruct((N - n_gate, D), BF16),
                   jax.ShapeDtypeStruct((n_gate, D), F32)],
        compiler_params=_params(("arbitrary",), VMEM_LIMIT),
        name="wprep",
    )(w_t, w_t)


def _inproj_kernel(x_ref, g_ref, w_ref, wg_ref, *rest):
    n_cast = (len(rest) - 3) // 2
    cast_in, (o_ref, gates_ref), cast_out, h_scr = (
        rest[:n_cast], rest[n_cast:n_cast + 2], rest[n_cast + 2:2 * n_cast + 2], rest[-1])
    for src, dst in zip(cast_in, cast_out):
        dst[...] = src[...].astype(BF16)

    @pl.when(pl.program_id(1) == 0)
    def _():
        hb = _rms(x_ref[...], g_ref[...]).astype(BF16)
        h_scr[...] = hb
        gt = lax.dot_general(wg_ref[...], hb, (((1,), (1,)), ((), ())), preferred_element_type=F32)
        for h in range(ML_HEADS):
            gates_ref[h] = gt[4 * h:4 * h + 4, :]

    o_ref[...] = lax.dot_general(h_scr[...], w_ref[...], (((1,), (1,)), ((), ())),
                                 preferred_element_type=F32).astype(BF16)


def _inproj(x2, g, w_main, w_gate, to_cast, tm, tn):
    T, D = x2.shape
    N = w_main.shape[0]
    ni, nj = T // tm, N // tn
    steps = ni * nj
    assert all(a.shape[0] % (steps * BF16_ROWS) == 0 for a in to_cast)
    slab = lambda a: pl.BlockSpec((a.shape[0] // steps, a.shape[1]), lambda i, j: (i * nj + j, 0))
    return pl.pallas_call(
        _inproj_kernel,
        grid=(ni, nj),
        in_specs=[
            pl.BlockSpec((tm, D), lambda i, j: (i, 0)),
            pl.BlockSpec((1, D), lambda i, j: (0, 0)),
            pl.BlockSpec((tn, D), lambda i, j: (j, 0)),
            pl.BlockSpec(w_gate.shape, lambda i, j: (0, 0)),
        ] + [slab(a) for a in to_cast],
        out_specs=[
            pl.BlockSpec((tm, tn), lambda i, j: (i, j)),
            pl.BlockSpec((ML_HEADS, 4, tm), lambda i, j: (0, 0, i)),
        ] + [slab(a) for a in to_cast],
        out_shape=[
            jax.ShapeDtypeStruct((T, N), BF16),
            jax.ShapeDtypeStruct((ML_HEADS, 4, T), F32),
        ] + [jax.ShapeDtypeStruct(a.shape, BF16) for a in to_cast],
        scratch_shapes=[pltpu.VMEM((tm, D), BF16)],
        compiler_params=_params(("parallel", "arbitrary"), VMEM_LIMIT),
        name="inproj",
    )(x2, g, w_main, w_gate, *to_cast)


def _log_sigmoid(x):
    return jnp.minimum(x, 0.0) - jnp.log(1.0 + jnp.exp(-jnp.abs(x)))


def _chunk_scan(x, pos, L, reverse, op, identity):
    n = x.shape[1]
    d = 1
    while d < L:
        if reverse:
            x = op(x, jnp.where(pos < L - d, pltpu.roll(x, n - d, 1), identity))
        else:
            x = op(x, jnp.where(pos >= d, pltpu.roll(x, d, 1), identity))
        d *= 2
    return x


def _mlstm_kernel(q_ref, k_ref, v_ref, o_ref, gr_ref, br_ref,
                  wq_ref, wk_ref, bq_ref, bk_ref, ng_ref, y_ref,
                  qt_scr, k_scr, vat_scr, bb_scr, hf_scr, hb_scr):
    S = q_ref.shape[1]
    L = ML_CHUNK
    NC = S // L
    Dh = ML_HEAD_DIM
    s_idx = lax.broadcasted_iota(jnp.int32, (L, L), 0)
    j_idx = lax.broadcasted_iota(jnp.int32, (L, L), 1)
    pos = lax.broadcasted_iota(jnp.int32, (NC, L), 1)

    def head(hh):
        hs = slice(hh * Dh, (hh + 1) * Dh)
        gates = []
        for d, rev in ((0, False), (1, True)):
            i_pre = gr_ref[hh, 2 * d] + br_ref[hh, 2 * d:2 * d + 1, :]
            lf = _log_sigmoid(gr_ref[hh, 2 * d + 1] + br_ref[hh, 2 * d + 1:2 * d + 2, :])
            a = _chunk_scan(lf, pos, L, rev, jnp.add, 0.0)
            b = i_pre - a
            gates.append((b, a, _chunk_scan(b, pos, L, rev, jnp.maximum, -jnp.inf), lf))

        def conv_silu(x_ref, w_ref, b_ref, c):
            x = x_ref[0, c * L:(c + 1) * L, hs].astype(F32)
            zero_row = jnp.zeros((1, Dh), F32)
            prev_row = (x_ref[0, c * L - BF16_ROWS:c * L, hs].astype(F32)[BF16_ROWS - 1:, :]
                        if c > 0 else zero_row)
            next_row = (x_ref[0, (c + 1) * L:(c + 1) * L + BF16_ROWS, hs].astype(F32)[0:1, :]
                        if c < NC - 1 else zero_row)
            tile = F32_ROWS
            row = lax.broadcasted_iota(jnp.int32, (tile, Dh), 0)
            x_prev = pltpu.roll(x, 1, 0)
            x_prev = jnp.concatenate([jnp.where(row == 0, prev_row, x_prev[:tile]), x_prev[tile:]], axis=0)
            x_next = pltpu.roll(x, L - 1, 0)
            x_next = jnp.concatenate(
                [x_next[:L - tile], jnp.where(row == tile - 1, next_row, x_next[L - tile:])], axis=0)
            w = w_ref[:, hs]
            y = x_prev * w[0:1] + x * w[1:2] + x_next * w[2:3] + b_ref[:, hs]
            return y / (1.0 + jnp.exp2(y * (-LOG2E)))

        vat_scr[hh, Dh:, :] = jnp.ones((Dh, S), BF16)

        def prep(c):
            cs = slice(c * L, (c + 1) * L)
            qt_scr[hh, :, cs] = conv_silu(q_ref, wq_ref, bq_ref, c).astype(BF16).T
            k_scr[hh, cs, :] = conv_silu(k_ref, wk_ref, bk_ref, c).astype(BF16)
            vat_scr[hh, :Dh, cs] = v_ref[0, cs, hs].T
            for d in range(2):
                bb_scr[hh, d, cs, :] = jnp.broadcast_to(gates[d][0][c:c + 1, :] * LOG2E, (LANES, L)).T

        def chunk(c, m, st_t, fwd):
            cs = slice(c * L, (c + 1) * L)
            d = 0 if fwd else 1
            h_scr = hf_scr if fwd else hb_scr
            b_row, a_row, bmax_row, lf_row = (t[c:c + 1, :] for t in gates[d])
            mask_t = (s_idx <= j_idx) if fwd else (s_idx >= j_idx)
            g = jnp.sum(lf_row, axis=1, keepdims=True)
            w_end = g + b_row
            m_new = jnp.maximum(g + m, jnp.max(w_end, axis=1, keepdims=True))
            decay = jnp.exp(g + m - m_new)
            wt = jnp.exp(w_end - m_new)
            c_row = jnp.maximum(m, bmax_row)
            inter_w = jnp.exp(m - c_row)
            floor = jnp.exp(-(a_row + c_row))
            k = k_scr[hh, cs, :]
            q_t = qt_scr[hh, :, cs]
            va_t = vat_scr[hh, :, cs]
            qk_t = jnp.dot(k, q_t, preferred_element_type=F32)
            b_col = bb_scr[hh, d, cs, :]
            expo = jnp.concatenate([b_col] * (L // LANES), axis=1) - (c_row * LOG2E - LOG2_Q_SCALE)
            p_t = jnp.exp2(jnp.where(mask_t, expo, -jnp.inf)) * qk_t
            q_in = q_t * (inter_w * (Dh ** -0.5)).astype(BF16)
            nd_t = (jnp.dot(va_t, p_t.astype(BF16), preferred_element_type=F32)
                    + jnp.dot(st_t.astype(BF16), q_in, preferred_element_type=F32))
            h_scr[hh, :, cs] = nd_t[:Dh] / jnp.maximum(jnp.abs(nd_t[Dh:]), floor)
            wt_b = wt.astype(BF16)
            vw = jnp.concatenate([va_t[:Dh] * wt_b, jnp.broadcast_to(wt_b, (Dh, L))], axis=0)
            return m_new, decay * st_t + jnp.dot(vw, k, preferred_element_type=F32)

        def finish(c):
            cs = slice(c * L, (c + 1) * L)
            h_t = hf_scr[hh, :, cs] + hb_scr[hh, :, cs]
            h_t = h_t * lax.rsqrt(jnp.mean(h_t * h_t, axis=0, keepdims=True) + EPS)
            o_gate = 1.0 / (1.0 + jnp.exp2(o_ref[0, cs, hs].astype(F32) * (-LOG2E)))
            y = h_t.T * ng_ref[:, hs] * o_gate
            y_ref[0, cs, hs] = y.astype(BF16)

        return prep, chunk, finish

    heads = [head(hh) for hh in range(ML_HEADS_PER_STEP)]
    m0 = jnp.full((1, 1), NEG_INIT, F32)
    st0 = jnp.zeros((2 * Dh, Dh), F32)
    state = [(m0, st0, m0, st0) for _ in heads]
    for prep, _, _ in heads:
        prep(0)
        prep(NC - 1)
    for c in range(NC):
        for prep, _, _ in heads:
            if c + 1 < NC - 1 - (c + 1):
                prep(c + 1)
                prep(NC - 2 - c)
            elif c + 1 == NC - 1 - (c + 1):
                prep(c + 1)
        for n, (_, chunk, _) in enumerate(heads):
            m_f, st_f, m_b, st_b = state[n]
            m_f, st_f = chunk(c, m_f, st_f, True)
            m_b, st_b = chunk(NC - 1 - c, m_b, st_b, False)
            state[n] = (m_f, st_f, m_b, st_b)
        for _, _, finish in heads:
            if c >= NC - 1 - c:
                finish(c)
                if c != NC - 1 - c:
                    finish(NC - 1 - c)


def _mlstm(proj3, grow, brow, w_conv, b_conv, norm_g):
    B, S, _ = proj3.shape
    H, Dh, G = ML_HEADS, ML_HEAD_DIM, ML_HEADS_PER_STEP
    assert H % G == 0
    ng = H // G
    blk = lambda part: pl.BlockSpec((1, S, G * Dh), lambda b, h: (b, 0, part * ng + h))
    par = lambda rows, part: pl.BlockSpec((rows, G * Dh), lambda b, h: (0, part * ng + h))
    head_buf = lambda shape, dt: pltpu.VMEM((G,) + shape, dt)
    return pl.pallas_call(
        _mlstm_kernel,
        grid=(B, ng),
        in_specs=[
            blk(0), blk(1), blk(2), blk(3),
            pl.BlockSpec((G, 4, S // ML_CHUNK, ML_CHUNK), lambda b, h: (h, 0, b, 0)),
            pl.BlockSpec((G, 4, 1), lambda b, h: (h, 0, 0)),
            par(3, 0), par(3, 1), par(1, 0), par(1, 1), par(1, 0),
        ],
        out_specs=pl.BlockSpec((1, S, G * Dh), lambda b, h: (b, 0, h)),
        out_shape=jax.ShapeDtypeStruct((B, S, ML_WIDTH), BF16),
        scratch_shapes=[
            head_buf((Dh, S), BF16), head_buf((S, Dh), BF16), head_buf((2 * Dh, S), BF16),
            head_buf((2, S, LANES), F32),
            head_buf((Dh, S), F32), head_buf((Dh, S), F32),
        ],
        compiler_params=_params(("parallel", "parallel"), VMEM_LIMIT),
        name="mlstm",
    )(proj3, proj3, proj3, proj3, grow, brow, w_conv, w_conv, b_conv, b_conv, norm_g)


def _na_kernel(q_ref, k_ref, v_ref, pat_ref, wup_ref, wdn_ref, y_ref, wup_o_ref, wdn_o_ref, t_scr):
    wup_o_ref[...] = wup_ref[...].astype(BF16)
    wdn_o_ref[...] = wdn_ref[...].astype(BF16)

    S = k_ref.shape[1]
    rows = S // GRID_W
    W = GRID_W
    n_pairs = NA_HEADS // 2
    n_dr = t_scr.shape[0]
    r0 = pl.program_id(1) * NA_ROWS_PER_STEP
    low = lax.broadcasted_iota(jnp.int32, (W, LANES), 1) < NA_HEAD_DIM

    @pl.when((pl.program_id(0) == 0) & (pl.program_id(1) == 0))
    def _():
        cq = lax.broadcasted_iota(jnp.int32, (W, LANES), 0)
        ck = lax.broadcasted_iota(jnp.int32, (W, LANES), 1) % W
        col_start = jnp.clip(cq - NA_WIN_COLS // 2, 0, W - NA_WIN_COLS)
        col_ok = (ck >= col_start) & (ck < col_start + NA_WIN_COLS)
        for d in range(n_dr):
            for h in range(NA_HEADS):
                pattern = jnp.broadcast_to(pat_ref[h, d:d + 1, :], (W, LANES))
                skewed = pltpu.roll(pattern, 0, 1, stride=1, stride_axis=0)
                t_scr[d, h // 2, (h % 2) * W:(h % 2 + 1) * W, :] = jnp.where(col_ok, skewed, -jnp.inf)

    ones = jnp.ones((NA_WIN_ROWS * W, LANES), BF16)

    def row_body(i, _):
        r = r0 + i
        row_start = jnp.clip(r - NA_WIN_ROWS // 2, 0, rows - NA_WIN_ROWS)
        dr0 = row_start - r + (NA_WIN_ROWS - 1)
        band = pl.ds(pl.multiple_of(row_start * W, W), NA_WIN_ROWS * W)
        qrow = pl.ds(pl.multiple_of(i * W, W), W)
        scores = []
        for hp in range(n_pairs):
            cols = slice(hp * LANES, (hp + 1) * LANES)
            q2 = q_ref[0, qrow, cols] * (NA_HEAD_DIM ** -0.5)
            zero = jnp.zeros_like(q2)
            qs = jnp.concatenate([jnp.where(low, q2, zero), jnp.where(low, zero, q2)], axis=0)
            s = lax.dot_general(qs, k_ref[0, band, cols], (((1,), (1,)), ((), ())),
                                preferred_element_type=F32)
            bias = jnp.concatenate([t_scr[dr0 + kk, hp] for kk in range(0, NA_WIN_ROWS, 2)], axis=-1)
            scores.append(s + bias)
        probs = [jnp.exp(s - jnp.max(s, axis=-1, keepdims=True)).astype(BF16) for s in scores]
        for hp in range(n_pairs):
            cols = slice(hp * LANES, (hp + 1) * LANES)
            v_aug = jnp.concatenate([v_ref[0, band, cols], ones], axis=1)
            o = jnp.dot(probs[hp], v_aug, preferred_element_type=F32)
            o = o[:, :LANES] / o[:, LANES:]
            y_ref[0, qrow, cols] = jnp.where(low, o[:W], o[W:]).astype(BF16)
        return 0

    lax.fori_loop(0, NA_ROWS_PER_STEP, row_body, 0, unroll=8)


def _na(proj3, patterns, w_up, w_down, q_blk, k_blk, v_blk):
    B, S, _ = proj3.shape
    rows = S // GRID_W
    rb = NA_ROWS_PER_STEP
    n_dr = patterns.shape[1]
    nr = rows // rb
    steps = B * nr
    assert w_up.shape[0] % (steps * BF16_ROWS) == 0 and w_down.shape[0] % (steps * BF16_ROWS) == 0
    slab = lambda a: pl.BlockSpec((a.shape[0] // steps, a.shape[1]), lambda b, r: (b * nr + r, 0))
    return pl.pallas_call(
        _na_kernel,
        grid=(B, nr),
        in_specs=[
            pl.BlockSpec((1, rb * GRID_W, NA_WIDTH), lambda b, r: (b, r, q_blk)),
            pl.BlockSpec((1, S, NA_WIDTH), lambda b, r: (b, 0, k_blk)),
            pl.BlockSpec((1, S, NA_WIDTH), lambda b, r: (b, 0, v_blk)),
            pl.BlockSpec(patterns.shape, lambda b, r: (0, 0, 0)),
            slab(w_up), slab(w_down),
        ],
        out_specs=[pl.BlockSpec((1, rb * GRID_W, NA_WIDTH), lambda b, r: (b, r, 0)), slab(w_up), slab(w_down)],
        out_shape=[jax.ShapeDtypeStruct((B, S, NA_WIDTH), BF16),
                   jax.ShapeDtypeStruct(w_up.shape, BF16), jax.ShapeDtypeStruct(w_down.shape, BF16)],
        scratch_shapes=[pltpu.VMEM((n_dr, NA_HEADS // 2, 2 * GRID_W, LANES), F32)],
        compiler_params=_params(("arbitrary", "arbitrary"), VMEM_LIMIT),
        name="natten",
    )(proj3, proj3, proj3, patterns, w_up, w_down)


def _na_bias_patterns(rpb):
    c0 = NA_WIN_COLS - 1
    r = rpb.astype(F32)
    gap = jnp.zeros(r.shape[:1] + (r.shape[1] - 1, LANES // 2 - NA_WIN_COLS - c0), F32)
    return jnp.concatenate([r[:, :-1, c0:], gap, r[:, 1:, :], gap, r[:, :-1, :c0]], axis=-1)


def _memkv_kernel(m_ref, g_ref, w_ref, o_ref):
    hb = _rms(m_ref[...], g_ref[...]).astype(BF16)
    o_ref[...] = jnp.dot(hb, w_ref[...], preferred_element_type=F32).astype(BF16)


def _memkv(mem2, g, w, tm):
    T, D = mem2.shape
    N = w.shape[1]
    return pl.pallas_call(
        _memkv_kernel,
        grid=(T // tm,),
        in_specs=[
            pl.BlockSpec((tm, D), lambda i: (i, 0)),
            pl.BlockSpec((1, D), lambda i: (0, 0)),
            pl.BlockSpec((D, N), lambda i: (0, 0)),
        ],
        out_specs=pl.BlockSpec((tm, N), lambda i: (i, 0)),
        out_shape=jax.ShapeDtypeStruct((T, N), BF16),
        compiler_params=_params(("parallel",), VMEM_LIMIT),
        name="memkv",
    )(mem2, g, w)


def _xa_kernel(q_ref, kv_ref, y_ref):
    Dh = XA_HEAD_DIM
    M = kv_ref.shape[1]
    ones = jnp.ones((M, Dh), BF16)
    for h in range(XA_HEADS):
        q = q_ref[0, :, h * Dh:(h + 1) * Dh]
        k = kv_ref[0, :, h * Dh:(h + 1) * Dh]
        v = kv_ref[0, :, XA_WIDTH + h * Dh:XA_WIDTH + (h + 1) * Dh]
        s = lax.dot_general(q, k, (((1,), (1,)), ((), ())), preferred_element_type=F32)
        p = jnp.exp2((s - jnp.max(s, axis=-1, keepdims=True)) * (Dh ** -0.5 * LOG2E))
        o = jnp.dot(p.astype(BF16), jnp.concatenate([v, ones], axis=1), preferred_element_type=F32)
        y_ref[0, :, h * Dh:(h + 1) * Dh] = (o[:, :Dh] / o[:, Dh:]).astype(BF16)


def _xa(proj3, kv3, q_blk, tq):
    B, S, _ = proj3.shape
    M = kv3.shape[1]
    return pl.pallas_call(
        _xa_kernel,
        grid=(B, S // tq),
        in_specs=[
            pl.BlockSpec((1, tq, XA_WIDTH), lambda b, i: (b, i, q_blk)),
            pl.BlockSpec((1, M, 2 * XA_WIDTH), lambda b, i: (b, 0, 0)),
        ],
        out_specs=pl.BlockSpec((1, tq, XA_WIDTH), lambda b, i: (b, i, 0)),
        out_shape=jax.ShapeDtypeStruct((B, S, XA_WIDTH), BF16),
        compiler_params=_params(("parallel", "parallel"), VMEM_LIMIT),
        name="memxattn",
    )(proj3, kv3)


def _merge_kernel(x_ref, yml_ref, yna_ref, yxa_ref, p0_ref, p1_ref, p2_ref, bg_ref,
                  wml_ref, wna_ref, wxa_ref, wo_ref, o_ref):
    D = x_ref.shape[1]
    merged = None
    for n, (y_ref, p_ref, w_ref) in enumerate(
            ((yml_ref, p0_ref, wml_ref), (yna_ref, p1_ref, wna_ref), (yxa_ref, p2_ref, wxa_ref))):
        gate = 0.5 + 0.5 * jnp.tanh(0.5 * (p_ref[...].astype(F32) + bg_ref[:, n * D:(n + 1) * D]))
        term = gate * jnp.dot(y_ref[...], w_ref[...], preferred_element_type=F32)
        merged = term if merged is None else merged + term
    o_ref[...] = x_ref[...] + jnp.dot(merged.astype(BF16), wo_ref[...], preferred_element_type=F32)


def _merge(x2, yml, yna, yxa, proj2, b_gate, wml, wna, wxa, wo, gate_blk, tm):
    T, D = x2.shape
    row = lambda w: pl.BlockSpec((tm, w), lambda i: (i, 0))
    full = lambda a: pl.BlockSpec(a.shape, lambda i: (0, 0))
    return pl.pallas_call(
        _merge_kernel,
        grid=(T // tm,),
        in_specs=[
            row(D), row(ML_WIDTH), row(NA_WIDTH), row(XA_WIDTH),
            pl.BlockSpec((tm, D), lambda i: (i, gate_blk)),
            pl.BlockSpec((tm, D), lambda i: (i, gate_blk + 1)),
            pl.BlockSpec((tm, D), lambda i: (i, gate_blk + 2)),
            full(b_gate), full(wml), full(wna), full(wxa), full(wo),
        ],
        out_specs=row(D),
        out_shape=jax.ShapeDtypeStruct((T, D), F32),
        compiler_params=_params(("parallel",), VMEM_LIMIT),
        name="merge",
    )(x2, yml, yna, yxa, proj2, proj2, proj2, b_gate, wml, wna, wxa, wo)


def _gelu_tanh(x):
    return 0.5 * x * (1.0 + jnp.tanh(0.7978845608028654 * (x + 0.044715 * (x * x * x))))


def _ffn_kernel(x_ref, g_ref, wa0_ref, wa1_ref, wu0_ref, wu1_ref, wc0_ref, wc1_ref, bc0_ref, bc1_ref,
                wd0_ref, wd1_ref, gf_ref, o_ref, h_scr, a0_scr, u0_scr, a1_scr, u1_scr):
    t = pl.program_id(1)
    last = pl.num_programs(1) - 1
    S = x_ref.shape[1]

    def produce(wa_ref, wu_ref, a_scr, u_scr):
        for r in range(0, S, FFN_UP_ROWS):
            h = h_scr[r:r + FFN_UP_ROWS, :]
            a_scr[r:r + FFN_UP_ROWS, :] = jnp.dot(h, wa_ref[...], preferred_element_type=F32)
            u_scr[r:r + FFN_UP_ROWS, :] = jnp.dot(h, wu_ref[...], preferred_element_type=F32)

    def consume(a_scr, u_scr, wc_ref, bc_ref, wd_ref, first=False, final=False):
        a = a_scr[...]
        row = lax.broadcasted_iota(jnp.int32, a.shape, 0)
        a_prev = jnp.where(row == 0, 0.0, pltpu.roll(a, 1, 0))
        a_next = jnp.where(row == S - 1, 0.0, pltpu.roll(a, S - 1, 0))
        w = wc_ref[...]
        bc = bc_ref[...]
        for r in range(0, S, FFN_DOWN_ROWS):
            rows = slice(r, r + FFN_DOWN_ROWS)
            conv = a_prev[rows] * w[0:1] + a[rows] * w[1:2] + a_next[rows] * w[2:3] + bc
            act = (_gelu_tanh(conv) * u_scr[rows, :]).astype(BF16)
            base = x_ref[0, rows, :] if first else o_ref[0, rows, :]
            total = base + jnp.dot(act, wd_ref[...], preferred_element_type=F32)
            o_ref[0, rows, :] = _rms(total, gf_ref[...]) if final else total

    @pl.when(t == 0)
    def _():
        h_scr[...] = _rms(x_ref[0], g_ref[...]).astype(BF16)
        produce(wa0_ref, wu0_ref, a0_scr, u0_scr)
        produce(wa1_ref, wu1_ref, a1_scr, u1_scr)
        consume(a0_scr, u0_scr, wc1_ref, bc1_ref, wd1_ref, first=True)

    @pl.when((t > 0) & (t < last))
    def _():
        produce(wa0_ref, wu0_ref, a0_scr, u0_scr)
        consume(a1_scr, u1_scr, wc0_ref, bc0_ref, wd0_ref)
        produce(wa1_ref, wu1_ref, a1_scr, u1_scr)
        consume(a0_scr, u0_scr, wc1_ref, bc1_ref, wd1_ref)

    @pl.when(t == last)
    def _():
        produce(wa0_ref, wu0_ref, a0_scr, u0_scr)
        consume(a1_scr, u1_scr, wc0_ref, bc0_ref, wd0_ref)
        consume(a0_scr, u0_scr, wc1_ref, bc1_ref, wd1_ref, final=True)


def _ffn(x3, g, w_up, w_conv, b_conv, w_down, g_final, tf):
    B, S, D = x3.shape
    FF = w_down.shape[0]
    nf = FF // tf
    assert nf % 2 == 1 and nf >= 3, "the two-chunk pipeline ends on a produce-one / consume-two step"
    steps = (nf + 1) // 2
    prod0 = lambda t: 2 * t
    prod1 = lambda t: jnp.minimum(2 * t + 1, nf - 1)
    cons0 = lambda t: jnp.maximum(2 * t - 1, 0)
    cons1 = lambda t: 2 * t
    col = lambda f, off=0: pl.BlockSpec((D, tf), lambda b, t: (0, off + f(t)))
    vec = lambda rows, f: pl.BlockSpec((rows, tf), lambda b, t: (0, f(t)))
    dn = lambda f: pl.BlockSpec((tf, D), lambda b, t: (f(t), 0))
    buf = pltpu.VMEM((S, tf), F32)
    return pl.pallas_call(
        _ffn_kernel,
        grid=(B, steps),
        in_specs=[
            pl.BlockSpec((1, S, D), lambda b, t: (b, 0, 0)),
            pl.BlockSpec((1, D), lambda b, t: (0, 0)),
            col(prod0), col(prod1), col(prod0, nf), col(prod1, nf),
            vec(3, cons0), vec(3, cons1), vec(1, cons0), vec(1, cons1),
            dn(cons0), dn(cons1),
            pl.BlockSpec((1, D), lambda b, t: (0, 0)),
        ],
        out_specs=pl.BlockSpec((1, S, D), lambda b, t: (b, 0, 0)),
        out_shape=jax.ShapeDtypeStruct((B, S, D), F32),
        scratch_shapes=[pltpu.VMEM((S, D), BF16), buf, buf, buf, buf],
        compiler_params=_params(("parallel", "arbitrary"), VMEM_LIMIT),
        name="ffn",
    )(x3, g, w_up, w_up, w_up, w_up, w_conv, w_conv, b_conv, b_conv, w_down, w_down, g_final)


def kernel(x, mem, mix_norm_g, w_in, b_ml_igate, b_ml_fgate, w_ml_conv, b_ml_conv, ml_norm_g, na_rpb, mem_norm_g, w_mem_kv, b_merge_gate, w_br_ml, w_br_na, w_br_xa, w_out, ffn_norm_g, w_ffn_up, w_ffn_conv, b_ffn_conv, w_ffn_down, final_norm_g):
    B, S, D = x.shape
    H = ML_HEADS
    M = mem.shape[1]
    T = B * S
    assert w_in.shape[0] == 1, "single-layer block: the FFN kernel also applies the final norm"
    assert S % ML_CHUNK == 0 and S % (GRID_W * NA_ROWS_PER_STEP) == 0
    l = 0
    n_gate = 4 * H
    g0 = 4 * ML_WIDTH
    row2 = lambda v: v.reshape(1, -1).astype(F32)

    w_main, w_gate = _wprep(jnp.transpose(w_in[l]), g0, n_gate, tr=1024)
    w_gate = jnp.transpose(w_gate.reshape(4, H, D), (1, 0, 2)).reshape(n_gate, D).astype(BF16)

    proj2, grow, w_kv_b, w_ml_b, w_na_b, w_xa_b, w_out_b = _inproj(
        x.reshape(T, D), row2(mix_norm_g[l]), w_main, w_gate,
        [w_mem_kv[l], w_br_ml[l], w_br_na[l], w_br_xa[l], w_out[l]], tm=1024, tn=3584)
    proj3 = proj2.reshape(B, S, -1)

    bias = jnp.stack([b_ml_igate[l][0], b_ml_fgate[l][0], b_ml_igate[l][1], b_ml_fgate[l][1]],
                     axis=-1).astype(F32)
    y_ml = _mlstm(proj3, grow.reshape(H, 4, T // ML_CHUNK, ML_CHUNK), bias[:, :, None],
                  w_ml_conv[l].astype(F32), row2(b_ml_conv[l]), row2(ml_norm_g[l]))

    nb = (4 * ML_WIDTH) // NA_WIDTH
    y_na, w_up_b, w_down_b = _na(proj3, _na_bias_patterns(na_rpb[l]), w_ffn_up[l], w_ffn_down[l],
                                 nb, nb + 1, nb + 2)

    kv = _memkv(mem.reshape(B * M, D), row2(mem_norm_g[l]), w_kv_b, tm=512)
    xb = (4 * ML_WIDTH + 3 * NA_WIDTH) // XA_WIDTH
    y_xa = _xa(proj3, kv.reshape(B, M, -1), xb, tq=S)

    gate_blk = (4 * ML_WIDTH + 3 * NA_WIDTH + XA_WIDTH) // D
    x1 = _merge(x.reshape(T, D), y_ml.reshape(T, -1), y_na.reshape(T, -1), y_xa.reshape(T, -1),
                proj2, row2(b_merge_gate[l]), w_ml_b, w_na_b, w_xa_b, w_out_b, gate_blk, tm=1024)

    return _ffn(x1.reshape(B, S, D), row2(ffn_norm_g[l]), w_up_b,
                w_ffn_conv[l].astype(F32), row2(b_ffn_conv[l]), w_down_b,
                row2(final_norm_g), tf=256)
```

```python
import math

import jax
import jax.numpy as jnp
from jax import lax
from jax.experimental import pallas as pl
from jax.experimental.pallas import tpu as pltpu

F32 = jnp.float32
BF16 = jnp.bfloat16
EPS = 1e-6
NEG_INIT = -1e30
LOG2E = 1.4426950408889634

GRID_W = 64
ML_HEADS = 4
ML_HEAD_DIM = 128
ML_WIDTH = ML_HEADS * ML_HEAD_DIM
LOG2_Q_SCALE = -0.5 * math.log2(ML_HEAD_DIM)
NA_HEADS = 8
NA_HEAD_DIM = 64
NA_WIDTH = NA_HEADS * NA_HEAD_DIM
NA_WIN_ROWS = 8
NA_WIN_COLS = 16
XA_HEADS = 4
XA_HEAD_DIM = 128
XA_WIDTH = XA_HEADS * XA_HEAD_DIM

LANES = 128
F32_ROWS = 8
BF16_ROWS = 16
ML_CHUNK = 256
ML_HEADS_PER_STEP = 2
NA_ROWS_PER_STEP = 16
FFN_UP_ROWS = 512
FFN_DOWN_ROWS = 512
VMEM_LIMIT = 60 * 1024 * 1024


def _params(dims, vmem=None):
    return pltpu.CompilerParams(dimension_semantics=dims, vmem_limit_bytes=vmem)


def _rms(x, g):
    return x * lax.rsqrt(jnp.mean(x * x, axis=-1, keepdims=True) + EPS) * g


def _wprep_kernel(w_ref, wg_ref, o_ref, og_ref):
    o_ref[...] = w_ref[...].astype(BF16)
    og_ref[...] = wg_ref[...]


def _wprep(w_t, g0, n_gate, tr):
    N, D = w_t.shape
    assert g0 % tr == 0 and g0 % n_gate == 0 and (N - n_gate) % tr == 0
    return pl.pallas_call(
        _wprep_kernel,
        grid=((N - n_gate) // tr,),
        in_specs=[
            pl.BlockSpec((pl.Element(tr), pl.Element(D)),
                         lambda i: (pl.multiple_of(i * tr + jnp.where(i * tr >= g0, n_gate, 0), n_gate), 0)),
            pl.BlockSpec((n_gate, D), lambda i: (g0 // n_gate, 0)),
        ],
        out_specs=[pl.BlockSpec((tr, D), lambda i: (i, 0)),
                   pl.BlockSpec((n_gate, D), lambda i: (0, 0))],
        out_shape=[jax.ShapeDtypeStruct((N - n_gate, D), BF16),
                   jax.ShapeDtypeStruct((n_gate, D), F32)],
        compiler_params=_params(("arbitrary",), VMEM_LIMIT),
        name="wprep",
    )(w_t, w_t)


def _inproj_kernel(x_ref, g_ref, w_ref, wg_ref, *rest):
    n_cast = (len(rest) - 3) // 2
    cast_in, (o_ref, gates_ref), cast_out, h_scr = (
        rest[:n_cast], rest[n_cast:n_cast + 2], rest[n_cast + 2:2 * n_cast + 2], rest[-1])
    for src, dst in zip(cast_in, cast_out):
        dst[...] = src[...].astype(BF16)

    @pl.when(pl.program_id(1) == 0)
    def _():
        hb = _rms(x_ref[...], g_ref[...]).astype(BF16)
        h_scr[...] = hb
        gt = lax.dot_general(wg_ref[...], hb, (((1,), (1,)), ((), ())), preferred_element_type=F32)
        for h in range(ML_HEADS):
            gates_ref[h] = gt[4 * h:4 * h + 4, :]

    o_ref[...] = lax.dot_general(h_scr[...], w_ref[...], (((1,), (1,)), ((), ())),
                                 preferred_element_type=F32).astype(BF16)


def _inproj(x2, g, w_main, w_gate, to_cast, tm, tn):
    T, D = x2.shape
    N = w_main.shape[0]
    ni, nj = T // tm, N // tn
    steps = ni * nj
    assert all(a.shape[0] % (steps * BF16_ROWS) == 0 for a in to_cast)
    slab = lambda a: pl.BlockSpec((a.shape[0] // steps, a.shape[1]), lambda i, j: (i * nj + j, 0))
    return pl.pallas_call(
        _inproj_kernel,
        grid=(ni, nj),
        in_specs=[
            pl.BlockSpec((tm, D), lambda i, j: (i, 0)),
            pl.BlockSpec((1, D), lambda i, j: (0, 0)),
            pl.BlockSpec((tn, D), lambda i, j: (j, 0)),
            pl.BlockSpec(w_gate.shape, lambda i, j: (0, 0)),
        ] + [slab(a) for a in to_cast],
        out_specs=[
            pl.BlockSpec((tm, tn), lambda i, j: (i, j)),
            pl.BlockSpec((ML_HEADS, 4, tm), lambda i, j: (0, 0, i)),
        ] + [slab(a) for a in to_cast],
        out_shape=[
            jax.ShapeDtypeStruct((T, N), BF16),
            jax.ShapeDtypeStruct((ML_HEADS, 4, T), F32),
        ] + [jax.ShapeDtypeStruct(a.shape, BF16) for a in to_cast],
        scratch_shapes=[pltpu.VMEM((tm, D), BF16)],
        compiler_params=_params(("parallel", "arbitrary"), VMEM_LIMIT),
        name="inproj",
    )(x2, g, w_main, w_gate, *to_cast)


def _log_sigmoid(x):
    return jnp.minimum(x, 0.0) - jnp.log(1.0 + jnp.exp(-jnp.abs(x)))


def _chunk_scan(x, pos, L, reverse, op, identity):
    n = x.shape[1]
    d = 1
    while d < L:
        if reverse:
            x = op(x, jnp.where(pos < L - d, pltpu.roll(x, n - d, 1), identity))
        else:
            x = op(x, jnp.where(pos >= d, pltpu.roll(x, d, 1), identity))
        d *= 2
    return x


def _mlstm_kernel(q_ref, k_ref, v_ref, o_ref, gr_ref, br_ref,
                  wq_ref, wk_ref, bq_ref, bk_ref, ng_ref, y_ref,
                  qt_scr, k_scr, vat_scr, bb_scr, hf_scr, hb_scr):
    S = q_ref.shape[1]
    L = ML_CHUNK
    NC = S // L
    Dh = ML_HEAD_DIM
    s_idx = lax.broadcasted_iota(jnp.int32, (L, L), 0)
    j_idx = lax.broadcasted_iota(jnp.int32, (L, L), 1)
    pos = lax.broadcasted_iota(jnp.int32, (NC, L), 1)

    def head(hh):
        hs = slice(hh * Dh, (hh + 1) * Dh)
        gates = []
        for d, rev in ((0, False), (1, True)):
            i_pre = gr_ref[hh, 2 * d] + br_ref[hh, 2 * d:2 * d + 1, :]
            lf = _log_sigmoid(gr_ref[hh, 2 * d + 1] + br_ref[hh, 2 * d + 1:2 * d + 2, :])
            a = _chunk_scan(lf, pos, L, rev, jnp.add, 0.0)
            b = i_pre - a
            gates.append((b, a, _chunk_scan(b, pos, L, rev, jnp.maximum, -jnp.inf), lf))

        def conv_silu(x_ref, w_ref, b_ref, c):
            x = x_ref[0, c * L:(c + 1) * L, hs].astype(F32)
            zero_row = jnp.zeros((1, Dh), F32)
            prev_row = (x_ref[0, c * L - BF16_ROWS:c * L, hs].astype(F32)[BF16_ROWS - 1:, :]
                        if c > 0 else zero_row)
            next_row = (x_ref[0, (c + 1) * L:(c + 1) * L + BF16_ROWS, hs].astype(F32)[0:1, :]
                        if c < NC - 1 else zero_row)
            tile = F32_ROWS
            row = lax.broadcasted_iota(jnp.int32, (tile, Dh), 0)
            x_prev = pltpu.roll(x, 1, 0)
            x_prev = jnp.concatenate([jnp.where(row == 0, prev_row, x_prev[:tile]), x_prev[tile:]], axis=0)
            x_next = pltpu.roll(x, L - 1, 0)
            x_next = jnp.concatenate(
                [x_next[:L - tile], jnp.where(row == tile - 1, next_row, x_next[L - tile:])], axis=0)
            w = w_ref[:, hs]
            y = x_prev * w[0:1] + x * w[1:2] + x_next * w[2:3] + b_ref[:, hs]
            return y / (1.0 + jnp.exp2(y * (-LOG2E)))

        vat_scr[hh, Dh:, :] = jnp.ones((Dh, S), BF16)

        def prep(c):
            cs = slice(c * L, (c + 1) * L)
            qt_scr[hh, :, cs] = conv_silu(q_ref, wq_ref, bq_ref, c).astype(BF16).T
            k_scr[hh, cs, :] = conv_silu(k_ref, wk_ref, bk_ref, c).astype(BF16)
            vat_scr[hh, :Dh, cs] = v_ref[0, cs, hs].T
            for d in range(2):
                bb_scr[hh, d, cs, :] = jnp.broadcast_to(gates[d][0][c:c + 1, :] * LOG2E, (LANES, L)).T

        def chunk(c, m, st_t, fwd):
            cs = slice(c * L, (c + 1) * L)
            d = 0 if fwd else 1
            h_scr = hf_scr if fwd else hb_scr
            b_row, a_row, bmax_row, lf_row = (t[c:c + 1, :] for t in gates[d])
            mask_t = (s_idx <= j_idx) if fwd else (s_idx >= j_idx)
            g = jnp.sum(lf_row, axis=1, keepdims=True)
            w_end = g + b_row
            m_new = jnp.maximum(g + m, jnp.max(w_end, axis=1, keepdims=True))
            decay = jnp.exp(g + m - m_new)
            wt = jnp.exp(w_end - m_new)
            c_row = jnp.maximum(m, bmax_row)
            inter_w = jnp.exp(m - c_row)
            floor = jnp.exp(-(a_row + c_row))
            k = k_scr[hh, cs, :]
            q_t = qt_scr[hh, :, cs]
            va_t = vat_scr[hh, :, cs]
            qk_t = jnp.dot(k, q_t, preferred_element_type=F32)
            b_col = bb_scr[hh, d, cs, :]
            expo = jnp.concatenate([b_col] * (L // LANES), axis=1) - (c_row * LOG2E - LOG2_Q_SCALE)
            p_t = jnp.exp2(jnp.where(mask_t, expo, -jnp.inf)) * qk_t
            q_in = q_t * (inter_w * (Dh ** -0.5)).astype(BF16)
            nd_t = (jnp.dot(va_t, p_t.astype(BF16), preferred_element_type=F32)
                    + jnp.dot(st_t.astype(BF16), q_in, preferred_element_type=F32))
            h_scr[hh, :, cs] = nd_t[:Dh] / jnp.maximum(jnp.abs(nd_t[Dh:]), floor)
            wt_b = wt.astype(BF16)
            vw = jnp.concatenate([va_t[:Dh] * wt_b, jnp.broadcast_to(wt_b, (Dh, L))], axis=0)
            return m_new, decay * st_t + jnp.dot(vw, k, preferred_element_type=F32)

        def finish(c):
            cs = slice(c * L, (c + 1) * L)
            h_t = hf_scr[hh, :, cs] + hb_scr[hh, :, cs]
            h_t = h_t * lax.rsqrt(jnp.mean(h_t * h_t, axis=0, keepdims=True) + EPS)
            o_gate = 1.0 / (1.0 + jnp.exp2(o_ref[0, cs, hs].astype(F32) * (-LOG2E)))
            y = h_t.T * ng_ref[:, hs] * o_gate
            y_ref[0, cs, hs] = y.astype(BF16)

        return prep, chunk, finish

    heads = [head(hh) for hh in range(ML_HEADS_PER_STEP)]
    m0 = jnp.full((1, 1), NEG_INIT, F32)
    st0 = jnp.zeros((2 * Dh, Dh), F32)
    state = [(m0, st0, m0, st0) for _ in heads]
    for prep, _, _ in heads:
        prep(0)
        prep(NC - 1)
    for c in range(NC):
        for prep, _, _ in heads:
            if c + 1 < NC - 1 - (c + 1):
                prep(c + 1)
                prep(NC - 2 - c)
            elif c + 1 == NC - 1 - (c + 1):
                prep(c + 1)
        for n, (_, chunk, _) in enumerate(heads):
            m_f, st_f, m_b, st_b = state[n]
            m_f, st_f = chunk(c, m_f, st_f, True)
            m_b, st_b = chunk(NC - 1 - c, m_b, st_b, False)
            state[n] = (m_f, st_f, m_b, st_b)
        for _, _, finish in heads:
            if c >= NC - 1 - c:
                finish(c)
                if c != NC - 1 - c:
                    finish(NC - 1 - c)


def _mlstm(proj3, grow, brow, w_conv, b_conv, norm_g):
    B, S, _ = proj3.shape
    H, Dh, G = ML_HEADS, ML_HEAD_DIM, ML_HEADS_PER_STEP
    assert H % G == 0
    ng = H // G
    blk = lambda part: pl.BlockSpec((1, S, G * Dh), lambda b, h: (b, 0, part * ng + h))
    par = lambda rows, part: pl.BlockSpec((rows, G * Dh), lambda b, h: (0, part * ng + h))
    head_buf = lambda shape, dt: pltpu.VMEM((G,) + shape, dt)
    return pl.pallas_call(
        _mlstm_kernel,
        grid=(B, ng),
        in_specs=[
            blk(0), blk(1), blk(2), blk(3),
            pl.BlockSpec((G, 4, S // ML_CHUNK, ML_CHUNK), lambda b, h: (h, 0, b, 0)),
            pl.BlockSpec((G, 4, 1), lambda b, h: (h, 0, 0)),
            par(3, 0), par(3, 1), par(1, 0), par(1, 1), par(1, 0),
        ],
        out_specs=pl.BlockSpec((1, S, G * Dh), lambda b, h: (b, 0, h)),
        out_shape=jax.ShapeDtypeStruct((B, S, ML_WIDTH), BF16),
        scratch_shapes=[
            head_buf((Dh, S), BF16), head_buf((S, Dh), BF16), head_buf((2 * Dh, S), BF16),
            head_buf((2, S, LANES), F32),
            head_buf((Dh, S), F32), head_buf((Dh, S), F32),
        ],
        compiler_params=_params(("parallel", "parallel"), VMEM_LIMIT),
        name="mlstm",
    )(proj3, proj3, proj3, proj3, grow, brow, w_conv, w_conv, b_conv, b_conv, norm_g)


def _na_kernel(q_ref, k_ref, v_ref, pat_ref, wup_ref, wdn_ref, y_ref, wup_o_ref, wdn_o_ref, t_scr):
    wup_o_ref[...] = wup_ref[...].astype(BF16)
    wdn_o_ref[...] = wdn_ref[...].astype(BF16)

    S = k_ref.shape[1]
    rows = S // GRID_W
    W = GRID_W
    n_pairs = NA_HEADS // 2
    n_dr = t_scr.shape[0]
    r0 = pl.program_id(1) * NA_ROWS_PER_STEP
    low = lax.broadcasted_iota(jnp.int32, (W, LANES), 1) < NA_HEAD_DIM

    @pl.when((pl.program_id(0) == 0) & (pl.program_id(1) == 0))
    def _():
        cq = lax.broadcasted_iota(jnp.int32, (W, LANES), 0)
        ck = lax.broadcasted_iota(jnp.int32, (W, LANES), 1) % W
        col_start = jnp.clip(cq - NA_WIN_COLS // 2, 0, W - NA_WIN_COLS)
        col_ok = (ck >= col_start) & (ck < col_start + NA_WIN_COLS)
        for d in range(n_dr):
            for h in range(NA_HEADS):
                pattern = jnp.broadcast_to(pat_ref[h, d:d + 1, :], (W, LANES))
                skewed = pltpu.roll(pattern, 0, 1, stride=1, stride_axis=0)
                t_scr[d, h // 2, (h % 2) * W:(h % 2 + 1) * W, :] = jnp.where(col_ok, skewed, -jnp.inf)

    ones = jnp.ones((NA_WIN_ROWS * W, LANES), BF16)

    def row_body(i, _):
        r = r0 + i
        row_start = jnp.clip(r - NA_WIN_ROWS // 2, 0, rows - NA_WIN_ROWS)
        dr0 = row_start - r + (NA_WIN_ROWS - 1)
        band = pl.ds(pl.multiple_of(row_start * W, W), NA_WIN_ROWS * W)
        qrow = pl.ds(pl.multiple_of(i * W, W), W)
        scores = []
        for hp in range(n_pairs):
            cols = slice(hp * LANES, (hp + 1) * LANES)
            q2 = q_ref[0, qrow, cols] * (NA_HEAD_DIM ** -0.5)
            zero = jnp.zeros_like(q2)
            qs = jnp.concatenate([jnp.where(low, q2, zero), jnp.where(low, zero, q2)], axis=0)
            s = lax.dot_general(qs, k_ref[0, band, cols], (((1,), (1,)), ((), ())),
                                preferred_element_type=F32)
            bias = jnp.concatenate([t_scr[dr0 + kk, hp] for kk in range(0, NA_WIN_ROWS, 2)], axis=-1)
            scores.append(s + bias)
        probs = [jnp.exp(s - jnp.max(s, axis=-1, keepdims=True)).astype(BF16) for s in scores]
        for hp in range(n_pairs):
            cols = slice(hp * LANES, (hp + 1) * LANES)
            v_aug = jnp.concatenate([v_ref[0, band, cols], ones], axis=1)
            o = jnp.dot(probs[hp], v_aug, preferred_element_type=F32)
            o = o[:, :LANES] / o[:, LANES:]
            y_ref[0, qrow, cols] = jnp.where(low, o[:W], o[W:]).astype(BF16)
        return 0

    lax.fori_loop(0, NA_ROWS_PER_STEP, row_body, 0, unroll=8)


def _na(proj3, patterns, w_up, w_down, q_blk, k_blk, v_blk):
    B, S, _ = proj3.shape
    rows = S // GRID_W
    rb = NA_ROWS_PER_STEP
    n_dr = patterns.shape[1]
    nr = rows // rb
    steps = B * nr
    assert w_up.shape[0] % (steps * BF16_ROWS) == 0 and w_down.shape[0] % (steps * BF16_ROWS) == 0
    slab = lambda a: pl.BlockSpec((a.shape[0] // steps, a.shape[1]), lambda b, r: (b * nr + r, 0))
    return pl.pallas_call(
        _na_kernel,
        grid=(B, nr),
        in_specs=[
            pl.BlockSpec((1, rb * GRID_W, NA_WIDTH), lambda b, r: (b, r, q_blk)),
            pl.BlockSpec((1, S, NA_WIDTH), lambda b, r: (b, 0, k_blk)),
            pl.BlockSpec((1, S, NA_WIDTH), lambda b, r: (b, 0, v_blk)),
            pl.BlockSpec(patterns.shape, lambda b, r: (0, 0, 0)),
            slab(w_up), slab(w_down),
        ],
        out_specs=[pl.BlockSpec((1, rb * GRID_W, NA_WIDTH), lambda b, r: (b, r, 0)), slab(w_up), slab(w_down)],
        out_shape=[jax.ShapeDtypeStruct((B, S, NA_WIDTH), BF16),
                   jax.ShapeDtypeStruct(w_up.shape, BF16), jax.ShapeDtypeStruct(w_down.shape, BF16)],
        scratch_shapes=[pltpu.VMEM((n_dr, NA_HEADS // 2, 2 * GRID_W, LANES), F32)],
        compiler_params=_params(("arbitrary", "arbitrary"), VMEM_LIMIT),
        name="natten",
    )(proj3, proj3, proj3, patterns, w_up, w_down)


def _na_bias_patterns(rpb):
    c0 = NA_WIN_COLS - 1
    r = rpb.astype(F32)
    gap = jnp.zeros(r.shape[:1] + (r.shape[1] - 1, LANES // 2 - NA_WIN_COLS - c0), F32)
    return jnp.concatenate([r[:, :-1, c0:], gap, r[:, 1:, :], gap, r[:, :-1, :c0]], axis=-1)


def _memkv_kernel(m_ref, g_ref, w_ref, o_ref):
    hb = _rms(m_ref[...], g_ref[...]).astype(BF16)
    o_ref[...] = jnp.dot(hb, w_ref[...], preferred_element_type=F32).astype(BF16)


def _memkv(mem2, g, w, tm):
    T, D = mem2.shape
    N = w.shape[1]
    return pl.pallas_call(
        _memkv_kernel,
        grid=(T // tm,),
        in_specs=[
            pl.BlockSpec((tm, D), lambda i: (i, 0)),
            pl.BlockSpec((1, D), lambda i: (0, 0)),
            pl.BlockSpec((D, N), lambda i: (0, 0)),
        ],
        out_specs=pl.BlockSpec((tm, N), lambda i: (i, 0)),
        out_shape=jax.ShapeDtypeStruct((T, N), BF16),
        compiler_params=_params(("parallel",), VMEM_LIMIT),
        name="memkv",
    )(mem2, g, w)


def _xa_kernel(q_ref, kv_ref, y_ref):
    Dh = XA_HEAD_DIM
    M = kv_ref.shape[1]
    ones = jnp.ones((M, Dh), BF16)
    for h in range(XA_HEADS):
        q = q_ref[0, :, h * Dh:(h + 1) * Dh]
        k = kv_ref[0, :, h * Dh:(h + 1) * Dh]
        v = kv_ref[0, :, XA_WIDTH + h * Dh:XA_WIDTH + (h + 1) * Dh]
        s = lax.dot_general(q, k, (((1,), (1,)), ((), ())), preferred_element_type=F32)
        p = jnp.exp2((s - jnp.max(s, axis=-1, keepdims=True)) * (Dh ** -0.5 * LOG2E))
        o = jnp.dot(p.astype(BF16), jnp.concatenate([v, ones], axis=1), preferred_element_type=F32)
        y_ref[0, :, h * Dh:(h + 1) * Dh] = (o[:, :Dh] / o[:, Dh:]).astype(BF16)


def _xa(proj3, kv3, q_blk, tq):
    B, S, _ = proj3.shape
    M = kv3.shape[1]
    return pl.pallas_call(
        _xa_kernel,
        grid=(B, S // tq),
        in_specs=[
            pl.BlockSpec((1, tq, XA_WIDTH), lambda b, i: (b, i, q_blk)),
            pl.BlockSpec((1, M, 2 * XA_WIDTH), lambda b, i: (b, 0, 0)),
        ],
        out_specs=pl.BlockSpec((1, tq, XA_WIDTH), lambda b, i: (b, i, 0)),
        out_shape=jax.ShapeDtypeStruct((B, S, XA_WIDTH), BF16),
        compiler_params=_params(("parallel", "parallel"), VMEM_LIMIT),
        name="memxattn",
    )(proj3, kv3)


def _merge_kernel(x_ref, yml_ref, yna_ref, yxa_ref, p0_ref, p1_ref, p2_ref, bg_ref,
                  wml_ref, wna_ref, wxa_ref, wo_ref, o_ref):
    D = x_ref.shape[1]
    merged = None
    for n, (y_ref, p_ref, w_ref) in enumerate(
            ((yml_ref, p0_ref, wml_ref), (yna_ref, p1_ref, wna_ref), (yxa_ref, p2_ref, wxa_ref))):
        gate = 0.5 + 0.5 * jnp.tanh(0.5 * (p_ref[...].astype(F32) + bg_ref[:, n * D:(n + 1) * D]))
        term = gate * jnp.dot(y_ref[...], w_ref[...], preferred_element_type=F32)
        merged = term if merged is None else merged + term
    o_ref[...] = x_ref[...] + jnp.dot(merged.astype(BF16), wo_ref[...], preferred_element_type=F32)


def _merge(x2, yml, yna, yxa, proj2, b_gate, wml, wna, wxa, wo, gate_blk, tm):
    T, D = x2.shape
    row = lambda w: pl.BlockSpec((tm, w), lambda i: (i, 0))
    full = lambda a: pl.BlockSpec(a.shape, lambda i: (0, 0))
    return pl.pallas_call(
        _merge_kernel,
        grid=(T // tm,),
        in_specs=[
            row(D), row(ML_WIDTH), row(NA_WIDTH), row(XA_WIDTH),
            pl.BlockSpec((tm, D), lambda i: (i, gate_blk)),
            pl.BlockSpec((tm, D), lambda i: (i, gate_blk + 1)),
            pl.BlockSpec((tm, D), lambda i: (i, gate_blk + 2)),
            full(b_gate), full(wml), full(wna), full(wxa), full(wo),
        ],
        out_specs=row(D),
        out_shape=jax.ShapeDtypeStruct((T, D), F32),
        compiler_params=_params(("parallel",), VMEM_LIMIT),
        name="merge",
    )(x2, yml, yna, yxa, proj2, proj2, proj2, b_gate, wml, wna, wxa, wo)


def _gelu_tanh(x):
    return 0.5 * x * (1.0 + jnp.tanh(0.7978845608028654 * (x + 0.044715 * (x * x * x))))


def _ffn_kernel(x_ref, g_ref, wa0_ref, wa1_ref, wu0_ref, wu1_ref, wc0_ref, wc1_ref, bc0_ref, bc1_ref,
                wd0_ref, wd1_ref, gf_ref, o_ref, h_scr, a0_scr, u0_scr, a1_scr, u1_scr):
    t = pl.program_id(1)
    last = pl.num_programs(1) - 1
    S = x_ref.shape[1]

    def produce(wa_ref, wu_ref, a_scr, u_scr):
        for r in range(0, S, FFN_UP_ROWS):
            h = h_scr[r:r + FFN_UP_ROWS, :]
            a_scr[r:r + FFN_UP_ROWS, :] = jnp.dot(h, wa_ref[...], preferred_element_type=F32)
            u_scr[r:r + FFN_UP_ROWS, :] = jnp.dot(h, wu_ref[...], preferred_element_type=F32)

    def consume(a_scr, u_scr, wc_ref, bc_ref, wd_ref, first=False, final=False):
        a = a_scr[...]
        row = lax.broadcasted_iota(jnp.int32, a.shape, 0)
        a_prev = jnp.where(row == 0, 0.0, pltpu.roll(a, 1, 0))
        a_next = jnp.where(row == S - 1, 0.0, pltpu.roll(a, S - 1, 0))
        w = wc_ref[...]
        bc = bc_ref[...]
        for r in range(0, S, FFN_DOWN_ROWS):
            rows = slice(r, r + FFN_DOWN_ROWS)
            conv = a_prev[rows] * w[0:1] + a[rows] * w[1:2] + a_next[rows] * w[2:3] + bc
            act = (_gelu_tanh(conv) * u_scr[rows, :]).astype(BF16)
            base = x_ref[0, rows, :] if first else o_ref[0, rows, :]
            total = base + jnp.dot(act, wd_ref[...], preferred_element_type=F32)
            o_ref[0, rows, :] = _rms(total, gf_ref[...]) if final else total

    @pl.when(t == 0)
    def _():
        h_scr[...] = _rms(x_ref[0], g_ref[...]).astype(BF16)
        produce(wa0_ref, wu0_ref, a0_scr, u0_scr)
        produce(wa1_ref, wu1_ref, a1_scr, u1_scr)
        consume(a0_scr, u0_scr, wc1_ref, bc1_ref, wd1_ref, first=True)

    @pl.when((t > 0) & (t < last))
    def _():
        produce(wa0_ref, wu0_ref, a0_scr, u0_scr)
        consume(a1_scr, u1_scr, wc0_ref, bc0_ref, wd0_ref)
        produce(wa1_ref, wu1_ref, a1_scr, u1_scr)
        consume(a0_scr, u0_scr, wc1_ref, bc1_ref, wd1_ref)

    @pl.when(t == last)
    def _():
        produce(wa0_ref, wu0_ref, a0_scr, u0_scr)
        consume(a1_scr, u1_scr, wc0_ref, bc0_ref, wd0_ref)
        consume(a0_scr, u0_scr, wc1_ref, bc1_ref, wd1_ref, final=True)


def _ffn(x3, g, w_up, w_conv, b_conv, w_down, g_final, tf):
    B, S, D = x3.shape
    FF = w_down.shape[0]
    nf = FF // tf
    assert nf % 2 == 1 and nf >= 3, "the two-chunk pipeline ends on a produce-one / consume-two step"
    steps = (nf + 1) // 2
    prod0 = lambda t: 2 * t
    prod1 = lambda t: jnp.minimum(2 * t + 1, nf - 1)
    cons0 = lambda t: jnp.maximum(2 * t - 1, 0)
    cons1 = lambda t: 2 * t
    col = lambda f, off=0: pl.BlockSpec((D, tf), lambda b, t: (0, off + f(t)))
    vec = lambda rows, f: pl.BlockSpec((rows, tf), lambda b, t: (0, f(t)))
    dn = lambda f: pl.BlockSpec((tf, D), lambda b, t: (f(t), 0))
    buf = pltpu.VMEM((S, tf), F32)
    return pl.pallas_call(
        _ffn_kernel,
        grid=(B, steps),
        in_specs=[
            pl.BlockSpec((1, S, D), lambda b, t: (b, 0, 0)),
            pl.BlockSpec((1, D), lambda b, t: (0, 0)),
            col(prod0), col(prod1), col(prod0, nf), col(prod1, nf),
            vec(3, cons0), vec(3, cons1), vec(1, cons0), vec(1, cons1),
            dn(cons0), dn(cons1),
            pl.BlockSpec((1, D), lambda b, t: (0, 0)),
        ],
        out_specs=pl.BlockSpec((1, S, D), lambda b, t: (b, 0, 0)),
        out_shape=jax.ShapeDtypeStruct((B, S, D), F32),
        scratch_shapes=[pltpu.VMEM((S, D), BF16), buf, buf, buf, buf],
        compiler_params=_params(("parallel", "arbitrary"), VMEM_LIMIT),
        name="ffn",
    )(x3, g, w_up, w_up, w_up, w_up, w_conv, w_conv, b_conv, b_conv, w_down, w_down, g_final)


def kernel(x, mem, mix_norm_g, w_in, b_ml_igate, b_ml_fgate, w_ml_conv, b_ml_conv, ml_norm_g, na_rpb, mem_norm_g, w_mem_kv, b_merge_gate, w_br_ml, w_br_na, w_br_xa, w_out, ffn_norm_g, w_ffn_up, w_ffn_conv, b_ffn_conv, w_ffn_down, final_norm_g):
    B, S, D = x.shape
    H = ML_HEADS
    M = mem.shape[1]
    T = B * S
    assert w_in.shape[0] == 1, "single-layer block: the FFN kernel also applies the final norm"
    assert S % ML_CHUNK == 0 and S % (GRID_W * NA_ROWS_PER_STEP) == 0
    l = 0
    n_gate = 4 * H
    g0 = 4 * ML_WIDTH
    row2 = lambda v: v.reshape(1, -1).astype(F32)

    w_main, w_gate = _wprep(jnp.transpose(w_in[l]), g0, n_gate, tr=1024)
    w_gate = jnp.transpose(w_gate.reshape(4, H, D), (1, 0, 2)).reshape(n_gate, D).astype(BF16)

    proj2, grow, w_kv_b, w_ml_b, w_na_b, w_xa_b, w_out_b = _inproj(
        x.reshape(T, D), row2(mix_norm_g[l]), w_main, w_gate,
        [w_mem_kv[l], w_br_ml[l], w_br_na[l], w_br_xa[l], w_out[l]], tm=1024, tn=3584)
    proj3 = proj2.reshape(B, S, -1)

    bias = jnp.stack([b_ml_igate[l][0], b_ml_fgate[l][0], b_ml_igate[l][1], b_ml_fgate[l][1]],
                     axis=-1).astype(F32)
    y_ml = _mlstm(proj3, grow.reshape(H, 4, T // ML_CHUNK, ML_CHUNK), bias[:, :, None],
                  w_ml_conv[l].astype(F32), row2(b_ml_conv[l]), row2(ml_norm_g[l]))

    nb = (4 * ML_WIDTH) // NA_WIDTH
    y_na, w_up_b, w_down_b = _na(proj3, _na_bias_patterns(na_rpb[l]), w_ffn_up[l], w_ffn_down[l],
                                 nb, nb + 1, nb + 2)

    kv = _memkv(mem.reshape(B * M, D), row2(mem_norm_g[l]), w_kv_b, tm=512)
    xb = (4 * ML_WIDTH + 3 * NA_WIDTH) // XA_WIDTH
    y_xa = _xa(proj3, kv.reshape(B, M, -1), xb, tq=S)

    gate_blk = (4 * ML_WIDTH + 3 * NA_WIDTH + XA_WIDTH) // D
    x1 = _merge(x.reshape(T, D), y_ml.reshape(T, -1), y_na.reshape(T, -1), y_xa.reshape(T, -1),
                proj2, row2(b_merge_gate[l]), w_ml_b, w_na_b, w_xa_b, w_out_b, gate_blk, tm=1024)

    return _ffn(x1.reshape(B, S, D), row2(ffn_norm_g[l]), w_up_b,
                w_ffn_conv[l].astype(F32), row2(b_ffn_conv[l]), w_down_b,
                row2(final_norm_g), tf=256)
```

```python
import math

import jax
import jax.numpy as jnp
from jax import lax
from jax.experimental import pallas as pl
from jax.experimental.pallas import tpu as pltpu

F32 = jnp.float32
BF16 = jnp.bfloat16
EPS = 1e-6
NEG_INIT = -1e30
LOG2E = 1.4426950408889634

GRID_W = 64
ML_HEADS = 4
ML_HEAD_DIM = 128
ML_WIDTH = ML_HEADS * ML_HEAD_DIM
LOG2_Q_SCALE = -0.5 * math.log2(ML_HEAD_DIM)
NA_HEADS = 8
NA_HEAD_DIM = 64
NA_WIDTH = NA_HEADS * NA_HEAD_DIM
NA_WIN_ROWS = 8
NA_WIN_COLS = 16
XA_HEADS = 4
XA_HEAD_DIM = 128
XA_WIDTH = XA_HEADS * XA_HEAD_DIM

LANES = 128
F32_ROWS = 8
BF16_ROWS = 16
ML_CHUNK = 256
ML_HEADS_PER_STEP = 2
NA_ROWS_PER_STEP = 16
FFN_UP_ROWS = 512
FFN_DOWN_ROWS = 512
VMEM_LIMIT = 60 * 1024 * 1024


def _params(dims, vmem=None):
    return pltpu.CompilerParams(dimension_semantics=dims, vmem_limit_bytes=vmem)


def _rms(x, g):
    return x * lax.rsqrt(jnp.mean(x * x, axis=-1, keepdims=True) + EPS) * g


def _wprep_kernel(w_ref, wg_ref, o_ref, og_ref):
    o_ref[...] = w_ref[...].astype(BF16)
    og_ref[...] = wg_ref[...]


def _wprep(w_t, g0, n_gate, tr):
    N, D = w_t.shape
    assert g0 % tr == 0 and g0 % n_gate == 0 and (N - n_gate) % tr == 0
    return pl.pallas_call(
        _wprep_kernel,
        grid=((N - n_gate) // tr,),
        in_specs=[
            pl.BlockSpec((pl.Element(tr), pl.Element(D)),
                         lambda i: (pl.multiple_of(i * tr + jnp.where(i * tr >= g0, n_gate, 0), n_gate), 0)),
            pl.BlockSpec((n_gate, D), lambda i: (g0 // n_gate, 0)),
        ],
        out_specs=[pl.BlockSpec((tr, D), lambda i: (i, 0)),
                   pl.BlockSpec((n_gate, D), lambda i: (0, 0))],
        out_shape=[jax.ShapeDtypeStruct((N - n_gate, D), BF16),
                   jax.ShapeDtypeStruct((n_gate, D), F32)],
        compiler_params=_params(("arbitrary",), VMEM_LIMIT),
        name="wprep",
    )(w_t, w_t)


def _inproj_kernel(x_ref, g_ref, w_ref, wg_ref, *rest):
    n_cast = (len(rest) - 3) // 2
    cast_in, (o_ref, gates_ref), cast_out, h_scr = (
        rest[:n_cast], rest[n_cast:n_cast + 2], rest[n_cast + 2:2 * n_cast + 2], rest[-1])
    for src, dst in zip(cast_in, cast_out):
        dst[...] = src[...].astype(BF16)

    @pl.when(pl.program_id(1) == 0)
    def _():
        hb = _rms(x_ref[...], g_ref[...]).astype(BF16)
        h_scr[...] = hb
        gt = lax.dot_general(wg_ref[...], hb, (((1,), (1,)), ((), ())), preferred_element_type=F32)
        for h in range(ML_HEADS):
            gates_ref[h] = gt[4 * h:4 * h + 4, :]

    o_ref[...] = lax.dot_general(h_scr[...], w_ref[...], (((1,), (1,)), ((), ())),
                                 preferred_element_type=F32).astype(BF16)


def _inproj(x2, g, w_main, w_gate, to_cast, tm, tn):
    T, D = x2.shape
    N = w_main.shape[0]
    ni, nj = T // tm, N // tn
    steps = ni * nj
    assert all(a.shape[0] % (steps * BF16_ROWS) == 0 for a in to_cast)
    slab = lambda a: pl.BlockSpec((a.shape[0] // steps, a.shape[1]), lambda i, j: (i * nj + j, 0))
    return pl.pallas_call(
        _inproj_kernel,
        grid=(ni, nj),
        in_specs=[
            pl.BlockSpec((tm, D), lambda i, j: (i, 0)),
            pl.BlockSpec((1, D), lambda i, j: (0, 0)),
            pl.BlockSpec((tn, D), lambda i, j: (j, 0)),
            pl.BlockSpec(w_gate.shape, lambda i, j: (0, 0)),
        ] + [slab(a) for a in to_cast],
        out_specs=[
            pl.BlockSpec((tm, tn), lambda i, j: (i, j)),
            pl.BlockSpec((ML_HEADS, 4, tm), lambda i, j: (0, 0, i)),
        ] + [slab(a) for a in to_cast],
        out_shape=[
            jax.ShapeDtypeStruct((T, N), BF16),
            jax.ShapeDtypeStruct((ML_HEADS, 4, T), F32),
        ] + [jax.ShapeDtypeStruct(a.shape, BF16) for a in to_cast],
        scratch_shapes=[pltpu.VMEM((tm, D), BF16)],
        compiler_params=_params(("parallel", "arbitrary"), VMEM_LIMIT),
        name="inproj",
    )(x2, g, w_main, w_gate, *to_cast)


def _log_sigmoid(x):
    return jnp.minimum(x, 0.0) - jnp.log(1.0 + jnp.exp(-jnp.abs(x)))


def _chunk_scan(x, pos, L, reverse, op, identity):
    n = x.shape[1]
    d = 1
    while d < L:
        if reverse:
            x = op(x, jnp.where(pos < L - d, pltpu.roll(x, n - d, 1), identity))
        else:
            x = op(x, jnp.where(pos >= d, pltpu.roll(x, d, 1), identity))
        d *= 2
    return x


def _mlstm_kernel(q_ref, k_ref, v_ref, o_ref, gr_ref, br_ref,
                  wq_ref, wk_ref, bq_ref, bk_ref, ng_ref, y_ref,
                  qt_scr, k_scr, vat_scr, bb_scr, hf_scr, hb_scr):
    S = q_ref.shape[1]
    L = ML_CHUNK
    NC = S // L
    Dh = ML_HEAD_DIM
    s_idx = lax.broadcasted_iota(jnp.int32, (L, L), 0)
    j_idx = lax.broadcasted_iota(jnp.int32, (L, L), 1)
    pos = lax.broadcasted_iota(jnp.int32, (NC, L), 1)

    def head(hh):
        hs = slice(hh * Dh, (hh + 1) * Dh)
        gates = []
        for d, rev in ((0, False), (1, True)):
            i_pre = gr_ref[hh, 2 * d] + br_ref[hh, 2 * d:2 * d + 1, :]
            lf = _log_sigmoid(gr_ref[hh, 2 * d + 1] + br_ref[hh, 2 * d + 1:2 * d + 2, :])
            a = _chunk_scan(lf, pos, L, rev, jnp.add, 0.0)
            b = i_pre - a
            gates.append((b, a, _chunk_scan(b, pos, L, rev, jnp.maximum, -jnp.inf), lf))

        def conv_silu(x_ref, w_ref, b_ref, c):
            x = x_ref[0, c * L:(c + 1) * L, hs].astype(F32)
            zero_row = jnp.zeros((1, Dh), F32)
            prev_row = (x_ref[0, c * L - BF16_ROWS:c * L, hs].astype(F32)[BF16_ROWS - 1:, :]
                        if c > 0 else zero_row)
            next_row = (x_ref[0, (c + 1) * L:(c + 1) * L + BF16_ROWS, hs].astype(F32)[0:1, :]
                        if c < NC - 1 else zero_row)
            tile = F32_ROWS
            row = lax.broadcasted_iota(jnp.int32, (tile, Dh), 0)
            x_prev = pltpu.roll(x, 1, 0)
            x_prev = jnp.concatenate([jnp.where(row == 0, prev_row, x_prev[:tile]), x_prev[tile:]], axis=0)
            x_next = pltpu.roll(x, L - 1, 0)
            x_next = jnp.concatenate(
                [x_next[:L - tile], jnp.where(row == tile - 1, next_row, x_next[L - tile:])], axis=0)
            w = w_ref[:, hs]
            y = x_prev * w[0:1] + x * w[1:2] + x_next * w[2:3] + b_ref[:, hs]
            return y / (1.0 + jnp.exp2(y * (-LOG2E)))

        vat_scr[hh, Dh:, :] = jnp.ones((Dh, S), BF16)

        def prep(c):
            cs = slice(c * L, (c + 1) * L)
            qt_scr[hh, :, cs] = conv_silu(q_ref, wq_ref, bq_ref, c).astype(BF16).T
            k_scr[hh, cs, :] = conv_silu(k_ref, wk_ref, bk_ref, c).astype(BF16)
            vat_scr[hh, :Dh, cs] = v_ref[0, cs, hs].T
            for d in range(2):
                bb_scr[hh, d, cs, :] = jnp.broadcast_to(gates[d][0][c:c + 1, :] * LOG2E, (LANES, L)).T

        def chunk(c, m, st_t, fwd):
            cs = slice(c * L, (c + 1) * L)
            d = 0 if fwd else 1
            h_scr = hf_scr if fwd else hb_scr
            b_row, a_row, bmax_row, lf_row = (t[c:c + 1, :] for t in gates[d])
            mask_t = (s_idx <= j_idx) if fwd else (s_idx >= j_idx)
            g = jnp.sum(lf_row, axis=1, keepdims=True)
            w_end = g + b_row
            m_new = jnp.maximum(g + m, jnp.max(w_end, axis=1, keepdims=True))
            decay = jnp.exp(g + m - m_new)
            wt = jnp.exp(w_end - m_new)
            c_row = jnp.maximum(m, bmax_row)
            inter_w = jnp.exp(m - c_row)
            floor = jnp.exp(-(a_row + c_row))
            k = k_scr[hh, cs, :]
            q_t = qt_scr[hh, :, cs]
            va_t = vat_scr[hh, :, cs]
            qk_t = jnp.dot(k, q_t, preferred_element_type=F32)
            b_col = bb_scr[hh, d, cs, :]
            expo = jnp.concatenate([b_col] * (L // LANES), axis=1) - (c_row * LOG2E - LOG2_Q_SCALE)
            p_t = jnp.exp2(jnp.where(mask_t, expo, -jnp.inf)) * qk_t
            q_in = q_t * (inter_w * (Dh ** -0.5)).astype(BF16)
            nd_t = (jnp.dot(va_t, p_t.astype(BF16), preferred_element_type=F32)
                    + jnp.dot(st_t.astype(BF16), q_in, preferred_element_type=F32))
            h_scr[hh, :, cs] = nd_t[:Dh] / jnp.maximum(jnp.abs(nd_t[Dh:]), floor)
            wt_b = wt.astype(BF16)
            vw = jnp.concatenate([va_t[:Dh] * wt_b, jnp.broadcast_to(wt_b, (Dh, L))], axis=0)
            return m_new, decay * st_t + jnp.dot(vw, k, preferred_element_type=F32)

        def finish(c):
            cs = slice(c * L, (c + 1) * L)
            h_t = hf_scr[hh, :, cs] + hb_scr[hh, :, cs]
            h_t = h_t * lax.rsqrt(jnp.mean(h_t * h_t, axis=0, keepdims=True) + EPS)
            o_gate = 1.0 / (1.0 + jnp.exp2(o_ref[0, cs, hs].astype(F32) * (-LOG2E)))
            y = h_t.T * ng_ref[:, hs] * o_gate
            y_ref[0, cs, hs] = y.astype(BF16)

        return prep, chunk, finish

    heads = [head(hh) for hh in range(ML_HEADS_PER_STEP)]
    m0 = jnp.full((1, 1), NEG_INIT, F32)
    st0 = jnp.zeros((2 * Dh, Dh), F32)
    state = [(m0, st0, m0, st0) for _ in heads]
    for prep, _, _ in heads:
        prep(0)
        prep(NC - 1)
    for c in range(NC):
        for prep, _, _ in heads:
            if c + 1 < NC - 1 - (c + 1):
                prep(c + 1)
                prep(NC - 2 - c)
            elif c + 1 == NC - 1 - (c + 1):
                prep(c + 1)
        for n, (_, chunk, _) in enumerate(heads):
            m_f, st_f, m_b, st_b = state[n]
            m_f, st_f = chunk(c, m_f, st_f, True)
            m_b, st_b = chunk(NC - 1 - c, m_b, st_b, False)
            state[n] = (m_f, st_f, m_b, st_b)
        for _, _, finish in heads:
            if c >= NC - 1 - c:
                finish(c)
                if c != NC - 1 - c:
                    finish(NC - 1 - c)


def _mlstm(proj3, grow, brow, w_conv, b_conv, norm_g):
    B, S, _ = proj3.shape
    H, Dh, G = ML_HEADS, ML_HEAD_DIM, ML_HEADS_PER_STEP
    assert H % G == 0
    ng = H // G
    blk = lambda part: pl.BlockSpec((1, S, G * Dh), lambda b, h: (b, 0, part * ng + h))
    par = lambda rows, part: pl.BlockSpec((rows, G * Dh), lambda b, h: (0, part * ng + h))
    head_buf = lambda shape, dt: pltpu.VMEM((G,) + shape, dt)
    return pl.pallas_call(
        _mlstm_kernel,
        grid=(B, ng),
        in_specs=[
            blk(0), blk(1), blk(2), blk(3),
            pl.BlockSpec((G, 4, S // ML_CHUNK, ML_CHUNK), lambda b, h: (h, 0, b, 0)),
            pl.BlockSpec((G, 4, 1), lambda b, h: (h, 0, 0)),
            par(3, 0), par(3, 1), par(1, 0), par(1, 1), par(1, 0),
        ],
        out_specs=pl.BlockSpec((1, S, G * Dh), lambda b, h: (b, 0, h)),
        out_shape=jax.ShapeDtypeStruct((B, S, ML_WIDTH), BF16),
        scratch_shapes=[
            head_buf((Dh, S), BF16), head_buf((S, Dh), BF16), head_buf((2 * Dh, S), BF16),
            head_buf((2, S, LANES), F32),
            head_buf((Dh, S), F32), head_buf((Dh, S), F32),
        ],
        compiler_params=_params(("parallel", "parallel"), VMEM_LIMIT),
        name="mlstm",
    )(proj3, proj3, proj3, proj3, grow, brow, w_conv, w_conv, b_conv, b_conv, norm_g)


def _na_kernel(q_ref, k_ref, v_ref, pat_ref, wup_ref, wdn_ref, y_ref, wup_o_ref, wdn_o_ref, t_scr):
    wup_o_ref[...] = wup_ref[...].astype(BF16)
    wdn_o_ref[...] = wdn_ref[...].astype(BF16)

    S = k_ref.shape[1]
    rows = S // GRID_W
    W = GRID_W
    n_pairs = NA_HEADS // 2
    n_dr = t_scr.shape[0]
    r0 = pl.program_id(1) * NA_ROWS_PER_STEP
    low = lax.broadcasted_iota(jnp.int32, (W, LANES), 1) < NA_HEAD_DIM

    @pl.when((pl.program_id(0) == 0) & (pl.program_id(1) == 0))
    def _():
        cq = lax.broadcasted_iota(jnp.int32, (W, LANES), 0)
        ck = lax.broadcasted_iota(jnp.int32, (W, LANES), 1) % W
        col_start = jnp.clip(cq - NA_WIN_COLS // 2, 0, W - NA_WIN_COLS)
        col_ok = (ck >= col_start) & (ck < col_start + NA_WIN_COLS)
        for d in range(n_dr):
            for h in range(NA_HEADS):
                pattern = jnp.broadcast_to(pat_ref[h, d:d + 1, :], (W, LANES))
                skewed = pltpu.roll(pattern, 0, 1, stride=1, stride_axis=0)
                t_scr[d, h // 2, (h % 2) * W:(h % 2 + 1) * W, :] = jnp.where(col_ok, skewed, -jnp.inf)

    ones = jnp.ones((NA_WIN_ROWS * W, LANES), BF16)

    def row_body(i, _):
        r = r0 + i
        row_start = jnp.clip(r - NA_WIN_ROWS // 2, 0, rows - NA_WIN_ROWS)
        dr0 = row_start - r + (NA_WIN_ROWS - 1)
        band = pl.ds(pl.multiple_of(row_start * W, W), NA_WIN_ROWS * W)
        qrow = pl.ds(pl.multiple_of(i * W, W), W)
        scores = []
        for hp in range(n_pairs):
            cols = slice(hp * LANES, (hp + 1) * LANES)
            q2 = q_ref[0, qrow, cols] * (NA_HEAD_DIM ** -0.5)
            zero = jnp.zeros_like(q2)
            qs = jnp.concatenate([jnp.where(low, q2, zero), jnp.where(low, zero, q2)], axis=0)
            s = lax.dot_general(qs, k_ref[0, band, cols], (((1,), (1,)), ((), ())),
                                preferred_element_type=F32)
            bias = jnp.concatenate([t_scr[dr0 + kk, hp] for kk in range(0, NA_WIN_ROWS, 2)], axis=-1)
            scores.append(s + bias)
        probs = [jnp.exp(s - jnp.max(s, axis=-1, keepdims=True)).astype(BF16) for s in scores]
        for hp in range(n_pairs):
            cols = slice(hp * LANES, (hp + 1) * LANES)
            v_aug = jnp.concatenate([v_ref[0, band, cols], ones], axis=1)
            o = jnp.dot(probs[hp], v_aug, preferred_element_type=F32)
            o = o[:, :LANES] / o[:, LANES:]
            y_ref[0, qrow, cols] = jnp.where(low, o[:W], o[W:]).astype(BF16)
        return 0

    lax.fori_loop(0, NA_ROWS_PER_STEP, row_body, 0, unroll=8)


def _na(proj3, patterns, w_up, w_down, q_blk, k_blk, v_blk):
    B, S, _ = proj3.shape
    rows = S // GRID_W
    rb = NA_ROWS_PER_STEP
    n_dr = patterns.shape[1]
    nr = rows // rb
    steps = B * nr
    assert w_up.shape[0] % (steps * BF16_ROWS) == 0 and w_down.shape[0] % (steps * BF16_ROWS) == 0
    slab = lambda a: pl.BlockSpec((a.shape[0] // steps, a.shape[1]), lambda b, r: (b * nr + r, 0))
    return pl.pallas_call(
        _na_kernel,
        grid=(B, nr),
        in_specs=[
            pl.BlockSpec((1, rb * GRID_W, NA_WIDTH), lambda b, r: (b, r, q_blk)),
            pl.BlockSpec((1, S, NA_WIDTH), lambda b, r: (b, 0, k_blk)),
            pl.BlockSpec((1, S, NA_WIDTH), lambda b, r: (b, 0, v_blk)),
            pl.BlockSpec(patterns.shape, lambda b, r: (0, 0, 0)),
            slab(w_up), slab(w_down),
        ],
        out_specs=[pl.BlockSpec((1, rb * GRID_W, NA_WIDTH), lambda b, r: (b, r, 0)), slab(w_up), slab(w_down)],
        out_shape=[jax.ShapeDtypeStruct((B, S, NA_WIDTH), BF16),
                   jax.ShapeDtypeStruct(w_up.shape, BF16), jax.ShapeDtypeStruct(w_down.shape, BF16)],
        scratch_shapes=[pltpu.VMEM((n_dr, NA_HEADS // 2, 2 * GRID_W, LANES), F32)],
        compiler_params=_params(("arbitrary", "arbitrary"), VMEM_LIMIT),
        name="natten",
    )(proj3, proj3, proj3, patterns, w_up, w_down)


def _na_bias_patterns(rpb):
    c0 = NA_WIN_COLS - 1
    r = rpb.astype(F32)
    gap = jnp.zeros(r.shape[:1] + (r.shape[1] - 1, LANES // 2 - NA_WIN_COLS - c0), F32)
    return jnp.concatenate([r[:, :-1, c0:], gap, r[:, 1:, :], gap, r[:, :-1, :c0]], axis=-1)


def _xa_kernel(q_ref, m_ref, g_ref, w_ref, y_ref):
    Dh = XA_HEAD_DIM
    M = m_ref.shape[1]
    kv = jnp.dot(_rms(m_ref[0], g_ref[...]).astype(BF16), w_ref[...],
                 preferred_element_type=F32).astype(BF16)
    ones = jnp.ones((M, Dh), BF16)
    for h in range(XA_HEADS):
        q = q_ref[0, :, h * Dh:(h + 1) * Dh]
        k = kv[:, h * Dh:(h + 1) * Dh]
        v = kv[:, XA_WIDTH + h * Dh:XA_WIDTH + (h + 1) * Dh]
        s = lax.dot_general(q, k, (((1,), (1,)), ((), ())), preferred_element_type=F32)
        p = jnp.exp2((s - jnp.max(s, axis=-1, keepdims=True)) * (Dh ** -0.5 * LOG2E))
        o = jnp.dot(p.astype(BF16), jnp.concatenate([v, ones], axis=1), preferred_element_type=F32)
        y_ref[0, :, h * Dh:(h + 1) * Dh] = (o[:, :Dh] / o[:, Dh:]).astype(BF16)


def _xa(proj3, mem, g, w_kv, q_blk):
    B, S, _ = proj3.shape
    _, M, D = mem.shape
    return pl.pallas_call(
        _xa_kernel,
        grid=(B,),
        in_specs=[
            pl.BlockSpec((1, S, XA_WIDTH), lambda b: (b, 0, q_blk)),
            pl.BlockSpec((1, M, D), lambda b: (b, 0, 0)),
            pl.BlockSpec((1, D), lambda b: (0, 0)),
            pl.BlockSpec(w_kv.shape, lambda b: (0, 0)),
        ],
        out_specs=pl.BlockSpec((1, S, XA_WIDTH), lambda b: (b, 0, 0)),
        out_shape=jax.ShapeDtypeStruct((B, S, XA_WIDTH), BF16),
        compiler_params=_params(("parallel",), VMEM_LIMIT),
        name="memxattn",
    )(proj3, mem, g, w_kv)


def _merge_kernel(x_ref, yml_ref, yna_ref, yxa_ref, p0_ref, p1_ref, p2_ref, bg_ref,
                  wml_ref, wna_ref, wxa_ref, wo_ref, o_ref):
    D = x_ref.shape[1]
    merged = None
    for n, (y_ref, p_ref, w_ref) in enumerate(
            ((yml_ref, p0_ref, wml_ref), (yna_ref, p1_ref, wna_ref), (yxa_ref, p2_ref, wxa_ref))):
        gate = 0.5 + 0.5 * jnp.tanh(0.5 * (p_ref[...].astype(F32) + bg_ref[:, n * D:(n + 1) * D]))
        term = gate * jnp.dot(y_ref[...], w_ref[...], preferred_element_type=F32)
        merged = term if merged is None else merged + term
    o_ref[...] = x_ref[...] + jnp.dot(merged.astype(BF16), wo_ref[...], preferred_element_type=F32)


def _merge(x2, yml, yna, yxa, proj2, b_gate, wml, wna, wxa, wo, gate_blk, tm):
    T, D = x2.shape
    row = lambda w: pl.BlockSpec((tm, w), lambda i: (i, 0))
    full = lambda a: pl.BlockSpec(a.shape, lambda i: (0, 0))
    return pl.pallas_call(
        _merge_kernel,
        grid=(T // tm,),
        in_specs=[
            row(D), row(ML_WIDTH), row(NA_WIDTH), row(XA_WIDTH),
            pl.BlockSpec((tm, D), lambda i: (i, gate_blk)),
            pl.BlockSpec((tm, D), lambda i: (i, gate_blk + 1)),
            pl.BlockSpec((tm, D), lambda i: (i, gate_blk + 2)),
            full(b_gate), full(wml), full(wna), full(wxa), full(wo),
        ],
        out_specs=row(D),
        out_shape=jax.ShapeDtypeStruct((T, D), F32),
        compiler_params=_params(("parallel",), VMEM_LIMIT),
        name="merge",
    )(x2, yml, yna, yxa, proj2, proj2, proj2, b_gate, wml, wna, wxa, wo)


def _gelu_tanh(x):
    return 0.5 * x * (1.0 + jnp.tanh(0.7978845608028654 * (x + 0.044715 * (x * x * x))))


def _ffn_kernel(x_ref, g_ref, wa0_ref, wa1_ref, wu0_ref, wu1_ref, wc0_ref, wc1_ref, bc0_ref, bc1_ref,
                wd0_ref, wd1_ref, gf_ref, o_ref, h_scr, a0_scr, u0_scr, a1_scr, u1_scr):
    t = pl.program_id(1)
    last = pl.num_programs(1) - 1
    S = x_ref.shape[1]

    def produce(wa_ref, wu_ref, a_scr, u_scr):
        for r in range(0, S, FFN_UP_ROWS):
            h = h_scr[r:r + FFN_UP_ROWS, :]
            a_scr[r:r + FFN_UP_ROWS, :] = jnp.dot(h, wa_ref[...], preferred_element_type=F32)
            u_scr[r:r + FFN_UP_ROWS, :] = jnp.dot(h, wu_ref[...], preferred_element_type=F32)

    def consume(a_scr, u_scr, wc_ref, bc_ref, wd_ref, first=False, final=False):
        a = a_scr[...]
        row = lax.broadcasted_iota(jnp.int32, a.shape, 0)
        a_prev = jnp.where(row == 0, 0.0, pltpu.roll(a, 1, 0))
        a_next = jnp.where(row == S - 1, 0.0, pltpu.roll(a, S - 1, 0))
        w = wc_ref[...]
        bc = bc_ref[...]
        for r in range(0, S, FFN_DOWN_ROWS):
            rows = slice(r, r + FFN_DOWN_ROWS)
            conv = a_prev[rows] * w[0:1] + a[rows] * w[1:2] + a_next[rows] * w[2:3] + bc
            act = (_gelu_tanh(conv) * u_scr[rows, :]).astype(BF16)
            base = x_ref[0, rows, :] if first else o_ref[0, rows, :]
            total = base + jnp.dot(act, wd_ref[...], preferred_element_type=F32)
            o_ref[0, rows, :] = _rms(total, gf_ref[...]) if final else total

    @pl.when(t == 0)
    def _():
        h_scr[...] = _rms(x_ref[0], g_ref[...]).astype(BF16)
        produce(wa0_ref, wu0_ref, a0_scr, u0_scr)
        produce(wa1_ref, wu1_ref, a1_scr, u1_scr)
        consume(a0_scr, u0_scr, wc1_ref, bc1_ref, wd1_ref, first=True)

    @pl.when((t > 0) & (t < last))
    def _():
        produce(wa0_ref, wu0_ref, a0_scr, u0_scr)
        consume(a1_scr, u1_scr, wc0_ref, bc0_ref, wd0_ref)
        produce(wa1_ref, wu1_ref, a1_scr, u1_scr)
        consume(a0_scr, u0_scr, wc1_ref, bc1_ref, wd1_ref)

    @pl.when(t == last)
    def _():
        produce(wa0_ref, wu0_ref, a0_scr, u0_scr)
        consume(a1_scr, u1_scr, wc0_ref, bc0_ref, wd0_ref)
        consume(a0_scr, u0_scr, wc1_ref, bc1_ref, wd1_ref, final=True)


def _ffn(x3, g, w_up, w_conv, b_conv, w_down, g_final, tf):
    B, S, D = x3.shape
    FF = w_down.shape[0]
    nf = FF // tf
    assert nf % 2 == 1 and nf >= 3, "the two-chunk pipeline ends on a produce-one / consume-two step"
    steps = (nf + 1) // 2
    prod0 = lambda t: 2 * t
    prod1 = lambda t: jnp.minimum(2 * t + 1, nf - 1)
    cons0 = lambda t: jnp.maximum(2 * t - 1, 0)
    cons1 = lambda t: 2 * t
    col = lambda f, off=0: pl.BlockSpec((D, tf), lambda b, t: (0, off + f(t)))
    vec = lambda rows, f: pl.BlockSpec((rows, tf), lambda b, t: (0, f(t)))
    dn = lambda f: pl.BlockSpec((tf, D), lambda b, t: (f(t), 0))
    buf = pltpu.VMEM((S, tf), F32)
    return pl.pallas_call(
        _ffn_kernel,
        grid=(B, steps),
        in_specs=[
            pl.BlockSpec((1, S, D), lambda b, t: (b, 0, 0)),
            pl.BlockSpec((1, D), lambda b, t: (0, 0)),
            col(prod0), col(prod1), col(prod0, nf), col(prod1, nf),
            vec(3, cons0), vec(3, cons1), vec(1, cons0), vec(1, cons1),
            dn(cons0), dn(cons1),
            pl.BlockSpec((1, D), lambda b, t: (0, 0)),
        ],
        out_specs=pl.BlockSpec((1, S, D), lambda b, t: (b, 0, 0)),
        out_shape=jax.ShapeDtypeStruct((B, S, D), F32),
        scratch_shapes=[pltpu.VMEM((S, D), BF16), buf, buf, buf, buf],
        compiler_params=_params(("parallel", "arbitrary"), VMEM_LIMIT),
        name="ffn",
    )(x3, g, w_up, w_up, w_up, w_up, w_conv, w_conv, b_conv, b_conv, w_down, w_down, g_final)


def kernel(x, mem, mix_norm_g, w_in, b_ml_igate, b_ml_fgate, w_ml_conv, b_ml_conv, ml_norm_g, na_rpb, mem_norm_g, w_mem_kv, b_merge_gate, w_br_ml, w_br_na, w_br_xa, w_out, ffn_norm_g, w_ffn_up, w_ffn_conv, b_ffn_conv, w_ffn_down, final_norm_g):
    B, S, D = x.shape
    H = ML_HEADS
    T = B * S
    assert w_in.shape[0] == 1, "single-layer block: the FFN kernel also applies the final norm"
    assert S % ML_CHUNK == 0 and S % (GRID_W * NA_ROWS_PER_STEP) == 0
    l = 0
    n_gate = 4 * H
    g0 = 4 * ML_WIDTH
    row2 = lambda v: v.reshape(1, -1).astype(F32)

    w_main, w_gate = _wprep(jnp.transpose(w_in[l]), g0, n_gate, tr=1024)
    w_gate = jnp.transpose(w_gate.reshape(4, H, D), (1, 0, 2)).reshape(n_gate, D).astype(BF16)

    proj2, grow, w_kv_b, w_ml_b, w_na_b, w_xa_b, w_out_b = _inproj(
        x.reshape(T, D), row2(mix_norm_g[l]), w_main, w_gate,
        [w_mem_kv[l], w_br_ml[l], w_br_na[l], w_br_xa[l], w_out[l]], tm=1024, tn=3584)
    proj3 = proj2.reshape(B, S, -1)

    bias = jnp.stack([b_ml_igate[l][0], b_ml_fgate[l][0], b_ml_igate[l][1], b_ml_fgate[l][1]],
                     axis=-1).astype(F32)
    y_ml = _mlstm(proj3, grow.reshape(H, 4, T // ML_CHUNK, ML_CHUNK), bias[:, :, None],
                  w_ml_conv[l].astype(F32), row2(b_ml_conv[l]), row2(ml_norm_g[l]))

    nb = (4 * ML_WIDTH) // NA_WIDTH
    y_na, w_up_b, w_down_b = _na(proj3, _na_bias_patterns(na_rpb[l]), w_ffn_up[l], w_ffn_down[l],
                                 nb, nb + 1, nb + 2)

    xb = (4 * ML_WIDTH + 3 * NA_WIDTH) // XA_WIDTH
    y_xa = _xa(proj3, mem, row2(mem_norm_g[l]), w_kv_b, xb)

    gate_blk = (4 * ML_WIDTH + 3 * NA_WIDTH + XA_WIDTH) // D
    x1 = _merge(x.reshape(T, D), y_ml.reshape(T, -1), y_na.reshape(T, -1), y_xa.reshape(T, -1),
                proj2, row2(b_merge_gate[l]), w_ml_b, w_na_b, w_xa_b, w_out_b, gate_blk, tm=1024)

    return _ffn(x1.reshape(B, S, D), row2(ffn_norm_g[l]), w_up_b,
                w_ffn_conv[l].astype(F32), row2(b_ffn_conv[l]), w_down_b,
                row2(final_norm_g), tf=256)
```

```python
import jax
import jax.numpy as jnp
from jax import lax
from jax.experimental import pallas as pl
from jax.experimental.pallas import tpu as pltpu

F32 = jnp.float32
BF16 = jnp.bfloat16
EPS = 1e-6
NEG_INIT = -1e30
LOG2E = 1.4426950408889634

GRID_W = 64
ML_HEADS = 4
ML_HEAD_DIM = 128
ML_WIDTH = ML_HEADS * ML_HEAD_DIM
NA_HEADS = 8
NA_HEAD_DIM = 64
NA_WIDTH = NA_HEADS * NA_HEAD_DIM
NA_WIN_ROWS = 8
NA_WIN_COLS = 16
XA_HEADS = 4
XA_HEAD_DIM = 128
XA_WIDTH = XA_HEADS * XA_HEAD_DIM

LANES = 128
F32_ROWS = 8
BF16_ROWS = 16
ML_CHUNK = 256
ML_HEADS_PER_STEP = 2
NA_ROWS_PER_STEP = 16
FFN_UP_ROWS = 512
FFN_DOWN_ROWS = 512
VMEM_LIMIT = 60 * 1024 * 1024


def _params(dims, vmem=None):
    return pltpu.CompilerParams(dimension_semantics=dims, vmem_limit_bytes=vmem)


def _rms(x, g):
    return x * lax.rsqrt(jnp.mean(x * x, axis=-1, keepdims=True) + EPS) * g


def _wprep_kernel(w_ref, wg_ref, o_ref, og_ref):
    o_ref[...] = w_ref[...].astype(BF16)
    og_ref[...] = wg_ref[...]


def _wprep(w_t, g0, n_gate, tr):
    N, D = w_t.shape
    assert g0 % tr == 0 and g0 % n_gate == 0 and (N - n_gate) % tr == 0
    return pl.pallas_call(
        _wprep_kernel,
        grid=((N - n_gate) // tr,),
        in_specs=[
            pl.BlockSpec((pl.Element(tr), pl.Element(D)),
                         lambda i: (pl.multiple_of(i * tr + jnp.where(i * tr >= g0, n_gate, 0), n_gate), 0)),
            pl.BlockSpec((n_gate, D), lambda i: (g0 // n_gate, 0)),
        ],
        out_specs=[pl.BlockSpec((tr, D), lambda i: (i, 0)),
                   pl.BlockSpec((n_gate, D), lambda i: (0, 0))],
        out_shape=[jax.ShapeDtypeStruct((N - n_gate, D), BF16),
                   jax.ShapeDtypeStruct((n_gate, D), F32)],
        compiler_params=_params(("arbitrary",), VMEM_LIMIT),
        name="wprep",
    )(w_t, w_t)


def _inproj_kernel(x_ref, g_ref, w_ref, wg_ref, *rest):
    n_cast = (len(rest) - 3) // 2
    cast_in, (o_ref, gates_ref), cast_out, h_scr = (
        rest[:n_cast], rest[n_cast:n_cast + 2], rest[n_cast + 2:2 * n_cast + 2], rest[-1])
    for src, dst in zip(cast_in, cast_out):
        dst[...] = src[...].astype(BF16)

    @pl.when(pl.program_id(1) == 0)
    def _():
        hb = _rms(x_ref[...], g_ref[...]).astype(BF16)
        h_scr[...] = hb
        gt = lax.dot_general(wg_ref[...], hb, (((1,), (1,)), ((), ())), preferred_element_type=F32)
        for h in range(ML_HEADS):
            gates_ref[h] = gt[4 * h:4 * h + 4, :]

    o_ref[...] = lax.dot_general(h_scr[...], w_ref[...], (((1,), (1,)), ((), ())),
                                 preferred_element_type=F32).astype(BF16)


def _inproj(x2, g, w_main, w_gate, to_cast, tm, tn):
    T, D = x2.shape
    N = w_main.shape[0]
    ni, nj = T // tm, N // tn
    steps = ni * nj
    assert all(a.shape[0] % (steps * BF16_ROWS) == 0 for a in to_cast)
    slab = lambda a: pl.BlockSpec((a.shape[0] // steps, a.shape[1]), lambda i, j: (i * nj + j, 0))
    return pl.pallas_call(
        _inproj_kernel,
        grid=(ni, nj),
        in_specs=[
            pl.BlockSpec((tm, D), lambda i, j: (i, 0)),
            pl.BlockSpec((1, D), lambda i, j: (0, 0)),
            pl.BlockSpec((tn, D), lambda i, j: (j, 0)),
            pl.BlockSpec(w_gate.shape, lambda i, j: (0, 0)),
        ] + [slab(a) for a in to_cast],
        out_specs=[
            pl.BlockSpec((tm, tn), lambda i, j: (i, j)),
            pl.BlockSpec((ML_HEADS, 4, tm), lambda i, j: (0, 0, i)),
        ] + [slab(a) for a in to_cast],
        out_shape=[
            jax.ShapeDtypeStruct((T, N), BF16),
            jax.ShapeDtypeStruct((ML_HEADS, 4, T), F32),
        ] + [jax.ShapeDtypeStruct(a.shape, BF16) for a in to_cast],
        scratch_shapes=[pltpu.VMEM((tm, D), BF16)],
        compiler_params=_params(("parallel", "arbitrary"), VMEM_LIMIT),
        name="inproj",
    )(x2, g, w_main, w_gate, *to_cast)


def _log_sigmoid(x):
    return jnp.minimum(x, 0.0) - jnp.log(1.0 + jnp.exp(-jnp.abs(x)))


def _chunk_scan(x, pos, L, reverse, op, identity):
    n = x.shape[1]
    d = 1
    while d < L:
        if reverse:
            x = op(x, jnp.where(pos < L - d, pltpu.roll(x, n - d, 1), identity))
        else:
            x = op(x, jnp.where(pos >= d, pltpu.roll(x, d, 1), identity))
        d *= 2
    return x


def _mlstm_kernel(q_ref, k_ref, v_ref, o_ref, gr_ref, br_ref,
                  wq_ref, wk_ref, bq_ref, bk_ref, ng_ref, y_ref,
                  qt_scr, k_scr, vat_scr, bb_scr, hf_scr, hb_scr):
    S = q_ref.shape[1]
    L = ML_CHUNK
    NC = S // L
    Dh = ML_HEAD_DIM
    s_idx = lax.broadcasted_iota(jnp.int32, (L, L), 0)
    j_idx = lax.broadcasted_iota(jnp.int32, (L, L), 1)
    pos = lax.broadcasted_iota(jnp.int32, (NC, L), 1)

    def head(hh):
        hs = slice(hh * Dh, (hh + 1) * Dh)
        gates = []
        for d, rev in ((0, False), (1, True)):
            i_pre = gr_ref[hh, 2 * d] + br_ref[hh, 2 * d:2 * d + 1, :]
            lf = _log_sigmoid(gr_ref[hh, 2 * d + 1] + br_ref[hh, 2 * d + 1:2 * d + 2, :])
            a = _chunk_scan(lf, pos, L, rev, jnp.add, 0.0)
            b = i_pre - a
            gates.append((b, a, _chunk_scan(b, pos, L, rev, jnp.maximum, -jnp.inf), lf))

        def conv_silu(x_ref, w_ref, b_ref, c):
            x = x_ref[0, c * L:(c + 1) * L, hs].astype(F32)
            zero_row = jnp.zeros((1, Dh), F32)
            prev_row = (x_ref[0, c * L - BF16_ROWS:c * L, hs].astype(F32)[BF16_ROWS - 1:, :]
                        if c > 0 else zero_row)
            next_row = (x_ref[0, (c + 1) * L:(c + 1) * L + BF16_ROWS, hs].astype(F32)[0:1, :]
                        if c < NC - 1 else zero_row)
            tile = F32_ROWS
            row = lax.broadcasted_iota(jnp.int32, (tile, Dh), 0)
            x_prev = pltpu.roll(x, 1, 0)
            x_prev = jnp.concatenate([jnp.where(row == 0, prev_row, x_prev[:tile]), x_prev[tile:]], axis=0)
            x_next = pltpu.roll(x, L - 1, 0)
            x_next = jnp.concatenate(
                [x_next[:L - tile], jnp.where(row == tile - 1, next_row, x_next[L - tile:])], axis=0)
            w = w_ref[:, hs]
            y = x_prev * w[0:1] + x * w[1:2] + x_next * w[2:3] + b_ref[:, hs]
            return y / (1.0 + jnp.exp2(y * (-LOG2E)))

        vat_scr[hh, Dh:, :] = jnp.ones((Dh, S), BF16)

        def prep(c):
            cs = slice(c * L, (c + 1) * L)
            qt_scr[hh, :, cs] = (conv_silu(q_ref, wq_ref, bq_ref, c) * (Dh ** -0.5)).astype(BF16).T
            k_scr[hh, cs, :] = conv_silu(k_ref, wk_ref, bk_ref, c).astype(BF16)
            vat_scr[hh, :Dh, cs] = v_ref[0, cs, hs].T
            for d in range(2):
                bb_scr[hh, d, cs, :] = jnp.broadcast_to(gates[d][0][c:c + 1, :] * LOG2E, (LANES, L)).T

        def chunk(c, m, st_t, fwd):
            cs = slice(c * L, (c + 1) * L)
            d = 0 if fwd else 1
            h_scr = hf_scr if fwd else hb_scr
            b_row, a_row, bmax_row, lf_row = (t[c:c + 1, :] for t in gates[d])
            mask_t = (s_idx <= j_idx) if fwd else (s_idx >= j_idx)
            g = jnp.sum(lf_row, axis=1, keepdims=True)
            w_end = g + b_row
            m_new = jnp.maximum(g + m, jnp.max(w_end, axis=1, keepdims=True))
            decay = jnp.exp(g + m - m_new)
            wt = jnp.exp(w_end - m_new)
            c_row = jnp.maximum(m, bmax_row)
            inter_w = jnp.exp(m - c_row)
            floor = jnp.exp(-(a_row + c_row))
            k = k_scr[hh, cs, :]
            q_t = qt_scr[hh, :, cs]
            va_t = vat_scr[hh, :, cs]
            qk_t = jnp.dot(k, q_t, preferred_element_type=F32)
            b_col = bb_scr[hh, d, cs, :]
            expo = jnp.concatenate([b_col] * (L // LANES), axis=1) - c_row * LOG2E
            p_t = jnp.exp2(jnp.where(mask_t, expo, -jnp.inf)) * qk_t
            q_in = q_t * inter_w.astype(BF16)
            nd_t = (jnp.dot(va_t, p_t.astype(BF16), preferred_element_type=F32)
                    + jnp.dot(st_t.astype(BF16), q_in, preferred_element_type=F32))
            h_scr[hh, :, cs] = nd_t[:Dh] / jnp.maximum(jnp.abs(nd_t[Dh:]), floor)
            wt_b = wt.astype(BF16)
            vw = jnp.concatenate([va_t[:Dh] * wt_b, jnp.broadcast_to(wt_b, (Dh, L))], axis=0)
            return m_new, decay * st_t + jnp.dot(vw, k, preferred_element_type=F32)

        def finish(c):
            cs = slice(c * L, (c + 1) * L)
            h_t = hf_scr[hh, :, cs] + hb_scr[hh, :, cs]
            h_t = h_t * lax.rsqrt(jnp.mean(h_t * h_t, axis=0, keepdims=True) + EPS)
            y = h_t.T * ng_ref[:, hs] * jax.nn.sigmoid(o_ref[0, cs, hs].astype(F32))
            y_ref[0, cs, hs] = y.astype(BF16)

        return prep, chunk, finish

    heads = [head(hh) for hh in range(ML_HEADS_PER_STEP)]
    m0 = jnp.full((1, 1), NEG_INIT, F32)
    st0 = jnp.zeros((2 * Dh, Dh), F32)
    state = [(m0, st0, m0, st0) for _ in heads]
    for prep, _, _ in heads:
        prep(0)
        prep(NC - 1)
    for c in range(NC):
        for prep, _, _ in heads:
            if c + 1 < NC - 1 - (c + 1):
                prep(c + 1)
                prep(NC - 2 - c)
            elif c + 1 == NC - 1 - (c + 1):
                prep(c + 1)
        for n, (_, chunk, _) in enumerate(heads):
            m_f, st_f, m_b, st_b = state[n]
            m_f, st_f = chunk(c, m_f, st_f, True)
            m_b, st_b = chunk(NC - 1 - c, m_b, st_b, False)
            state[n] = (m_f, st_f, m_b, st_b)
        for _, _, finish in heads:
            if c >= NC - 1 - c:
                finish(c)
                if c != NC - 1 - c:
                    finish(NC - 1 - c)


def _mlstm(proj3, grow, brow, w_conv, b_conv, norm_g):
    B, S, _ = proj3.shape
    H, Dh, G = ML_HEADS, ML_HEAD_DIM, ML_HEADS_PER_STEP
    assert H % G == 0
    ng = H // G
    blk = lambda part: pl.BlockSpec((1, S, G * Dh), lambda b, h: (b, 0, part * ng + h))
    par = lambda rows, part: pl.BlockSpec((rows, G * Dh), lambda b, h: (0, part * ng + h))
    head_buf = lambda shape, dt: pltpu.VMEM((G,) + shape, dt)
    return pl.pallas_call(
        _mlstm_kernel,
        grid=(B, ng),
        in_specs=[
            blk(0), blk(1), blk(2), blk(3),
            pl.BlockSpec((G, 4, S // ML_CHUNK, ML_CHUNK), lambda b, h: (h, 0, b, 0)),
            pl.BlockSpec((G, 4, 1), lambda b, h: (h, 0, 0)),
            par(3, 0), par(3, 1), par(1, 0), par(1, 1), par(1, 0),
        ],
        out_specs=pl.BlockSpec((1, S, G * Dh), lambda b, h: (b, 0, h)),
        out_shape=jax.ShapeDtypeStruct((B, S, ML_WIDTH), BF16),
        scratch_shapes=[
            head_buf((Dh, S), BF16), head_buf((S, Dh), BF16), head_buf((2 * Dh, S), BF16),
            head_buf((2, S, LANES), F32),
            head_buf((Dh, S), F32), head_buf((Dh, S), F32),
        ],
        compiler_params=_params(("parallel", "parallel"), VMEM_LIMIT),
        name="mlstm",
    )(proj3, proj3, proj3, proj3, grow, brow, w_conv, w_conv, b_conv, b_conv, norm_g)


def _na_kernel(q_ref, k_ref, v_ref, pat_ref, wup_ref, wdn_ref, y_ref, wup_o_ref, wdn_o_ref, t_scr):
    wup_o_ref[...] = wup_ref[...].astype(BF16)
    wdn_o_ref[...] = wdn_ref[...].astype(BF16)

    S = k_ref.shape[1]
    rows = S // GRID_W
    W = GRID_W
    n_pairs = NA_HEADS // 2
    n_dr = t_scr.shape[0]
    r0 = pl.program_id(1) * NA_ROWS_PER_STEP
    low = lax.broadcasted_iota(jnp.int32, (W, LANES), 1) < NA_HEAD_DIM

    @pl.when((pl.program_id(0) == 0) & (pl.program_id(1) == 0))
    def _():
        cq = lax.broadcasted_iota(jnp.int32, (W, LANES), 0)
        ck = lax.broadcasted_iota(jnp.int32, (W, LANES), 1) % W
        col_start = jnp.clip(cq - NA_WIN_COLS // 2, 0, W - NA_WIN_COLS)
        col_ok = (ck >= col_start) & (ck < col_start + NA_WIN_COLS)
        for d in range(n_dr):
            for h in range(NA_HEADS):
                pattern = jnp.broadcast_to(pat_ref[h, d:d + 1, :], (W, LANES))
                skewed = pltpu.roll(pattern, 0, 1, stride=1, stride_axis=0)
                t_scr[d, h // 2, (h % 2) * W:(h % 2 + 1) * W, :] = jnp.where(col_ok, skewed, -jnp.inf)

    ones = jnp.ones((NA_WIN_ROWS * W, LANES), BF16)

    def row_body(i, _):
        r = r0 + i
        row_start = jnp.clip(r - NA_WIN_ROWS // 2, 0, rows - NA_WIN_ROWS)
        dr0 = row_start - r + (NA_WIN_ROWS - 1)
        band = pl.ds(pl.multiple_of(row_start * W, W), NA_WIN_ROWS * W)
        qrow = pl.ds(pl.multiple_of(i * W, W), W)
        scores = []
        for hp in range(n_pairs):
            cols = slice(hp * LANES, (hp + 1) * LANES)
            q2 = q_ref[0, qrow, cols] * (NA_HEAD_DIM ** -0.5)
            zero = jnp.zeros_like(q2)
            qs = jnp.concatenate([jnp.where(low, q2, zero), jnp.where(low, zero, q2)], axis=0)
            s = lax.dot_general(qs, k_ref[0, band, cols], (((1,), (1,)), ((), ())),
                                preferred_element_type=F32)
            bias = jnp.concatenate([t_scr[dr0 + kk, hp] for kk in range(0, NA_WIN_ROWS, 2)], axis=-1)
            scores.append(s + bias)
        probs = [jnp.exp(s - jnp.max(s, axis=-1, keepdims=True)).astype(BF16) for s in scores]
        for hp in range(n_pairs):
            cols = slice(hp * LANES, (hp + 1) * LANES)
            v_aug = jnp.concatenate([v_ref[0, band, cols], ones], axis=1)
            o = jnp.dot(probs[hp], v_aug, preferred_element_type=F32)
            o = o[:, :LANES] / o[:, LANES:]
            y_ref[0, qrow, cols] = jnp.where(low, o[:W], o[W:]).astype(BF16)
        return 0

    lax.fori_loop(0, NA_ROWS_PER_STEP, row_body, 0, unroll=8)


def _na(proj3, patterns, w_up, w_down, q_blk, k_blk, v_blk):
    B, S, _ = proj3.shape
    rows = S // GRID_W
    rb = NA_ROWS_PER_STEP
    n_dr = patterns.shape[1]
    nr = rows // rb
    steps = B * nr
    assert w_up.shape[0] % (steps * BF16_ROWS) == 0 and w_down.shape[0] % (steps * BF16_ROWS) == 0
    slab = lambda a: pl.BlockSpec((a.shape[0] // steps, a.shape[1]), lambda b, r: (b * nr + r, 0))
    return pl.pallas_call(
        _na_kernel,
        grid=(B, nr),
        in_specs=[
            pl.BlockSpec((1, rb * GRID_W, NA_WIDTH), lambda b, r: (b, r, q_blk)),
            pl.BlockSpec((1, S, NA_WIDTH), lambda b, r: (b, 0, k_blk)),
            pl.BlockSpec((1, S, NA_WIDTH), lambda b, r: (b, 0, v_blk)),
            pl.BlockSpec(patterns.shape, lambda b, r: (0, 0, 0)),
            slab(w_up), slab(w_down),
        ],
        out_specs=[pl.BlockSpec((1, rb * GRID_W, NA_WIDTH), lambda b, r: (b, r, 0)), slab(w_up), slab(w_down)],
        out_shape=[jax.ShapeDtypeStruct((B, S, NA_WIDTH), BF16),
                   jax.ShapeDtypeStruct(w_up.shape, BF16), jax.ShapeDtypeStruct(w_down.shape, BF16)],
        scratch_shapes=[pltpu.VMEM((n_dr, NA_HEADS // 2, 2 * GRID_W, LANES), F32)],
        compiler_params=_params(("arbitrary", "arbitrary"), VMEM_LIMIT),
        name="natten",
    )(proj3, proj3, proj3, patterns, w_up, w_down)


def _na_bias_patterns(rpb):
    c0 = NA_WIN_COLS - 1
    r = rpb.astype(F32)
    gap = jnp.zeros(r.shape[:1] + (r.shape[1] - 1, LANES // 2 - NA_WIN_COLS - c0), F32)
    return jnp.concatenate([r[:, :-1, c0:], gap, r[:, 1:, :], gap, r[:, :-1, :c0]], axis=-1)


def _xa_kernel(q_ref, m_ref, g_ref, w_ref, y_ref):
    Dh = XA_HEAD_DIM
    M = m_ref.shape[1]
    kv = jnp.dot(_rms(m_ref[0], g_ref[...]).astype(BF16), w_ref[...],
                 preferred_element_type=F32).astype(BF16)
    ones = jnp.ones((M, Dh), BF16)
    for h in range(XA_HEADS):
        q = q_ref[0, :, h * Dh:(h + 1) * Dh]
        k = kv[:, h * Dh:(h + 1) * Dh]
        v = kv[:, XA_WIDTH + h * Dh:XA_WIDTH + (h + 1) * Dh]
        s = lax.dot_general(q, k, (((1,), (1,)), ((), ())), preferred_element_type=F32)
        p = jnp.exp2((s - jnp.max(s, axis=-1, keepdims=True)) * (Dh ** -0.5 * LOG2E))
        o = jnp.dot(p.astype(BF16), jnp.concatenate([v, ones], axis=1), preferred_element_type=F32)
        y_ref[0, :, h * Dh:(h + 1) * Dh] = (o[:, :Dh] / o[:, Dh:]).astype(BF16)


def _xa(proj3, mem, g, w_kv, q_blk):
    B, S, _ = proj3.shape
    _, M, D = mem.shape
    return pl.pallas_call(
        _xa_kernel,
        grid=(B,),
        in_specs=[
            pl.BlockSpec((1, S, XA_WIDTH), lambda b: (b, 0, q_blk)),
            pl.BlockSpec((1, M, D), lambda b: (b, 0, 0)),
            pl.BlockSpec((1, D), lambda b: (0, 0)),
            pl.BlockSpec(w_kv.shape, lambda b: (0, 0)),
        ],
        out_specs=pl.BlockSpec((1, S, XA_WIDTH), lambda b: (b, 0, 0)),
        out_shape=jax.ShapeDtypeStruct((B, S, XA_WIDTH), BF16),
        compiler_params=_params(("parallel",), VMEM_LIMIT),
        name="memxattn",
    )(proj3, mem, g, w_kv)


def _merge_kernel(x_ref, yml_ref, yna_ref, yxa_ref, p0_ref, p1_ref, p2_ref, bg_ref,
                  wml_ref, wna_ref, wxa_ref, wo_ref, o_ref):
    D = x_ref.shape[1]
    merged = None
    for n, (y_ref, p_ref, w_ref) in enumerate(
            ((yml_ref, p0_ref, wml_ref), (yna_ref, p1_ref, wna_ref), (yxa_ref, p2_ref, wxa_ref))):
        gate = 0.5 + 0.5 * jnp.tanh(0.5 * (p_ref[...].astype(F32) + bg_ref[:, n * D:(n + 1) * D]))
        term = gate * jnp.dot(y_ref[...], w_ref[...], preferred_element_type=F32)
        merged = term if merged is None else merged + term
    o_ref[...] = x_ref[...] + jnp.dot(merged.astype(BF16), wo_ref[...], preferred_element_type=F32)


def _merge(x2, yml, yna, yxa, proj2, b_gate, wml, wna, wxa, wo, gate_blk, tm):
    T, D = x2.shape
    row = lambda w: pl.BlockSpec((tm, w), lambda i: (i, 0))
    full = lambda a: pl.BlockSpec(a.shape, lambda i: (0, 0))
    return pl.pallas_call(
        _merge_kernel,
        grid=(T // tm,),
        in_specs=[
            row(D), row(ML_WIDTH), row(NA_WIDTH), row(XA_WIDTH),
            pl.BlockSpec((tm, D), lambda i: (i, gate_blk)),
            pl.BlockSpec((tm, D), lambda i: (i, gate_blk + 1)),
            pl.BlockSpec((tm, D), lambda i: (i, gate_blk + 2)),
            full(b_gate), full(wml), full(wna), full(wxa), full(wo),
        ],
        out_specs=row(D),
        out_shape=jax.ShapeDtypeStruct((T, D), F32),
        compiler_params=_params(("parallel",), VMEM_LIMIT),
        name="merge",
    )(x2, yml, yna, yxa, proj2, proj2, proj2, b_gate, wml, wna, wxa, wo)


def _gelu_tanh(x):
    return 0.5 * x * (1.0 + jnp.tanh(0.7978845608028654 * (x + 0.044715 * (x * x * x))))


def _ffn_kernel(x_ref, g_ref, wa0_ref, wa1_ref, wu0_ref, wu1_ref, wc0_ref, wc1_ref, bc0_ref, bc1_ref,
                wd0_ref, wd1_ref, gf_ref, o_ref, h_scr, a0_scr, u0_scr, a1_scr, u1_scr):
    t = pl.program_id(1)
    last = pl.num_programs(1) - 1
    S = x_ref.shape[1]

    def produce(wa_ref, wu_ref, a_scr, u_scr):
        for r in range(0, S, FFN_UP_ROWS):
            h = h_scr[r:r + FFN_UP_ROWS, :]
            a_scr[r:r + FFN_UP_ROWS, :] = jnp.dot(h, wa_ref[...], preferred_element_type=F32)
            u_scr[r:r + FFN_UP_ROWS, :] = jnp.dot(h, wu_ref[...], preferred_element_type=F32)

    def consume(a_scr, u_scr, wc_ref, bc_ref, wd_ref, first=False, final=False):
        a = a_scr[...]
        row = lax.broadcasted_iota(jnp.int32, a.shape, 0)
        a_prev = jnp.where(row == 0, 0.0, pltpu.roll(a, 1, 0))
        a_next = jnp.where(row == S - 1, 0.0, pltpu.roll(a, S - 1, 0))
        w = wc_ref[...]
        bc = bc_ref[...]
        for r in range(0, S, FFN_DOWN_ROWS):
            rows = slice(r, r + FFN_DOWN_ROWS)
            conv = a_prev[rows] * w[0:1] + a[rows] * w[1:2] + a_next[rows] * w[2:3] + bc
            act = (_gelu_tanh(conv) * u_scr[rows, :]).astype(BF16)
            base = x_ref[0, rows, :] if first else o_ref[0, rows, :]
            total = base + jnp.dot(act, wd_ref[...], preferred_element_type=F32)
            o_ref[0, rows, :] = _rms(total, gf_ref[...]) if final else total

    @pl.when(t == 0)
    def _():
        h_scr[...] = _rms(x_ref[0], g_ref[...]).astype(BF16)
        produce(wa0_ref, wu0_ref, a0_scr, u0_scr)
        produce(wa1_ref, wu1_ref, a1_scr, u1_scr)
        consume(a0_scr, u0_scr, wc1_ref, bc1_ref, wd1_ref, first=True)

    @pl.when((t > 0) & (t < last))
    def _():
        produce(wa0_ref, wu0_ref, a0_scr, u0_scr)
        consume(a1_scr, u1_scr, wc0_ref, bc0_ref, wd0_ref)
        produce(wa1_ref, wu1_ref, a1_scr, u1_scr)
        consume(a0_scr, u0_scr, wc1_ref, bc1_ref, wd1_ref)

    @pl.when(t == last)
    def _():
        produce(wa0_ref, wu0_ref, a0_scr, u0_scr)
        consume(a1_scr, u1_scr, wc0_ref, bc0_ref, wd0_ref)
        consume(a0_scr, u0_scr, wc1_ref, bc1_ref, wd1_ref, final=True)


def _ffn(x3, g, w_up, w_conv, b_conv, w_down, g_final, tf):
    B, S, D = x3.shape
    FF = w_down.shape[0]
    nf = FF // tf
    assert nf % 2 == 1 and nf >= 3, "the two-chunk pipeline ends on a produce-one / consume-two step"
    steps = (nf + 1) // 2
    prod0 = lambda t: 2 * t
    prod1 = lambda t: jnp.minimum(2 * t + 1, nf - 1)
    cons0 = lambda t: jnp.maximum(2 * t - 1, 0)
    cons1 = lambda t: 2 * t
    col = lambda f, off=0: pl.BlockSpec((D, tf), lambda b, t: (0, off + f(t)))
    vec = lambda rows, f: pl.BlockSpec((rows, tf), lambda b, t: (0, f(t)))
    dn = lambda f: pl.BlockSpec((tf, D), lambda b, t: (f(t), 0))
    buf = pltpu.VMEM((S, tf), F32)
    return pl.pallas_call(
        _ffn_kernel,
        grid=(B, steps),
        in_specs=[
            pl.BlockSpec((1, S, D), lambda b, t: (b, 0, 0)),
            pl.BlockSpec((1, D), lambda b, t: (0, 0)),
            col(prod0), col(prod1), col(prod0, nf), col(prod1, nf),
            vec(3, cons0), vec(3, cons1), vec(1, cons0), vec(1, cons1),
            dn(cons0), dn(cons1),
            pl.BlockSpec((1, D), lambda b, t: (0, 0)),
        ],
        out_specs=pl.BlockSpec((1, S, D), lambda b, t: (b, 0, 0)),
        out_shape=jax.ShapeDtypeStruct((B, S, D), F32),
        scratch_shapes=[pltpu.VMEM((S, D), BF16), buf, buf, buf, buf],
        compiler_params=_params(("parallel", "arbitrary"), VMEM_LIMIT),
        name="ffn",
    )(x3, g, w_up, w_up, w_up, w_up, w_conv, w_conv, b_conv, b_conv, w_down, w_down, g_final)


def kernel(x, mem, mix_norm_g, w_in, b_ml_igate, b_ml_fgate, w_ml_conv, b_ml_conv, ml_norm_g, na_rpb, mem_norm_g, w_mem_kv, b_merge_gate, w_br_ml, w_br_na, w_br_xa, w_out, ffn_norm_g, w_ffn_up, w_ffn_conv, b_ffn_conv, w_ffn_down, final_norm_g):
    B, S, D = x.shape
    H = ML_HEADS
    T = B * S
    assert w_in.shape[0] == 1, "single-layer block: the FFN kernel also applies the final norm"
    assert S % ML_CHUNK == 0 and S % (GRID_W * NA_ROWS_PER_STEP) == 0
    l = 0
    n_gate = 4 * H
    g0 = 4 * ML_WIDTH
    row2 = lambda v: v.reshape(1, -1).astype(F32)

    w_main, w_gate = _wprep(jnp.transpose(w_in[l]), g0, n_gate, tr=1024)
    w_gate = jnp.transpose(w_gate.reshape(4, H, D), (1, 0, 2)).reshape(n_gate, D).astype(BF16)

    proj2, grow, w_kv_b, w_ml_b, w_na_b, w_xa_b, w_out_b = _inproj(
        x.reshape(T, D), row2(mix_norm_g[l]), w_main, w_gate,
        [w_mem_kv[l], w_br_ml[l], w_br_na[l], w_br_xa[l], w_out[l]], tm=1024, tn=3584)
    proj3 = proj2.reshape(B, S, -1)

    bias = jnp.stack([b_ml_igate[l][0], b_ml_fgate[l][0], b_ml_igate[l][1], b_ml_fgate[l][1]],
                     axis=-1).astype(F32)
    y_ml = _mlstm(proj3, grow.reshape(H, 4, T // ML_CHUNK, ML_CHUNK), bias[:, :, None],
                  w_ml_conv[l].astype(F32), row2(b_ml_conv[l]), row2(ml_norm_g[l]))

    nb = (4 * ML_WIDTH) // NA_WIDTH
    y_na, w_up_b, w_down_b = _na(proj3, _na_bias_patterns(na_rpb[l]), w_ffn_up[l], w_ffn_down[l],
                                 nb, nb + 1, nb + 2)

    xb = (4 * ML_WIDTH + 3 * NA_WIDTH) // XA_WIDTH
    y_xa = _xa(proj3, mem, row2(mem_norm_g[l]), w_kv_b, xb)

    gate_blk = (4 * ML_WIDTH + 3 * NA_WIDTH + XA_WIDTH) // D
    x1 = _merge(x.reshape(T, D), y_ml.reshape(T, -1), y_na.reshape(T, -1), y_xa.reshape(T, -1),
                proj2, row2(b_merge_gate[l]), w_ml_b, w_na_b, w_xa_b, w_out_b, gate_blk, tm=1024)

    return _ffn(x1.reshape(B, S, D), row2(ffn_norm_g[l]), w_up_b,
                w_ffn_conv[l].astype(F32), row2(b_ffn_conv[l]), w_down_b,
                row2(final_norm_g), tf=256)
```
